```python
import math
import jax, jax.numpy as jnp
from jax import lax
import numpy as np

D_MODEL = 1024
BATCH = 2
SEQ = 8192
DEPTH = 2
DEC_BATCH = 32
DEC_SEQ = 4
PAST_LEN = 16384
PAGE_SIZE = 128

HEAD_DIM = 64
GROUP_HEADS = 4
GROUP_WIDTH = GROUP_HEADS * HEAD_DIM
N_GROUPS = 4
MIX_WIDTH = N_GROUPS * GROUP_WIDTH
D_FF = -((-8 * D_MODEL) // (3 * 256)) * 256
ROPE_THETA = 10000.0
QBLK = 128
MOBA_BLOCK = 256
MOBA_TOPK = 3
GDN_CONV = 4
GDN_CHUNK = 64
MLSTM_CHUNK = 64
NORM_EPS = 1e-6
NEG = -1e30

PROJ_SPLITS = (
    ('fox_q', GROUP_WIDTH), ('fox_k', GROUP_WIDTH), ('fox_v', GROUP_WIDTH), ('fox_f', GROUP_HEADS),
    ('gdn_qkv', 3 * GROUP_WIDTH), ('gdn_a', GROUP_HEADS), ('gdn_b', GROUP_HEADS), ('gdn_g', GROUP_WIDTH),
    ('moba_q', GROUP_WIDTH), ('moba_k', GROUP_WIDTH), ('moba_v', GROUP_WIDTH),
    ('mlstm_q', GROUP_WIDTH), ('mlstm_k', GROUP_WIDTH), ('mlstm_v', GROUP_WIDTH),
    ('mlstm_i', GROUP_HEADS), ('mlstm_f', GROUP_HEADS), ('mlstm_o', GROUP_WIDTH),
)
D_PROJ = sum(w for _, w in PROJ_SPLITS)

kernel_name = 'hymba_fox_gdn_moba_mlstm_step'

F32 = jnp.float32


def split_proj(z):
    out = {}
    off = 0
    for name, w in PROJ_SPLITS:
        out[name] = z[..., off:off + w]
        off += w
    return out


def rms_norm(x, g):
    xf = x.astype(F32)
    y = xf * lax.rsqrt(jnp.mean(xf * xf, axis=-1, keepdims=True) + NORM_EPS)
    return (y * g.astype(F32)).astype(x.dtype)


def l2_normalize(x):
    xf = x.astype(F32)
    return xf * lax.rsqrt(jnp.sum(xf * xf, axis=-1, keepdims=True) + NORM_EPS)


def heads(z):
    return z.reshape(z.shape[:-1] + (GROUP_HEADS, HEAD_DIM))


def rotary(x, pos):
    half = HEAD_DIM // 2
    inv_freq = jnp.power(ROPE_THETA, -jnp.arange(half, dtype=F32) / half)
    ang = pos.astype(F32)[:, None] * inv_freq[None, :]
    cos = jnp.cos(ang)[None, :, None, :]
    sin = jnp.sin(ang)[None, :, None, :]
    xf = x.astype(F32)
    x1, x2 = xf[..., :half], xf[..., half:]
    return jnp.concatenate([x1 * cos - x2 * sin, x2 * cos + x1 * sin], axis=-1).astype(x.dtype)


def map_query_blocks(fn, qpos, *qs):
    T = qpos.shape[0]
    blk = QBLK if T % QBLK == 0 else T
    n = T // blk

    def to_blocks(a):
        a = a.reshape(a.shape[:2] + (n, blk) + a.shape[3:])
        return jnp.moveaxis(a, 2, 0)

    xs = (qpos.reshape(n, blk),) + tuple(to_blocks(a) for a in qs)
    out = lax.map(lambda a: fn(*a), xs)
    out = jnp.moveaxis(out, 0, 2)
    return out.reshape(out.shape[:2] + (T,) + out.shape[4:])


def fox_attend(q, cq, qpos, k, v, ck, kpos):
    s = jnp.einsum('bhqd,bhkd->bhqk', q, k).astype(F32) * (HEAD_DIM ** -0.5)
    s = s + cq[..., :, None] - ck[..., None, :]
    s = jnp.where(kpos[None, :] <= qpos[:, None], s, NEG)
    p = jax.nn.softmax(s, axis=-1).astype(v.dtype)
    return jnp.einsum('bhqk,bhkd->bhqd', p, v)


def fox_mixer(pr, pos, q_norm, k_norm, f_bias, past):
    B, T = pr['fox_q'].shape[:2]
    q = rms_norm(heads(pr['fox_q']), q_norm)
    k = rms_norm(heads(pr['fox_k']), k_norm)
    v = heads(pr['fox_v'])
    lf = jax.nn.log_sigmoid((pr['fox_f'] + f_bias).astype(F32))
    if past is None:
        k_all, v_all, lf_all = k, v, lf
    else:
        k_all = jnp.concatenate([past[0], k], axis=1)
        v_all = jnp.concatenate([past[1], v], axis=1)
        lf_all = jnp.concatenate([past[2].astype(F32), lf], axis=1)
    L = k_all.shape[1]
    c = jnp.cumsum(lf_all, axis=1).transpose(0, 2, 1)
    kh = k_all.transpose(0, 2, 1, 3)
    vh = v_all.transpose(0, 2, 1, 3)
    kpos = jnp.arange(L, dtype=jnp.int32)

    def attend(qp, qb, cb):
        return fox_attend(qb, cb, qp, kh, vh, c, kpos)

    o = map_query_blocks(attend, pos, q.transpose(0, 2, 1, 3), c[:, :, L - T:])
    return o.transpose(0, 2, 1, 3).reshape(B, T, GROUP_WIDTH), (k, v, lf)


def causal_short_conv(x, w, buf):
    T = x.shape[1]
    xp = jnp.concatenate([buf.astype(x.dtype), x], axis=1)
    y = w[0] * xp[:, 0:T]
    for j in range(1, GDN_CONV):
        y = y + w[j] * xp[:, j:j + T]
    return jax.nn.silu(y), xp[:, xp.shape[1] - (GDN_CONV - 1):]


def unit_lower_inverse(Lm):
    C = Lm.shape[-1]
    inv = jnp.eye(C, dtype=Lm.dtype) - Lm
    P = Lm
    for _ in range(max(0, (C - 1).bit_length() - 1)):
        P = P @ P
        inv = inv + inv @ P
    return inv


def gated_delta_chunked(q, k, v, g, beta, S0):
    B, H, T, DK = q.shape
    C = GDN_CHUNK if T % GDN_CHUNK == 0 else T
    N = T // C

    def chunks(a):
        return a.reshape((B, H, N, C) + a.shape[3:])

    q, k, v, g, beta = chunks(q), chunks(k), chunks(v), chunks(g), chunks(beta)
    g = jnp.cumsum(g, axis=-1)
    incl = jnp.tril(jnp.ones((C, C), bool))
    strict = jnp.tril(jnp.ones((C, C), bool), -1)
    decay = jnp.exp(jnp.where(incl, g[..., :, None] - g[..., None, :], NEG))
    kb = k * beta[..., None]
    Lm = jnp.where(strict, jnp.einsum('bhnid,bhnjd->bhnij', kb, k) * decay, 0.0)
    Tinv = unit_lower_inverse(Lm)
    u = Tinv @ (v * beta[..., None])
    w = Tinv @ (kb * jnp.exp(g)[..., None])
    attn = jnp.einsum('bhnid,bhnjd->bhnij', q, k) * decay
    q_dec = q * jnp.exp(g)[..., None]
    g_last = g[..., -1]
    k_tail = k * jnp.exp(g_last[..., None] - g)[..., None]

    def step(S, xs):
        q_c, u_c, w_c, a_c, kt_c, gl_c = xs
        v_new = u_c - w_c @ S
        o = q_c @ S + a_c @ v_new
        S = S * jnp.exp(gl_c)[..., None, None] + jnp.swapaxes(kt_c, -1, -2) @ v_new
        return S, o

    xs = tuple(jnp.moveaxis(a, 2, 0) for a in (q_dec, u, w, attn, k_tail, g_last))
    S, o = lax.scan(step, S0, xs)
    o = jnp.moveaxis(o, 0, 2).reshape(B, H, T, -1)
    return o, S


def gdn_mixer(pr, conv_w, a_log, dt_bias, norm_g, conv_buf, S0):
    B, T = pr['gdn_qkv'].shape[:2]
    qkv, conv_state = causal_short_conv(pr['gdn_qkv'], conv_w, conv_buf)
    q = l2_normalize(heads(qkv[..., :GROUP_WIDTH])) * (HEAD_DIM ** -0.5)
    k = l2_normalize(heads(qkv[..., GROUP_WIDTH:2 * GROUP_WIDTH]))
    v = heads(qkv[..., 2 * GROUP_WIDTH:]).astype(F32)
    beta = jax.nn.sigmoid(pr['gdn_b'].astype(F32))
    g = -jnp.exp(a_log.astype(F32)) * jax.nn.softplus(pr['gdn_a'].astype(F32) + dt_bias.astype(F32))
    o, S = gated_delta_chunked(q.transpose(0, 2, 1, 3), k.transpose(0, 2, 1, 3), v.transpose(0, 2, 1, 3),
                               g.transpose(0, 2, 1), beta.transpose(0, 2, 1), S0.astype(F32))
    o = rms_norm(o.transpose(0, 2, 1, 3), norm_g) * jax.nn.silu(heads(pr['gdn_g']).astype(F32))
    return o.reshape(B, T, GROUP_WIDTH), (conv_state, S)


def moba_attend(qpos, q, kb, vb, kmean):
    B, H, NB = kb.shape[:3]
    Q = q.shape[2]
    qblk = qpos // MOBA_BLOCK
    gs = jnp.einsum('bhqd,bhnd->bhqn', q, kmean).astype(F32)
    past_ok = jnp.arange(NB, dtype=jnp.int32)[None, :] < qblk[:, None]
    gs = jnp.where(past_ok, gs, NEG)
    _, sel = lax.top_k(gs, min(MOBA_TOPK, NB))
    sel_ok = sel < qblk[:, None]
    own = jnp.broadcast_to(qblk[None, None, :, None], sel.shape[:3] + (1,))
    idx = jnp.concatenate([sel, own], axis=-1)
    ok = jnp.concatenate([sel_ok, jnp.ones(sel.shape[:3] + (1,), bool)], axis=-1)
    bi = jnp.arange(B)[:, None, None, None]
    hi = jnp.arange(H)[None, :, None, None]
    kg = kb[bi, hi, idx]
    vg = vb[bi, hi, idx]
    kpos = idx[..., None] * MOBA_BLOCK + jnp.arange(MOBA_BLOCK, dtype=jnp.int32)
    mask = ok[..., None] & (kpos <= qpos[None, None, :, None, None])
    s = jnp.einsum('bhqd,bhqjkd->bhqjk', q, kg).astype(F32) * (HEAD_DIM ** -0.5)
    s = jnp.where(mask, s, NEG)
    p = jax.nn.softmax(s.reshape(B, H, Q, -1), axis=-1).reshape(s.shape).astype(vg.dtype)
    return jnp.einsum('bhqjk,bhqjkd->bhqd', p, vg)


def moba_mixer(pr, pos, q_norm, k_norm, past):
    B, T = pr['moba_q'].shape[:2]
    q = rotary(rms_norm(heads(pr['moba_q']), q_norm), pos)
    k = rotary(rms_norm(heads(pr['moba_k']), k_norm), pos)
    v = heads(pr['moba_v'])
    if past is None:
        k_all, v_all = k, v
    else:
        k_all = jnp.concatenate([past[0], k], axis=1)
        v_all = jnp.concatenate([past[1], v], axis=1)
    L = k_all.shape[1]
    NB = -(-L // MOBA_BLOCK)
    pad = NB * MOBA_BLOCK - L

    def blocks(a):
        a = jnp.pad(a, ((0, 0), (0, pad), (0, 0), (0, 0)))
        return a.reshape(B, NB, MOBA_BLOCK, GROUP_HEADS, HEAD_DIM).transpose(0, 3, 1, 2, 4)

    kb, vb = blocks(k_all), blocks(v_all)
    kmean = jnp.mean(kb.astype(F32), axis=3).astype(kb.dtype)

    def attend(qp, qb):
        return moba_attend(qp, qb, kb, vb, kmean)

    o = map_query_blocks(attend, pos, q.transpose(0, 2, 1, 3))
    return o.transpose(0, 2, 1, 3).reshape(B, T, GROUP_WIDTH), (k, v)


def mlstm_chunked(q, k, v, ig, lf, C0, n0, m0):
    B, H, T, DK = q.shape
    C = MLSTM_CHUNK if T % MLSTM_CHUNK == 0 else T
    N = T // C

    def chunks(a):
        return a.reshape((B, H, N, C) + a.shape[3:])

    q, k, v, ig, lf = chunks(q), chunks(k), chunks(v), chunks(ig), chunks(lf)
    b = jnp.cumsum(lf, axis=-1)
    incl = jnp.tril(jnp.ones((C, C), bool))
    Dm = jnp.where(incl, b[..., :, None] - b[..., None, :] + ig[..., None, :], NEG)
    m_intra = jnp.max(Dm, axis=-1)
    qk = jnp.einsum('bhnid,bhnjd->bhnij', q, k)
    a_end = b[..., -1:] - b + ig
    b_last = b[..., -1]

    def step(carry, xs):
        Cs, ns, m = carry
        q_c, k_c, v_c, b_c, D_c, mi_c, qk_c, ae_c, bl_c = xs
        m_t = jnp.maximum(b_c + m[..., None], mi_c)
        inter = jnp.exp(b_c + m[..., None] - m_t)
        W = jnp.exp(D_c - m_t[..., None]) * qk_c
        num = inter[..., None] * (q_c @ Cs) + W @ v_c
        den = inter * jnp.einsum('bhcd,bhd->bhc', q_c, ns) + jnp.sum(W, axis=-1)
        h = num / jnp.maximum(jnp.abs(den), jnp.exp(-m_t))[..., None]
        m_new = jnp.maximum(bl_c + m, jnp.max(ae_c, axis=-1))
        wt = jnp.exp(ae_c - m_new[..., None])
        sc = jnp.exp(bl_c + m - m_new)
        Cs = sc[..., None, None] * Cs + jnp.einsum('bhc,bhck,bhcv->bhkv', wt, k_c, v_c)
        ns = sc[..., None] * ns + jnp.einsum('bhc,bhck->bhk', wt, k_c)
        return (Cs, ns, m_new), h

    xs = tuple(jnp.moveaxis(a, 2, 0) for a in (q, k, v, b, Dm, m_intra, qk, a_end, b_last))
    (Cs, ns, m), h = lax.scan(step, (C0, n0, m0), xs)
    h = jnp.moveaxis(h, 0, 2).reshape(B, H, T, -1)
    return h, (Cs, ns, m)


def mlstm_mixer(pr, i_bias, f_bias, norm_g, C0, n0, m0):
    B, T = pr['mlstm_q'].shape[:2]
    q = heads(pr['mlstm_q']).astype(F32).transpose(0, 2, 1, 3)
    k = (heads(pr['mlstm_k']).astype(F32) * (HEAD_DIM ** -0.5)).transpose(0, 2, 1, 3)
    v = heads(pr['mlstm_v']).astype(F32).transpose(0, 2, 1, 3)
    ig = (pr['mlstm_i'] + i_bias).astype(F32).transpose(0, 2, 1)
    lf = jax.nn.log_sigmoid((pr['mlstm_f'] + f_bias).astype(F32)).transpose(0, 2, 1)
    h, state = mlstm_chunked(q, k, v, ig, lf, C0.astype(F32), n0.astype(F32), m0.astype(F32))
    h = rms_norm(h.transpose(0, 2, 1, 3), norm_g) * jax.nn.sigmoid(heads(pr['mlstm_o']).astype(F32))
    return h.reshape(B, T, GROUP_WIDTH), state


def trunk_layer(x, pos, fox_past, moba_past, gdn_conv, gdn_s, ml_c, ml_n, ml_m, lp):
    dt = x.dtype
    h = rms_norm(x, lp['ln1'])
    pr = split_proj(h @ lp['w_in'])
    o_a, fox_rows = fox_mixer(pr, pos, lp['fox_q_norm'], lp['fox_k_norm'], lp['fox_f_bias'], fox_past)
    o_b, gdn_state = gdn_mixer(pr, lp['gdn_conv_w'], lp['gdn_a_log'], lp['gdn_dt_bias'], lp['gdn_norm'],
                               gdn_conv, gdn_s)
    o_c, moba_rows = moba_mixer(pr, pos, lp['moba_q_norm'], lp['moba_k_norm'], moba_past)
    o_d, ml_state = mlstm_mixer(pr, lp['mlstm_i_bias'], lp['mlstm_f_bias'], lp['mlstm_norm'], ml_c, ml_n, ml_m)
    mixed = jnp.concatenate([o_a.astype(dt), o_b.astype(dt), o_c.astype(dt), o_d.astype(dt)], axis=-1)
    x = x + mixed @ lp['w_out']
    h2 = rms_norm(x, lp['ln2'])
    x = x + (jax.nn.silu(h2 @ lp['w_gate']) * (h2 @ lp['w_up'])) @ lp['w_down']
    return x, fox_rows + moba_rows + gdn_state + ml_state


def gather_pages(pool, page_table):
    g = pool[page_table]
    return g.reshape((g.shape[0], g.shape[1] * g.shape[2]) + g.shape[3:])


def setup_inputs(seed: int = 0) -> dict:
    key = jax.random.key(seed)
    keys = iter(jax.random.split(key, 48))

    def nrm(shape, scale=1.0):
        return scale * jax.random.normal(next(keys), shape, F32)

    n_pages = PAST_LEN // PAGE_SIZE
    n_used = DEC_BATCH * n_pages
    n_pool = n_used + n_used // 4
    kv_shape = (DEPTH, n_pool, PAGE_SIZE, GROUP_HEADS, HEAD_DIM)
    H = GROUP_HEADS
    dt_init = jnp.exp(jax.random.uniform(next(keys), (DEPTH, H), F32, math.log(1e-3), math.log(1e-1)))
    return {
        'x_prompt': nrm((BATCH, SEQ, D_MODEL)),
        'x_sample': nrm((DEC_BATCH, DEC_SEQ, D_MODEL)),
        'cache_fox_k': nrm(kv_shape),
        'cache_fox_v': nrm(kv_shape),
        'cache_fox_logf': jax.nn.log_sigmoid(3.0 + nrm((DEPTH, n_pool, PAGE_SIZE, H))),
        'cache_moba_k': nrm(kv_shape),
        'cache_moba_v': nrm(kv_shape),
        'state_gdn_conv': nrm((DEPTH, DEC_BATCH, GDN_CONV - 1, 3 * GROUP_WIDTH)),
        'state_gdn_s': nrm((DEPTH, DEC_BATCH, H, HEAD_DIM, HEAD_DIM), 0.1),
        'state_mlstm_c': nrm((DEPTH, DEC_BATCH, H, HEAD_DIM, HEAD_DIM), 0.1),
        'state_mlstm_n': nrm((DEPTH, DEC_BATCH, H, HEAD_DIM), 0.1),
        'state_mlstm_m': nrm((DEPTH, DEC_BATCH, H)),
        'page_table': jax.random.permutation(next(keys), n_pool)[:n_used].reshape(DEC_BATCH, n_pages).astype(jnp.int32),
        'ln1': 1.0 + nrm((DEPTH, D_MODEL), 0.02),
        'w_in': nrm((DEPTH, D_MODEL, D_PROJ), D_MODEL ** -0.5),
        'fox_q_norm': 1.0 + nrm((DEPTH, HEAD_DIM), 0.02),
        'fox_k_norm': 1.0 + nrm((DEPTH, HEAD_DIM), 0.02),
        'fox_f_bias': 3.0 + nrm((DEPTH, H), 0.5),
        'gdn_conv_w': nrm((DEPTH, GDN_CONV, 3 * GROUP_WIDTH), GDN_CONV ** -0.5),
        'gdn_a_log': jnp.log(jax.random.uniform(next(keys), (DEPTH, H), F32, 1.0, 16.0)),
        'gdn_dt_bias': dt_init + jnp.log(-jnp.expm1(-dt_init)),
        'gdn_norm': 1.0 + nrm((DEPTH, HEAD_DIM), 0.02),
        'moba_q_norm': 1.0 + nrm((DEPTH, HEAD_DIM), 0.02),
        'moba_k_norm': 1.0 + nrm((DEPTH, HEAD_DIM), 0.02),
        'mlstm_i_bias': nrm((DEPTH, H), 0.1),
        'mlstm_f_bias': 3.0 + nrm((DEPTH, H), 0.5),
        'mlstm_norm': 1.0 + nrm((DEPTH, HEAD_DIM), 0.02),
        'w_out': nrm((DEPTH, MIX_WIDTH, D_MODEL), MIX_WIDTH ** -0.5),
        'ln2': 1.0 + nrm((DEPTH, D_MODEL), 0.02),
        'w_gate': nrm((DEPTH, D_MODEL, D_FF), D_MODEL ** -0.5),
        'w_up': nrm((DEPTH, D_MODEL, D_FF), D_MODEL ** -0.5),
        'w_down': nrm((DEPTH, D_FF, D_MODEL), D_FF ** -0.5),
    }


def reference(x_prompt, x_sample, cache_fox_k, cache_fox_v, cache_fox_logf, cache_moba_k, cache_moba_v,
              state_gdn_conv, state_gdn_s, state_mlstm_c, state_mlstm_n, state_mlstm_m, page_table,
              ln1, w_in, fox_q_norm, fox_k_norm, fox_f_bias, gdn_conv_w, gdn_a_log, gdn_dt_bias, gdn_norm,
              moba_q_norm, moba_k_norm, mlstm_i_bias, mlstm_f_bias, mlstm_norm, w_out, ln2,
              w_gate, w_up, w_down):
    B, T = x_prompt.shape[:2]
    TS = x_sample.shape[1]
    past_len = page_table.shape[1] * cache_fox_k.shape[2]
    pos_p = jnp.arange(T, dtype=jnp.int32)
    pos_s = past_len + jnp.arange(TS, dtype=jnp.int32)
    H = GROUP_HEADS
    y_prompt, y_sample = x_prompt, x_sample
    p_states, s_states = [], []
    for i in range(DEPTH):
        lp = {
            'ln1': ln1[i], 'w_in': w_in[i], 'fox_q_norm': fox_q_norm[i], 'fox_k_norm': fox_k_norm[i],
            'fox_f_bias': fox_f_bias[i], 'gdn_conv_w': gdn_conv_w[i], 'gdn_a_log': gdn_a_log[i],
            'gdn_dt_bias': gdn_dt_bias[i], 'gdn_norm': gdn_norm[i], 'moba_q_norm': moba_q_norm[i],
            'moba_k_norm': moba_k_norm[i], 'mlstm_i_bias': mlstm_i_bias[i], 'mlstm_f_bias': mlstm_f_bias[i],
            'mlstm_norm': mlstm_norm[i], 'w_out': w_out[i], 'ln2': ln2[i], 'w_gate': w_gate[i],
            'w_up': w_up[i], 'w_down': w_down[i],
        }
        y_prompt, st_p = trunk_layer(
            y_prompt, pos_p, None, None,
            jnp.zeros((B, GDN_CONV - 1, 3 * GROUP_WIDTH), x_prompt.dtype),
            jnp.zeros((B, H, HEAD_DIM, HEAD_DIM), F32),
            jnp.zeros((B, H, HEAD_DIM, HEAD_DIM), F32),
            jnp.zeros((B, H, HEAD_DIM), F32),
            jnp.zeros((B, H), F32), lp)
        fox_past = (gather_pages(cache_fox_k[i], page_table), gather_pages(cache_fox_v[i], page_table),
                    gather_pages(cache_fox_logf[i], page_table))
        moba_past = (gather_pages(cache_moba_k[i], page_table), gather_pages(cache_moba_v[i], page_table))
        y_sample, st_s = trunk_layer(
            y_sample, pos_s, fox_past, moba_past, state_gdn_conv[i], state_gdn_s[i],
            state_mlstm_c[i], state_mlstm_n[i], state_mlstm_m[i], lp)
        p_states.append(st_p)
        s_states.append(st_s)
    (p_fox_k, p_fox_v, p_fox_logf, p_moba_k, p_moba_v, p_gdn_conv, p_gdn_s,
     p_mlstm_c, p_mlstm_n, p_mlstm_m) = [jnp.stack(a) for a in zip(*p_states)]
    (s_fox_k, s_fox_v, s_fox_logf, s_moba_k, s_moba_v, s_gdn_conv, s_gdn_s,
     s_mlstm_c, s_mlstm_n, s_mlstm_m) = [jnp.stack(a) for a in zip(*s_states)]
    return (y_prompt, y_sample,
            p_fox_k, p_fox_v, p_fox_logf, p_moba_k, p_moba_v, p_gdn_conv, p_gdn_s,
            p_mlstm_c, p_mlstm_n, p_mlstm_m,
            s_fox_k, s_fox_v, s_fox_logf, s_moba_k, s_moba_v, s_gdn_conv, s_gdn_s,
            s_mlstm_c, s_mlstm_n, s_mlstm_m)
```

```python
import functools
import math

import jax
import jax.numpy as jnp
from jax import lax
from jax.experimental import pallas as pl
from jax.experimental.pallas import tpu as pltpu

F32 = jnp.float32
BF16 = jnp.bfloat16

HEAD_DIM = 64
N_HEADS = 4
GROUP_WIDTH = N_HEADS * HEAD_DIM
LANES = 128
SUBLANES = 8
MOBA_BLOCK = 256
MOBA_TOPK = 3
MOBA_SEL_LANES = 32
GDN_CONV = 4
CHUNK = 64
NORM_EPS = 1e-6
ROPE_THETA = 10000.0
NEG = -1e30
QK_SCALE = HEAD_DIM ** -0.5
VMEM_LIMIT = 48 * 1024 * 1024

ZB_FOX_Q, ZB_FOX_K, ZB_FOX_V = 0, 1, 2
ZB_GDN_QKV = 3
ZB_GDN_G = 6
ZB_MOBA_Q, ZB_MOBA_K, ZB_MOBA_V = 7, 8, 9
ZB_ML_Q, ZB_ML_K, ZB_ML_V, ZB_ML_O = 10, 11, 12, 13
Z_WIDE = 14 * GROUP_WIDTH
Z_WIDTH = Z_WIDE + LANES
ZB_GATES = Z_WIDE // LANES
GL_FOX_F, GL_GDN_A, GL_GDN_B, GL_ML_I, GL_ML_F = 0, 4, 8, 12, 16


def _dot(a, b):
    return jnp.dot(a, b, preferred_element_type=F32)


def _dot_nt(a, b):
    return lax.dot_general(a, b, (((1,), (1,)), ((), ())), preferred_element_type=F32)


def _dot_tn(a, b):
    return lax.dot_general(a, b, (((0,), (0,)), ((), ())), preferred_element_type=F32)


def _split3(x):
    hi = x.astype(BF16)
    r1 = x - hi.astype(F32)
    mid = r1.astype(BF16)
    lo = (r1 - mid.astype(F32)).astype(BF16)
    return hi, mid, lo


def _dot3_rhs(a_bf16, x):
    hi, mid, lo = _split3(x)
    return _dot(a_bf16, hi) + _dot(a_bf16, mid) + _dot(a_bf16, lo)


def _dot3_lhs(x, b_bf16):
    hi, mid, lo = _split3(x)
    return _dot(hi, b_bf16) + _dot(mid, b_bf16) + _dot(lo, b_bf16)


def _sigmoid(x):
    return 1.0 / (1.0 + jnp.exp(-x))


def _log_sigmoid(x):
    return jnp.minimum(x, 0.0) - jnp.log1p(jnp.exp(-jnp.abs(x)))


def _softplus(x):
    return jnp.maximum(x, 0.0) + jnp.log1p(jnp.exp(-jnp.abs(x)))


def _row_of_col(col, n):
    r = lax.broadcasted_iota(jnp.int32, (n, n), 0)
    c = lax.broadcasted_iota(jnp.int32, (n, n), 1)
    return jnp.sum(jnp.where(r == c, col, 0.0), axis=0, keepdims=True)


def _params(sem):
    return pltpu.CompilerParams(dimension_semantics=sem, vmem_limit_bytes=VMEM_LIMIT)


def _const_spec(shape):
    nd = len(shape)
    return pl.BlockSpec(shape, lambda *_: (0,) * nd, pipeline_mode=pl.Buffered(1))


PROJ_COL_CHUNK = 512


def _proj_kernel(x_ref, g_ref, w_ref, z_ref):
    x = x_ref[...]
    ms = jnp.mean(x * x, axis=-1, keepdims=True)
    h = (x * lax.rsqrt(ms + NORM_EPS) * g_ref[...]).astype(BF16)
    for c0 in range(0, Z_WIDTH, PROJ_COL_CHUNK):
        c1 = min(c0 + PROJ_COL_CHUNK, Z_WIDTH)
        z_ref[:, c0:c1] = _dot(h, w_ref[:, c0:c1])


def _proj(x2d, g, w_packed, tm):
    n, d = x2d.shape
    return pl.pallas_call(
        _proj_kernel,
        grid=(n // tm,),
        in_specs=[pl.BlockSpec((tm, d), lambda i: (i, 0)),
                  _const_spec((1, d)),
                  _const_spec((d, Z_WIDTH))],
        out_specs=pl.BlockSpec((tm, Z_WIDTH), lambda i: (i, 0)),
        out_shape=jax.ShapeDtypeStruct((n, Z_WIDTH), F32),
        compiler_params=_params(("arbitrary",)),
        name="norm_proj",
    )(x2d, g.reshape(1, d), w_packed)


FFN_COL_CHUNK = 256


def _out_ffn_kernel(x_ref, oa_ref, ob_ref, oc_ref, od_ref, wo_ref, g2_ref, wg_ref, wu_ref, wd_ref, y_ref):
    x1 = x_ref[...]
    for gi, o_ref in enumerate((oa_ref, ob_ref, oc_ref, od_ref)):
        x1 = x1 + _dot(o_ref[...].astype(BF16), wo_ref[gi * GROUP_WIDTH:(gi + 1) * GROUP_WIDTH, :])
    ms = jnp.mean(x1 * x1, axis=-1, keepdims=True)
    h2 = (x1 * lax.rsqrt(ms + NORM_EPS) * g2_ref[...]).astype(BF16)
    y_ref[...] = x1
    d_ff = wg_ref.shape[1]
    for c0 in range(0, d_ff, FFN_COL_CHUNK):
        c1 = c0 + FFN_COL_CHUNK
        gate = _dot(h2, wg_ref[:, c0:c1])
        up = _dot(h2, wu_ref[:, c0:c1])
        act = (gate * _sigmoid(gate) * up).astype(BF16)
        y_ref[...] += _dot(act, wd_ref[c0:c1, :])


def _out_ffn(x2d, mix, wo, g2, wg, wu, wd, tm):
    n, d = x2d.shape
    d_ff = wg.shape[1]
    assert d_ff % FFN_COL_CHUNK == 0
    row = lambda w: pl.BlockSpec((tm, w), lambda i: (i, 0))
    return pl.pallas_call(
        _out_ffn_kernel,
        grid=(n // tm,),
        in_specs=[row(d)] + [row(GROUP_WIDTH)] * 4 + [
            _const_spec((d, d)), _const_spec((1, d)), _const_spec((d, d_ff)),
            _const_spec((d, d_ff)), _const_spec((d_ff, d))],
        out_specs=row(d),
        out_shape=jax.ShapeDtypeStruct((n, d), F32),
        compiler_params=_params(("arbitrary",)),
        name="out_ffn",
    )(x2d, *mix, wo, g2.reshape(1, d), wg, wu, wd)


def _lane_iota(rows):
    return lax.broadcasted_iota(jnp.int32, (rows, LANES), 1)


def _head_rms(x, w, gsum):
    y = x * x
    hi = y.astype(BF16)
    lo = (y - hi.astype(F32)).astype(BF16)
    ss = _dot(hi, gsum) + _dot(lo, gsum)
    return x * lax.rsqrt(ss * (1.0 / HEAD_DIM) + NORM_EPS) * w


def _rotary(x, cos, sin_signed):
    half = HEAD_DIM // 2
    out = []
    for p in range(GROUP_WIDTH // LANES):
        sl = slice(p * LANES, (p + 1) * LANES)
        v = x[:, sl]
        lane = _lane_iota(v.shape[0])
        partner = jnp.where(lane % HEAD_DIM < half,
                            pltpu.roll(v, LANES - half, 1), pltpu.roll(v, half, 1))
        out.append(v * cos[:, sl] + partner * sin_signed[:, sl])
    return jnp.concatenate(out, axis=-1)


def _head_group(x, h, aug):
    p, e = divmod(h, 2)
    piece = x[:, p * LANES:(p + 1) * LANES]
    if e:
        piece = pltpu.roll(piece, HEAD_DIM, 1)
    return jnp.where(_lane_iota(x.shape[0]) < HEAD_DIM, piece, aug)


def _prep_prompt_kernel(zfq, zfk, zfv, zmq, zmk, zmv, zg, fqn, fkn, mqn, mkn, fb, cos, sin, gsum, tri,
                        fqa, fka, fva, fk32, flf, mqa, mka, mva, mk32, carry_ref, kmean_ref):
    j = pl.program_id(1)
    tm = zfq.shape[0]
    lane = _lane_iota(tm)
    g = gsum[...]

    @pl.when(j == 0)
    def _():
        carry_ref[...] = jnp.zeros_like(carry_ref)
        kmean_ref[...] = jnp.zeros_like(kmean_ref)

    fq = _head_rms(zfq[...], fqn[...], g) * QK_SCALE
    fk = _head_rms(zfk[...], fkn[...], g)
    fk32[...] = fk
    lf = _log_sigmoid(zg[...] + fb[...])
    flf[...] = lf
    c = _dot3_rhs(tri[...], lf) + carry_ref[...]
    carry_ref[...] = c[tm - 1:tm, :]
    fv = zfv[...]
    for h in range(N_HEADS):
        col = c[:, GL_FOX_F + h:GL_FOX_F + h + 1]
        hi = col.astype(BF16).astype(F32)
        r1 = col - hi
        mid = r1.astype(BF16).astype(F32)
        lo = (r1 - mid).astype(BF16).astype(F32)
        base = HEAD_DIM
        one_q = jnp.where((lane >= base + 3) & (lane < base + 6), 1.0, 0.0)
        aug_q = jnp.where(lane == base, hi, jnp.where(lane == base + 1, mid, jnp.where(lane == base + 2, lo, one_q)))
        one_k = jnp.where((lane >= base) & (lane < base + 3), 1.0, 0.0)
        aug_k = jnp.where(lane == base + 3, -hi, jnp.where(lane == base + 4, -mid, jnp.where(lane == base + 5, -lo, one_k)))
        aug_v = jnp.where(lane == base, 1.0, 0.0)
        sl = slice(h * LANES, (h + 1) * LANES)
        fqa[:, sl] = _head_group(fq, h, aug_q).astype(BF16)
        fka[:, sl] = _head_group(fk, h, aug_k).astype(BF16)
        fva[:, sl] = _head_group(fv, h, aug_v).astype(BF16)

    mq = _rotary(_head_rms(zmq[...], mqn[...], g), cos[...], sin[...])
    mk = _rotary(_head_rms(zmk[...], mkn[...], g), cos[...], sin[...])
    mk32[...] = mk
    mv = zmv[...]
    km = kmean_ref[...]
    lane_w = lax.broadcasted_iota(jnp.int32, km.shape, 1)
    kbd = jnp.concatenate([jnp.where(lane_w // HEAD_DIM == h, km, 0.0) for h in range(N_HEADS)], axis=0)
    qh = mq.astype(BF16)
    ql = (mq - qh.astype(F32)).astype(BF16)
    kh = kbd.astype(BF16)
    kl = (kbd - kh.astype(F32)).astype(BF16)
    gate = _dot_nt(qh, kh) + _dot_nt(qh, kl) + _dot_nt(ql, kh)
    blk = lane % MOBA_SEL_LANES
    gate = jnp.where(blk < j, gate, NEG)
    sel = jnp.zeros((tm, LANES), F32)
    for h in range(N_HEADS):
        gh = jnp.where(lane // MOBA_SEL_LANES == h, gate, -jnp.inf)
        for _ in range(MOBA_TOPK):
            mx = jnp.max(gh, axis=-1, keepdims=True)
            first = jnp.min(jnp.where(gh == mx, lane, 2 * LANES), axis=-1, keepdims=True)
            pick = lane == first
            sel = jnp.where(pick, 1.0, sel)
            gh = jnp.where(pick, -jnp.inf, gh)
    pen = jnp.where((sel > 0.0) | (blk == j), 0.0, NEG)
    in_aug = (lane >= HEAD_DIM) & (lane < HEAD_DIM + MOBA_SEL_LANES)
    aug_k = jnp.where(in_aug & (lane - HEAD_DIM == j), 1.0, 0.0)
    aug_v = jnp.where(lane == HEAD_DIM, 1.0, 0.0)
    mqs = mq * QK_SCALE
    for h in range(N_HEADS):
        shift = (HEAD_DIM - MOBA_SEL_LANES * h) % LANES
        moved = pltpu.roll(pen, shift, 1) if shift else pen
        aug_q = jnp.where(in_aug, moved, 0.0)
        sl = slice(h * LANES, (h + 1) * LANES)
        mqa[:, sl] = _head_group(mqs, h, aug_q).astype(BF16)
        mka[:, sl] = _head_group(mk, h, aug_k).astype(BF16)
        mva[:, sl] = _head_group(mv, h, aug_v).astype(BF16)
    row = lax.broadcasted_iota(jnp.int32, km.shape, 0)
    kmean_ref[...] = jnp.where(row == j, jnp.mean(mk, axis=0, keepdims=True), km)


def _prep_prompt(z, nb, t, norms, fbias, cos, sin, gsum, tri):
    tm = MOBA_BLOCK
    nt = t // tm
    assert t % tm == 0 and nt <= MOBA_SEL_LANES
    n = nb * t
    zb = lambda blk: pl.BlockSpec((tm, GROUP_WIDTH), lambda b, j: (b * nt + j, blk))
    rowspec = lambda w: pl.BlockSpec((tm, w), lambda b, j: (b * nt + j, 0))
    wide = 4 * LANES
    outs = [(wide, BF16)] * 3 + [(GROUP_WIDTH, F32), (LANES, F32)] + [(wide, BF16)] * 3 + [(GROUP_WIDTH, F32)]
    return pl.pallas_call(
        _prep_prompt_kernel,
        grid=(nb, nt),
        in_specs=[zb(ZB_FOX_Q), zb(ZB_FOX_K), zb(ZB_FOX_V), zb(ZB_MOBA_Q), zb(ZB_MOBA_K), zb(ZB_MOBA_V),
                  pl.BlockSpec((tm, LANES), lambda b, j: (b * nt + j, ZB_GATES))]
                 + [_const_spec((1, GROUP_WIDTH))] * 4 + [_const_spec((1, LANES))]
                 + [pl.BlockSpec((tm, GROUP_WIDTH), lambda b, j: (j, 0))] * 2
                 + [_const_spec((GROUP_WIDTH, GROUP_WIDTH)), _const_spec((tm, tm))],
        out_specs=[rowspec(w) for w, _ in outs],
        out_shape=[jax.ShapeDtypeStruct((n, w), dt) for w, dt in outs],
        scratch_shapes=[pltpu.VMEM((1, LANES), F32), pltpu.VMEM((MOBA_SEL_LANES, GROUP_WIDTH), F32)],
        compiler_params=_params(("arbitrary", "arbitrary")),
        name="prep_prompt",
    )(z, z, z, z, z, z, z, *norms, fbias, cos, sin, gsum, tri)


def _prep_sample_kernel(zfq, zfk, zmq, zmk, zg, fqn, fkn, mqn, mkn, fb, cos, sin, gsum,
                        fq32, fk32, flf, mq32, mk32):
    g = gsum[...]
    fq32[...] = _head_rms(zfq[...], fqn[...], g) * QK_SCALE
    fk32[...] = _head_rms(zfk[...], fkn[...], g)
    flf[...] = _log_sigmoid(zg[...] + fb[...])
    mq32[...] = _rotary(_head_rms(zmq[...], mqn[...], g), cos[...], sin[...]) * QK_SCALE
    mk32[...] = _rotary(_head_rms(zmk[...], mkn[...], g), cos[...], sin[...])


def _prep_sample(z, norms, fbias, cos, sin, gsum):
    n = z.shape[0]
    zb = lambda blk: pl.BlockSpec((n, GROUP_WIDTH), lambda i: (0, blk))
    full = lambda w: pl.BlockSpec((n, w), lambda i: (0, 0))
    outs = [GROUP_WIDTH, GROUP_WIDTH, LANES, GROUP_WIDTH, GROUP_WIDTH]
    return pl.pallas_call(
        _prep_sample_kernel,
        grid=(1,),
        in_specs=[zb(ZB_FOX_Q), zb(ZB_FOX_K), zb(ZB_MOBA_Q), zb(ZB_MOBA_K),
                  pl.BlockSpec((n, LANES), lambda i: (0, ZB_GATES))]
                 + [full(GROUP_WIDTH)] * 0 + [_const_spec((1, GROUP_WIDTH))] * 4 + [_const_spec((1, LANES))]
                 + [full(GROUP_WIDTH)] * 2 + [_const_spec((GROUP_WIDTH, GROUP_WIDTH))],
        out_specs=[full(w) for w in outs],
        out_shape=[jax.ShapeDtypeStruct((n, w), F32) for w in outs],
        compiler_params=_params(("arbitrary",)),
        name="prep_sample",
    )(z, z, z, z, z, *norms, fbias, cos, sin, gsum)


def _flash_kernel(qi_ref, kj_ref, q_ref, k_ref, v_ref, o_ref, m_ref, acc_ref):
    s_idx = pl.program_id(1)
    i = qi_ref[s_idx]
    j = kj_ref[s_idx]
    tq = q_ref.shape[0]

    @pl.when(j == 0)
    def _():
        m_ref[...] = jnp.full_like(m_ref, NEG)
        acc_ref[...] = jnp.zeros_like(acc_ref)

    def step(diagonal):
        for h in range(N_HEADS):
            sl = slice(h * LANES, (h + 1) * LANES)
            s = _dot_nt(q_ref[:, sl], k_ref[:, sl])
            if diagonal:
                r = lax.broadcasted_iota(jnp.int32, s.shape, 0)
                c = lax.broadcasted_iota(jnp.int32, s.shape, 1)
                s = jnp.where(r >= c, s, NEG)
            m_prev = m_ref[h][:, :1]
            m_new = jnp.maximum(m_prev, jnp.max(s, axis=-1, keepdims=True))
            p = jnp.exp(s - m_new).astype(BF16)
            acc = acc_ref[h] * jnp.exp(m_prev - m_new) + _dot(p, v_ref[:, sl])
            acc_ref[h] = acc
            m_ref[h] = jnp.broadcast_to(m_new, (tq, LANES))
            if diagonal:
                o_ref[:, h * HEAD_DIM:(h + 1) * HEAD_DIM] = acc[:, :HEAD_DIM] / acc[:, HEAD_DIM:HEAD_DIM + 1]

    @pl.when(j < i)
    def _():
        step(False)

    @pl.when(j == i)
    def _():
        step(True)


def _flash(qa, ka, va, nb, t, ta):
    nt = t // ta
    assert t % ta == 0
    pairs = [(i, j) for i in range(nt) for j in range(i + 1)]
    qi = jnp.asarray([p[0] for p in pairs], jnp.int32)
    kj = jnp.asarray([p[1] for p in pairs], jnp.int32)
    wide = 4 * LANES
    qspec = pl.BlockSpec((ta, wide), lambda b, s, qi, kj: (b * nt + qi[s], 0))
    kspec = pl.BlockSpec((ta, wide), lambda b, s, qi, kj: (b * nt + kj[s], 0))
    return pl.pallas_call(
        _flash_kernel,
        grid_spec=pltpu.PrefetchScalarGridSpec(
            num_scalar_prefetch=2,
            grid=(nb, len(pairs)),
            in_specs=[qspec, kspec, kspec],
            out_specs=pl.BlockSpec((ta, GROUP_WIDTH), lambda b, s, qi, kj: (b * nt + qi[s], 0)),
            scratch_shapes=[pltpu.VMEM((N_HEADS, ta, LANES), F32), pltpu.VMEM((N_HEADS, ta, LANES), F32)]),
        out_shape=jax.ShapeDtypeStruct((nb * t, GROUP_WIDTH), F32),
        compiler_params=_params(("arbitrary", "arbitrary")),
        name="flash_attn",
    )(qi, kj, qa, ka, va)


def _chunk_masks():
    r = lax.broadcasted_iota(jnp.int32, (CHUNK, CHUNK), 0)
    c = lax.broadcasted_iota(jnp.int32, (CHUNK, CHUNK), 1)
    return r >= c, r > c, r == c


def _valid_rows(t_valid, t_padded):
    if t_valid == t_padded:
        return None
    rows = pl.program_id(1) * CHUNK + lax.broadcasted_iota(jnp.int32, (CHUNK, 1), 0)
    return rows < t_valid


def _rms_heads_out(x, w):
    return x * lax.rsqrt(jnp.mean(x * x, axis=-1, keepdims=True) + NORM_EPS) * w


def _gdn_kernel(x_ref, gate_ref, zg_ref, hist_ref, w_ref, arow_ref, dtrow_ref, ng_ref, s0_ref,
                o_ref, sout_ref, s_ref, xprev_ref, xp_ref, *, t_valid, t_padded):
    n = pl.program_id(1)
    hd, gw = HEAD_DIM, GROUP_WIDTH

    @pl.when(n == 0)
    def _():
        s_ref[...] = s0_ref[...]
        xprev_ref[...] = hist_ref[...]

    x = x_ref[...]
    xp_ref[0:SUBLANES, :] = xprev_ref[...]
    xp_ref[SUBLANES:, :] = x
    xprev_ref[...] = x[CHUNK - SUBLANES:, :]
    first = SUBLANES - (GDN_CONV - 1)
    y = w_ref[0:1, :] * xp_ref[first:first + CHUNK, :]
    for jj in range(1, GDN_CONV):
        y = y + w_ref[jj:jj + 1, :] * xp_ref[first + jj:first + jj + CHUNK, :]
    qkv = y * _sigmoid(y)

    gates = zg_ref[...]
    g_all = -jnp.exp(arow_ref[...]) * _softplus(gates + dtrow_ref[...])
    beta_all = _sigmoid(gates)
    valid = _valid_rows(t_valid, t_padded)
    if valid is not None:
        g_all = jnp.where(valid, g_all, 0.0)
    incl, strict, eye = _chunk_masks()
    gc_all = _dot3_rhs(incl.astype(BF16), g_all)
    gate_act = gate_ref[...]
    gate_act = gate_act * _sigmoid(gate_act)
    outs = []
    for h in range(N_HEADS):
        q = qkv[:, h * hd:(h + 1) * hd]
        k = qkv[:, gw + h * hd:gw + (h + 1) * hd]
        v = qkv[:, 2 * gw + h * hd:2 * gw + (h + 1) * hd]
        q = q * lax.rsqrt(jnp.sum(q * q, axis=-1, keepdims=True) + NORM_EPS) * QK_SCALE
        k = k * lax.rsqrt(jnp.sum(k * k, axis=-1, keepdims=True) + NORM_EPS)
        beta = beta_all[:, GL_GDN_B + h:GL_GDN_B + h + 1]
        if valid is not None:
            k = jnp.where(valid, k, 0.0)
            v = jnp.where(valid, v, 0.0)
            beta = jnp.where(valid, beta, 0.0)
        gc = gc_all[:, GL_GDN_A + h:GL_GDN_A + h + 1]
        decay = jnp.exp(jnp.where(incl, gc - _row_of_col(gc, CHUNK), NEG))
        kb = k * beta
        lm = jnp.where(strict, _dot_nt(kb, k) * decay, 0.0)
        inv = jnp.where(eye, 1.0, 0.0) - lm
        pw = lm
        for _ in range((CHUNK - 1).bit_length() - 1):
            pw = _dot(pw, pw)
            inv = inv + _dot(inv, pw)
        eg = jnp.exp(gc)
        uw = _dot(inv, jnp.concatenate([v * beta, kb * eg], axis=-1))
        u, w = uw[:, :hd], uw[:, hd:]
        attn = _dot_nt(q, k) * decay
        g_last = gc[CHUNK - 1:CHUNK, :]
        k_tail = k * jnp.exp(g_last - gc)
        s = s_ref[h]
        v_new = u - _dot(w, s)
        o = _dot(q * eg, s) + _dot(attn, v_new)
        s_ref[h] = s * jnp.exp(g_last) + _dot_tn(k_tail, v_new)
        outs.append(_rms_heads_out(o, ng_ref[...]) * gate_act[:, h * hd:(h + 1) * hd])
    o_ref[...] = jnp.concatenate(outs, axis=-1)

    @pl.when(n == pl.num_programs(1) - 1)
    def _():
        sout_ref[...] = s_ref[...]


def _gdn(z, nb, t_padded, t_valid, hist, conv_w8, arow, dtrow, ng, s0):
    nc = t_padded // CHUNK
    assert t_padded % CHUNK == 0
    qkv_w = 3 * GROUP_WIDTH
    assert (ZB_GDN_QKV * GROUP_WIDTH) % qkv_w == 0
    state = pl.BlockSpec((None, N_HEADS, HEAD_DIM, HEAD_DIM), lambda b, n: (b, 0, 0, 0))
    return pl.pallas_call(
        functools.partial(_gdn_kernel, t_valid=t_valid, t_padded=t_padded),
        grid=(nb, nc),
        in_specs=[pl.BlockSpec((CHUNK, qkv_w), lambda b, n: (b * nc + n, ZB_GDN_QKV * GROUP_WIDTH // qkv_w)),
                  pl.BlockSpec((CHUNK, GROUP_WIDTH), lambda b, n: (b * nc + n, ZB_GDN_G)),
                  pl.BlockSpec((CHUNK, LANES), lambda b, n: (b * nc + n, ZB_GATES)),
                  pl.BlockSpec((None, SUBLANES, qkv_w), lambda b, n: (b, 0, 0)),
                  _const_spec((SUBLANES, qkv_w)), _const_spec((1, LANES)), _const_spec((1, LANES)),
                  _const_spec((1, HEAD_DIM)), state],
        out_specs=[pl.BlockSpec((CHUNK, GROUP_WIDTH), lambda b, n: (b * nc + n, 0)), state],
        out_shape=[jax.ShapeDtypeStruct((nb * t_padded, GROUP_WIDTH), F32),
                   jax.ShapeDtypeStruct((nb, N_HEADS, HEAD_DIM, HEAD_DIM), F32)],
        scratch_shapes=[pltpu.VMEM((N_HEADS, HEAD_DIM, HEAD_DIM), F32),
                        pltpu.VMEM((SUBLANES, qkv_w), F32),
                        pltpu.VMEM((CHUNK + SUBLANES, qkv_w), F32)],
        compiler_params=_params(("arbitrary", "arbitrary")),
        name="gdn_chunks",
    )(z, z, z, hist, conv_w8, arow, dtrow, ng, s0)


def _mlstm_kernel(q_ref, k_ref, v_ref, og_ref, zg_ref, ibrow_ref, fbrow_ref, ng_ref, c0_ref, n0_ref, m0_ref,
                  o_ref, cout_ref, nout_ref, mout_ref, c_ref, n_ref, m_ref, *, t_valid, t_padded):
    step = pl.program_id(1)
    hd = HEAD_DIM

    @pl.when(step == 0)
    def _():
        c_ref[...] = c0_ref[...]
        n_ref[...] = n0_ref[...]
        m_ref[...] = m0_ref[...]

    gates = zg_ref[...]
    ig_all = gates + ibrow_ref[...]
    lf_all = _log_sigmoid(gates + fbrow_ref[...])
    valid = _valid_rows(t_valid, t_padded)
    if valid is not None:
        ig_all = jnp.where(valid, ig_all, NEG)
        lf_all = jnp.where(valid, lf_all, 0.0)
    incl, _, _ = _chunk_masks()
    b_all = _dot3_rhs(incl.astype(BF16), lf_all)
    og = _sigmoid(og_ref[...])
    outs = []
    for h in range(N_HEADS):
        sl = slice(h * hd, (h + 1) * hd)
        q = q_ref[:, sl]
        k = k_ref[:, sl] * QK_SCALE
        v = v_ref[:, sl]
        b = b_all[:, GL_ML_F + h:GL_ML_F + h + 1]
        ig = ig_all[:, GL_ML_I + h:GL_ML_I + h + 1]
        dm = jnp.where(incl, b - _row_of_col(b, CHUNK) + _row_of_col(ig, CHUNK), NEG)
        m_intra = jnp.max(dm, axis=-1, keepdims=True)
        qk = _dot_nt(q, k)
        b_last = b[CHUNK - 1:CHUNK, :]
        a_end = b_last - b + ig
        m = m_ref[h][:, :1]
        cs = c_ref[h]
        ns = n_ref[h]
        m_t = jnp.maximum(b + m, m_intra)
        inter = jnp.exp(b + m - m_t)
        wmat = jnp.exp(dm - m_t) * qk
        num = inter * _dot(q, cs) + _dot(wmat, v)
        den = inter * jnp.sum(q * ns, axis=-1, keepdims=True) + jnp.sum(wmat, axis=-1, keepdims=True)
        hh = num / jnp.maximum(jnp.abs(den), jnp.exp(-m_t))
        m_new = jnp.maximum(b_last + m, jnp.max(a_end, axis=0, keepdims=True))
        wk = jnp.exp(a_end - m_new) * k
        sc = jnp.exp(b_last + m - m_new)
        c_ref[h] = sc * cs + _dot_tn(wk, v)
        n_ref[h] = sc * ns + jnp.sum(wk, axis=0, keepdims=True)
        m_ref[h] = jnp.broadcast_to(m_new, (1, LANES))
        outs.append(_rms_heads_out(hh, ng_ref[...]) * og[:, sl])
    o_ref[...] = jnp.concatenate(outs, axis=-1)

    @pl.when(step == pl.num_programs(1) - 1)
    def _():
        cout_ref[...] = c_ref[...]
        nout_ref[...] = n_ref[...]
        mout_ref[...] = m_ref[...]


def _mlstm(z, nb, t_padded, t_valid, ibrow, fbrow, ng, c0, n0, m0):
    nc = t_padded // CHUNK
    assert t_padded % CHUNK == 0
    zb = lambda blk: pl.BlockSpec((CHUNK, GROUP_WIDTH), lambda b, n: (b * nc + n, blk))
    st = lambda r, w: pl.BlockSpec((None, N_HEADS, r, w), lambda b, n: (b, 0, 0, 0))
    shp = lambda r, w: jax.ShapeDtypeStruct((nb, N_HEADS, r, w), F32)
    return pl.pallas_call(
        functools.partial(_mlstm_kernel, t_valid=t_valid, t_padded=t_padded),
        grid=(nb, nc),
        in_specs=[zb(ZB_ML_Q), zb(ZB_ML_K), zb(ZB_ML_V), zb(ZB_ML_O),
                  pl.BlockSpec((CHUNK, LANES), lambda b, n: (b * nc + n, ZB_GATES)),
                  _const_spec((1, LANES)), _const_spec((1, LANES)), _const_spec((1, HEAD_DIM)),
                  st(HEAD_DIM, HEAD_DIM), st(1, HEAD_DIM), st(1, LANES)],
        out_specs=[pl.BlockSpec((CHUNK, GROUP_WIDTH), lambda b, n: (b * nc + n, 0)),
                   st(HEAD_DIM, HEAD_DIM), st(1, HEAD_DIM), st(1, LANES)],
        out_shape=[jax.ShapeDtypeStruct((nb * t_padded, GROUP_WIDTH), F32),
                   shp(HEAD_DIM, HEAD_DIM), shp(1, HEAD_DIM), shp(1, LANES)],
        scratch_shapes=[pltpu.VMEM((N_HEADS, HEAD_DIM, HEAD_DIM), F32),
                        pltpu.VMEM((N_HEADS, 1, HEAD_DIM), F32),
                        pltpu.VMEM((N_HEADS, 1, LANES), F32)],
        compiler_params=_params(("arbitrary", "arbitrary")),
        name="mlstm_chunks",
    )(z, z, z, z, z, ibrow, fbrow, ng, c0, n0, m0)


N_QROWS = 16
PAGES_PER_STEP = 8


def _suffix_scan(x):
    lane = lax.broadcasted_iota(jnp.int32, x.shape, 1)
    y = x
    s = 1
    while s < LANES:
        y = y + jnp.where(lane + s < LANES, pltpu.roll(y, LANES - s, 1), 0.0)
        s *= 2
    return y


def _dup_rows(x8):
    return jnp.concatenate([x8, x8], axis=0)


def _col_of_row(row, n):
    r = lax.broadcasted_iota(jnp.int32, (n, row.shape[1]), 0)
    c = lax.broadcasted_iota(jnp.int32, (n, row.shape[1]), 1)
    return jnp.sum(jnp.where(r == c, row, 0.0), axis=1, keepdims=True)


def _row_of_col_padded(col):
    r = lax.broadcasted_iota(jnp.int32, (N_QROWS, LANES), 0)
    c = lax.broadcasted_iota(jnp.int32, (N_QROWS, LANES), 1)
    return jnp.sum(jnp.where(r == c, col, 0.0), axis=0, keepdims=True)


def _decode_kernel(*refs, fox, n_pages_step, n_blocks):
    pt_ref = refs[0]
    del pt_ref
    q_ref, knew_ref, vnew_ref = refs[1:4]
    pos = 4
    lfnew_ref = qt_ref = None
    if fox:
        lfnew_ref = refs[pos]
    else:
        qt_ref = refs[pos]
    pos += 1
    k_refs = refs[pos:pos + n_pages_step]
    pos += n_pages_step
    v_refs = refs[pos:pos + n_pages_step]
    pos += n_pages_step
    lf_refs = ()
    if fox:
        lf_refs = refs[pos:pos + n_pages_step]
        pos += n_pages_step
    o_ref = refs[pos]
    scratch = refs[pos + 1:]
    g = pl.program_id(1)
    last = pl.num_programs(1) - 1

    row = lax.broadcasted_iota(jnp.int32, (N_QROWS, GROUP_WIDTH), 0)
    lane_w = lax.broadcasted_iota(jnp.int32, (N_QROWS, GROUP_WIDTH), 1)
    qbd = jnp.where(lane_w // HEAD_DIM == row % N_HEADS, q_ref[...], 0.0)
    qbd16 = qbd.astype(BF16)
    tok = lax.broadcasted_iota(jnp.int32, (N_QROWS, LANES), 0) // N_HEADS
    key = lax.broadcasted_iota(jnp.int32, (N_QROWS, LANES), 1)
    new_ok = key <= tok

    def scores(kt_page):
        return _dot(qbd16, kt_page.astype(BF16))

    def weighted_values(p, vt_page):
        return _dot_nt(p.astype(BF16), vt_page.astype(BF16))

    if fox:
        m_ref, l_ref, acc_ref, run_ref, sq_ref = scratch

        def online(s_list, v_list):
            s_all = jnp.concatenate(s_list, axis=-1) if len(s_list) > 1 else s_list[0]
            m_prev = m_ref[:, :1]
            m_new = jnp.maximum(m_prev, jnp.max(s_all, axis=-1, keepdims=True))
            alpha = jnp.exp(m_prev - m_new)
            p = jnp.exp(s_all - m_new)
            acc = acc_ref[...] * alpha
            for i, v_page in enumerate(v_list):
                acc = acc + weighted_values(p[:, i * LANES:(i + 1) * LANES], v_page)
            acc_ref[...] = acc
            l_ref[...] = jnp.broadcast_to(l_ref[:, :1] * alpha + jnp.sum(p, axis=-1, keepdims=True), (N_QROWS, LANES))
            m_ref[...] = jnp.broadcast_to(m_new, (N_QROWS, LANES))

        @pl.when(g == 0)
        def _():
            m_ref[...] = jnp.full_like(m_ref, NEG)
            l_ref[...] = jnp.zeros_like(l_ref)
            acc_ref[...] = jnp.zeros_like(acc_ref)
            lf8 = lfnew_ref[...]
            incl = _suffix_scan(lf8)
            excl16 = _dup_rows(incl - lf8)
            sq = jnp.sum(jnp.where(key == tok, excl16, 0.0), axis=-1, keepdims=True)
            sq_ref[...] = jnp.broadcast_to(sq, (N_QROWS, LANES))
            run_ref[...] = jnp.broadcast_to(_dup_rows(incl[:, :1]), (N_QROWS, LANES))
            s = jnp.where(new_ok, scores(knew_ref[...]) + excl16 - sq, NEG)
            online([s], [vnew_ref[...]])

        run = run_ref[:, :1]
        sq = sq_ref[:, :1]
        s_list, v_list = [], []
        for i in range(n_pages_step):
            lf8 = lf_refs[i][...]
            incl = _suffix_scan(lf8)
            s_list.append(scores(k_refs[i][...]) + _dup_rows(incl - lf8) + (run - sq))
            run = run + _dup_rows(incl[:, :1])
            v_list.append(v_refs[i][...])
        run_ref[...] = jnp.broadcast_to(run, (N_QROWS, LANES))
        online(s_list, v_list)

        @pl.when(g == last)
        def _():
            o_ref[...] = acc_ref[...] / l_ref[:, :1]
    else:
        opart_ref, mt_ref, lt_ref, gt_ref, mown_ref, lown_ref, oown_ref = scratch
        blk_row = lax.broadcasted_iota(jnp.int32, (n_blocks, LANES), 0)

        @pl.when(g == 0)
        def _():
            mt_ref[...] = jnp.zeros_like(mt_ref)
            lt_ref[...] = jnp.zeros_like(lt_ref)
            gt_ref[...] = jnp.zeros_like(gt_ref)
            s = jnp.where(new_ok, scores(knew_ref[...]), NEG)
            m = jnp.max(s, axis=-1, keepdims=True)
            p = jnp.exp(s - m)
            mown_ref[...] = jnp.broadcast_to(m, (N_QROWS, LANES))
            lown_ref[...] = jnp.broadcast_to(jnp.sum(p, axis=-1, keepdims=True), (N_QROWS, LANES))
            oown_ref[...] = weighted_values(p, vnew_ref[...])

        crow = lax.broadcasted_iota(jnp.int32, (GROUP_WIDTH, LANES), 0)
        qlane = lax.broadcasted_iota(jnp.int32, (GROUP_WIDTH, LANES), 1)
        qbd_t = jnp.where(crow // HEAD_DIM == qlane % N_HEADS, qt_ref[...], 0.0)
        pages_per_block = MOBA_BLOCK // LANES
        for bi in range(n_pages_step // pages_per_block):
            n = g * (n_pages_step // pages_per_block) + bi
            kp = [k_refs[bi * pages_per_block + e][...] for e in range(pages_per_block)]
            vp = [v_refs[bi * pages_per_block + e][...] for e in range(pages_per_block)]
            s = jnp.concatenate([scores(k) for k in kp], axis=-1)
            m = jnp.max(s, axis=-1, keepdims=True)
            p = jnp.exp(s - m)
            l = jnp.sum(p, axis=-1, keepdims=True)
            o = weighted_values(p[:, :LANES], vp[0])
            for e in range(1, pages_per_block):
                o = o + weighted_values(p[:, e * LANES:(e + 1) * LANES], vp[e])
            kmean = jnp.sum(sum(kp), axis=-1, keepdims=True) * (1.0 / MOBA_BLOCK)
            gate = jnp.sum(qbd_t * kmean, axis=0, keepdims=True)
            opart_ref[n] = o
            here = blk_row == n
            mt_ref[...] = jnp.where(here, _row_of_col_padded(m), mt_ref[...])
            lt_ref[...] = jnp.where(here, _row_of_col_padded(l), lt_ref[...])
            gt_ref[...] = jnp.where(here, gate, gt_ref[...])

        @pl.when(g == last)
        def _():
            gt = gt_ref[...]
            sel = jnp.zeros(gt.shape, F32)
            for _ in range(MOBA_TOPK):
                mx = jnp.max(gt, axis=0, keepdims=True)
                first = jnp.min(jnp.where(gt == mx, blk_row, n_blocks), axis=0, keepdims=True)
                pick = blk_row == first
                sel = jnp.where(pick, 1.0, sel)
                gt = jnp.where(pick, -jnp.inf, gt)
            mt = jnp.where(sel > 0.0, mt_ref[...], NEG)
            m_own = _row_of_col_padded(mown_ref[:, :1])
            l_own = _row_of_col_padded(lown_ref[:, :1])
            m_all = jnp.maximum(jnp.max(mt, axis=0, keepdims=True), m_own)
            w = jnp.where(sel > 0.0, jnp.exp(mt - m_all), 0.0)
            w_own = jnp.exp(m_own - m_all)
            denom = jnp.sum(w * lt_ref[...], axis=0, keepdims=True) + w_own * l_own
            w = w / denom
            out = oown_ref[...] * _col_of_row(w_own / denom, N_QROWS)
            for n in range(n_blocks):
                out = out + opart_ref[n] * _col_of_row(w[n:n + 1, :], N_QROWS)
            o_ref[...] = out


def _decode(page_table, layer, q16, knew_t, vnew_t, extra, cache_kt, cache_vt, cache_lf8, fox):
    nb, n_pages = page_table.shape
    pps = PAGES_PER_STEP
    assert n_pages % pps == 0 and cache_kt.shape[2:] == (GROUP_WIDTH, LANES)
    n_blocks = n_pages * LANES // MOBA_BLOCK
    assert (n_pages * LANES) % MOBA_BLOCK == 0 and n_blocks >= MOBA_TOPK and n_blocks % SUBLANES == 0
    steps = n_pages // pps

    def page_spec(i, rows):
        if fox:
            idx = lambda b, g, pt: (layer, pt[b * n_pages + n_pages - 1 - (g * pps + i)], 0, 0)
        else:
            idx = lambda b, g, pt: (layer, pt[b * n_pages + g * pps + i], 0, 0)
        return pl.BlockSpec((None, None, rows, LANES), idx)

    seq = lambda r, w: pl.BlockSpec((None, r, w), lambda b, g, pt: (b, 0, 0))
    in_specs = [seq(N_QROWS, GROUP_WIDTH), seq(GROUP_WIDTH, LANES), seq(GROUP_WIDTH, LANES),
                seq(SUBLANES if fox else GROUP_WIDTH, LANES)]
    args = [q16, knew_t, vnew_t, extra]
    in_specs += [page_spec(i, GROUP_WIDTH) for i in range(pps)] * 2
    args += [cache_kt] * pps + [cache_vt] * pps
    if fox:
        in_specs += [page_spec(i, SUBLANES) for i in range(pps)]
        args += [cache_lf8] * pps
        scratch = [pltpu.VMEM((N_QROWS, LANES), F32), pltpu.VMEM((N_QROWS, LANES), F32),
                   pltpu.VMEM((N_QROWS, GROUP_WIDTH), F32), pltpu.VMEM((N_QROWS, LANES), F32),
                   pltpu.VMEM((N_QROWS, LANES), F32)]
    else:
        scratch = [pltpu.VMEM((n_blocks, N_QROWS, GROUP_WIDTH), F32)] + [pltpu.VMEM((n_blocks, LANES), F32)] * 3 + [
            pltpu.VMEM((N_QROWS, LANES), F32), pltpu.VMEM((N_QROWS, LANES), F32),
            pltpu.VMEM((N_QROWS, GROUP_WIDTH), F32)]
    return pl.pallas_call(
        functools.partial(_decode_kernel, fox=fox, n_pages_step=pps, n_blocks=n_blocks),
        grid_spec=pltpu.PrefetchScalarGridSpec(
            num_scalar_prefetch=1, grid=(nb, steps), in_specs=in_specs,
            out_specs=seq(N_QROWS, GROUP_WIDTH), scratch_shapes=scratch),
        out_shape=jax.ShapeDtypeStruct((nb, N_QROWS, GROUP_WIDTH), F32),
        compiler_params=_params(("arbitrary", "arbitrary")),
        name="fox_decode" if fox else "moba_decode",
    )(page_table.reshape(-1), *args)


def _pack_w_in(w):
    gw, nh = GROUP_WIDTH, N_HEADS
    widths = [('fox_q', gw), ('fox_k', gw), ('fox_v', gw), ('fox_f', nh), ('gdn_qkv', 3 * gw), ('gdn_a', nh),
              ('gdn_b', nh), ('gdn_g', gw), ('moba_q', gw), ('moba_k', gw), ('moba_v', gw), ('mlstm_q', gw),
              ('mlstm_k', gw), ('mlstm_v', gw), ('mlstm_i', nh), ('mlstm_f', nh), ('mlstm_o', gw)]
    cols, off = {}, 0
    for name, width in widths:
        cols[name] = w[:, off:off + width]
        off += width
    assert off == w.shape[1]
    wide = [cols[k] for k in ('fox_q', 'fox_k', 'fox_v', 'gdn_qkv', 'gdn_g', 'moba_q', 'moba_k', 'moba_v',
                              'mlstm_q', 'mlstm_k', 'mlstm_v', 'mlstm_o')]
    gates = [cols[k] for k in ('fox_f', 'gdn_a', 'gdn_b', 'mlstm_i', 'mlstm_f')]
    pad = jnp.zeros((w.shape[0], LANES - 5 * nh), w.dtype)
    return jnp.concatenate(wide + gates + [pad], axis=1).astype(BF16)


def _gate_row(vec, lane0):
    return jnp.zeros((1, LANES), F32).at[0, lane0:lane0 + N_HEADS].set(vec.astype(F32))


def _tile_heads(vec):
    return jnp.tile(vec.astype(F32), N_HEADS).reshape(1, GROUP_WIDTH)


def _rope_tables(pos):
    half = HEAD_DIM // 2
    inv_freq = jnp.power(ROPE_THETA, -jnp.arange(half, dtype=F32) / half)
    ang = pos.astype(F32)[:, None] * inv_freq[None, :]
    cos = jnp.cos(ang)
    sin = jnp.sin(ang)
    cos_h = jnp.concatenate([cos, cos], axis=-1)
    sin_h = jnp.concatenate([-sin, sin], axis=-1)
    return jnp.tile(cos_h, (1, N_HEADS)), jnp.tile(sin_h, (1, N_HEADS))


def _head_sum_matrix():
    lane = jnp.arange(GROUP_WIDTH)
    return (lane[:, None] // HEAD_DIM == lane[None, :] // HEAD_DIM).astype(BF16)


def _lower_tri(n):
    r = jnp.arange(n)
    return (r[:, None] >= r[None, :]).astype(BF16)


def _pad_rows(a, rows, front=0):
    return jnp.pad(a, ((0, 0), (front, rows - front - a.shape[1]), (0, 0)))


ROW_TILE = 256
ATTN_TILE = 256


def _layer_weights(w_in, w_out, w_gate, w_up, w_down):
    return (_pack_w_in(w_in), w_out.astype(BF16), w_gate.astype(BF16), w_up.astype(BF16), w_down.astype(BF16))


def _zcols(z, blk, width=GROUP_WIDTH):
    return z[:, blk * GROUP_WIDTH:blk * GROUP_WIDTH + width]


def _prompt_layer(x, lw, lp, consts):
    nb, t, d = x.shape
    wp, wo, wg, wu, wd = lw
    x2 = x.reshape(nb * t, d)
    z = _proj(x2, lp['ln1'], wp, ROW_TILE)
    fqa, fka, fva, fk32, flf, mqa, mka, mva, mk32 = _prep_prompt(
        z, nb, t, lp['attn_norms'], lp['fox_fb_row'], consts['cos_p'], consts['sin_p'], consts['gsum'], consts['tri'])
    o_a = _flash(fqa, fka, fva, nb, t, ATTN_TILE)
    o_c = _flash(mqa, mka, mva, nb, t, ATTN_TILE)
    zeros_state = jnp.zeros((nb, N_HEADS, HEAD_DIM, HEAD_DIM), F32)
    o_b, gdn_s = _gdn(z, nb, t, t, jnp.zeros((nb, SUBLANES, 3 * GROUP_WIDTH), F32), lp['conv_w8'],
                      lp['gdn_a_row'], lp['gdn_dt_row'], lp['gdn_ng'], zeros_state)
    o_d, ml_c, ml_n, ml_m = _mlstm(z, nb, t, t, lp['ml_ib_row'], lp['ml_fb_row'], lp['ml_ng'], zeros_state,
                                   jnp.zeros((nb, N_HEADS, 1, HEAD_DIM), F32), jnp.zeros((nb, N_HEADS, 1, LANES), F32))
    y = _out_ffn(x2, (o_a, o_b, o_c, o_d), wo, lp['ln2'], wg, wu, wd, ROW_TILE).reshape(nb, t, d)
    heads = lambda a: a.reshape(nb, t, N_HEADS, HEAD_DIM)
    z3 = z.reshape(nb, t, Z_WIDTH)
    qkv0 = ZB_GDN_QKV * GROUP_WIDTH
    states = (heads(fk32), heads(_zcols(z, ZB_FOX_V)), flf[:, :N_HEADS].reshape(nb, t, N_HEADS),
              heads(mk32), heads(_zcols(z, ZB_MOBA_V)),
              z3[:, t - (GDN_CONV - 1):, qkv0:qkv0 + 3 * GROUP_WIDTH], gdn_s,
              ml_c, ml_n[:, :, 0, :], ml_m[:, :, 0, 0])
    return y, states


def _sample_layer(x, layer, lw, lp, consts, caches, page_table, st):
    nb, t, d = x.shape
    assert t <= N_HEADS and t >= GDN_CONV - 1 and nb * t % SUBLANES == 0
    wp, wo, wg, wu, wd = lw
    x2 = x.reshape(nb * t, d)
    z = _proj(x2, lp['ln1'], wp, nb * t)
    fq32, fk32, flf, mq32, mk32 = _prep_sample(z, lp['attn_norms'], lp['fox_fb_row'], consts['cos_s'], consts['sin_s'],
                                               consts['gsum'])
    seq = lambda a: a.reshape(nb, t, a.shape[-1])
    tok_rows = lambda a: _pad_rows(jnp.repeat(seq(a), N_HEADS, axis=1), N_QROWS)
    new_page = lambda a: jnp.pad(jnp.swapaxes(seq(a), 1, 2), ((0, 0), (0, 0), (0, LANES - t)))
    rows_t = lambda a: jnp.pad(jnp.swapaxes(tok_rows(a), 1, 2), ((0, 0), (0, 0), (0, LANES - N_QROWS)))
    lf_new = jnp.swapaxes(seq(flf[:, :N_HEADS]), 1, 2)
    lf_new = jnp.pad(lf_new, ((0, 0), (0, 0), (0, LANES - t)))
    lf_new8 = jnp.concatenate([lf_new, lf_new], axis=1)
    fox_k, fox_v, fox_lf8, moba_k, moba_v = caches

    def own_head(o16):
        o5 = o16.reshape(nb, N_HEADS, N_HEADS, N_HEADS, HEAD_DIM)[:, :t]
        hh = jnp.arange(N_HEADS)
        return o5[:, :, hh, hh, :].reshape(nb * t, GROUP_WIDTH)

    o_a = own_head(_decode(page_table, layer, tok_rows(fq32), new_page(fk32), new_page(_zcols(z, ZB_FOX_V)),
                           lf_new8, fox_k, fox_v, fox_lf8, True))
    o_c = own_head(_decode(page_table, layer, tok_rows(mq32), new_page(mk32), new_page(_zcols(z, ZB_MOBA_V)),
                           rows_t(mq32), moba_k, moba_v, None, False))
    zp = _pad_rows(seq(z), CHUNK).reshape(nb * CHUNK, Z_WIDTH)
    conv_buf, gdn_s0, ml_c0, ml_n0, ml_m0 = st
    real_rows = lambda a: a.reshape(nb, CHUNK, GROUP_WIDTH)[:, :t].reshape(nb * t, GROUP_WIDTH)
    o_b, gdn_s = _gdn(zp, nb, CHUNK, t, _pad_rows(conv_buf, SUBLANES, front=SUBLANES - (GDN_CONV - 1)), lp['conv_w8'],
                      lp['gdn_a_row'], lp['gdn_dt_row'], lp['gdn_ng'], gdn_s0)
    o_d, ml_c, ml_n, ml_m = _mlstm(zp, nb, CHUNK, t, lp['ml_ib_row'], lp['ml_fb_row'], lp['ml_ng'], ml_c0,
                                   ml_n0[:, :, None, :],
                                   jnp.broadcast_to(ml_m0[:, :, None, None], (nb, N_HEADS, 1, LANES)))
    y = _out_ffn(x2, (o_a, real_rows(o_b), o_c, real_rows(o_d)), wo, lp['ln2'], wg, wu, wd, nb * t).reshape(nb, t, d)
    heads = lambda a: a.reshape(nb, t, N_HEADS, HEAD_DIM)
    qkv0 = ZB_GDN_QKV * GROUP_WIDTH
    conv_rows = jnp.concatenate([conv_buf, seq(z)[:, :, qkv0:qkv0 + 3 * GROUP_WIDTH]], axis=1)
    states = (heads(fk32), heads(_zcols(z, ZB_FOX_V)), seq(flf[:, :N_HEADS]),
              heads(mk32), heads(_zcols(z, ZB_MOBA_V)),
              conv_rows[:, conv_rows.shape[1] - (GDN_CONV - 1):], gdn_s,
              ml_c, ml_n[:, :, 0, :], ml_m[:, :, 0, 0])
    return y, states


def kernel(x_prompt, x_sample, cache_fox_k, cache_fox_v, cache_fox_logf, cache_moba_k, cache_moba_v,
           state_gdn_conv, state_gdn_s, state_mlstm_c, state_mlstm_n, state_mlstm_m, page_table,
           ln1, w_in, fox_q_norm, fox_k_norm, fox_f_bias, gdn_conv_w, gdn_a_log, gdn_dt_bias, gdn_norm,
           moba_q_norm, moba_k_norm, mlstm_i_bias, mlstm_f_bias, mlstm_norm, w_out, ln2,
           w_gate, w_up, w_down):
    depth = w_in.shape[0]
    t_prompt = x_prompt.shape[1]
    nb_s, t_sample = x_sample.shape[:2]
    n_pool, page = cache_fox_k.shape[1:3]
    past_len = page_table.shape[1] * page
    assert page == LANES and past_len % MOBA_BLOCK == 0
    cos_p, sin_p = _rope_tables(jnp.arange(t_prompt, dtype=jnp.int32))
    cos_s, sin_s = _rope_tables(past_len + jnp.arange(t_sample, dtype=jnp.int32))
    consts = {'cos_p': cos_p, 'sin_p': sin_p, 'cos_s': jnp.tile(cos_s, (nb_s, 1)), 'sin_s': jnp.tile(sin_s, (nb_s, 1)),
              'gsum': _head_sum_matrix(), 'tri': _lower_tri(MOBA_BLOCK)}
    pages = lambda c: jnp.transpose(c, (0, 1, 3, 4, 2)).reshape(depth, n_pool, GROUP_WIDTH, page)
    lf_t = jnp.swapaxes(cache_fox_logf, 2, 3)
    caches = (pages(cache_fox_k), pages(cache_fox_v), jnp.concatenate([lf_t, lf_t], axis=2),
              pages(cache_moba_k), pages(cache_moba_v))
    y_p, y_s = x_prompt, x_sample
    p_states, s_states = [], []
    for i in range(depth):
        lw = _layer_weights(w_in[i], w_out[i], w_gate[i], w_up[i], w_down[i])
        lp = {
            'ln1': ln1[i], 'ln2': ln2[i],
            'attn_norms': [_tile_heads(v[i]) for v in (fox_q_norm, fox_k_norm, moba_q_norm, moba_k_norm)],
            'fox_fb_row': _gate_row(fox_f_bias[i], GL_FOX_F),
            'conv_w8': jnp.pad(gdn_conv_w[i].astype(F32), ((0, SUBLANES - GDN_CONV), (0, 0))),
            'gdn_a_row': _gate_row(gdn_a_log[i], GL_GDN_A), 'gdn_dt_row': _gate_row(gdn_dt_bias[i], GL_GDN_A),
            'gdn_ng': gdn_norm[i].reshape(1, HEAD_DIM).astype(F32),
            'ml_ib_row': _gate_row(mlstm_i_bias[i], GL_ML_I), 'ml_fb_row': _gate_row(mlstm_f_bias[i], GL_ML_F),
            'ml_ng': mlstm_norm[i].reshape(1, HEAD_DIM).astype(F32),
        }
        y_p, st_p = _prompt_layer(y_p, lw, lp, consts)
        st = (state_gdn_conv[i], state_gdn_s[i], state_mlstm_c[i], state_mlstm_n[i], state_mlstm_m[i])
        y_s, st_s = _sample_layer(y_s, i, lw, lp, consts, caches, page_table, st)
        p_states.append(st_p)
        s_states.append(st_s)
    stack = lambda states: [jnp.stack(a) for a in zip(*states)]
    return (y_p, y_s, *stack(p_states), *stack(s_states))
```

```python
import functools
import math

import jax
import jax.numpy as jnp
from jax import lax
from jax.experimental import pallas as pl
from jax.experimental.pallas import tpu as pltpu

F32 = jnp.float32
BF16 = jnp.bfloat16

HEAD_DIM = 64
N_HEADS = 4
GROUP_WIDTH = N_HEADS * HEAD_DIM
LANES = 128
SUBLANES = 8
MOBA_BLOCK = 256
MOBA_TOPK = 3
MOBA_SEL_LANES = 32
GDN_CONV = 4
CHUNK = 64
NORM_EPS = 1e-6
ROPE_THETA = 10000.0
NEG = -1e30
QK_SCALE = HEAD_DIM ** -0.5
VMEM_LIMIT = 48 * 1024 * 1024

ZB_FOX_Q, ZB_FOX_K, ZB_FOX_V = 0, 1, 2
ZB_GDN_QKV = 3
ZB_GDN_G = 6
ZB_MOBA_Q, ZB_MOBA_K, ZB_MOBA_V = 7, 8, 9
ZB_ML_Q, ZB_ML_K, ZB_ML_V, ZB_ML_O = 10, 11, 12, 13
Z_WIDE = 14 * GROUP_WIDTH
Z_WIDTH = Z_WIDE + LANES
ZB_GATES = Z_WIDE // LANES
GL_FOX_F, GL_GDN_A, GL_GDN_B, GL_ML_I, GL_ML_F = 0, 4, 8, 12, 16


def _dot(a, b):
    return jnp.dot(a, b, preferred_element_type=F32)


def _dot_nt(a, b):
    return lax.dot_general(a, b, (((1,), (1,)), ((), ())), preferred_element_type=F32)


def _dot_tn(a, b):
    return lax.dot_general(a, b, (((0,), (0,)), ((), ())), preferred_element_type=F32)


def _split3(x):
    hi = x.astype(BF16)
    r1 = x - hi.astype(F32)
    mid = r1.astype(BF16)
    lo = (r1 - mid.astype(F32)).astype(BF16)
    return hi, mid, lo


def _dot3_rhs(a_bf16, x):
    hi, mid, lo = _split3(x)
    return _dot(a_bf16, hi) + _dot(a_bf16, mid) + _dot(a_bf16, lo)


def _dot3_lhs(x, b_bf16):
    hi, mid, lo = _split3(x)
    return _dot(hi, b_bf16) + _dot(mid, b_bf16) + _dot(lo, b_bf16)


def _sigmoid(x):
    return 1.0 / (1.0 + jnp.exp(-x))


def _log_sigmoid(x):
    return jnp.minimum(x, 0.0) - jnp.log1p(jnp.exp(-jnp.abs(x)))


def _softplus(x):
    return jnp.maximum(x, 0.0) + jnp.log1p(jnp.exp(-jnp.abs(x)))


def _row_of_col(col, n):
    r = lax.broadcasted_iota(jnp.int32, (n, n), 0)
    c = lax.broadcasted_iota(jnp.int32, (n, n), 1)
    return jnp.sum(jnp.where(r == c, col, 0.0), axis=0, keepdims=True)


def _params(sem):
    return pltpu.CompilerParams(dimension_semantics=sem, vmem_limit_bytes=VMEM_LIMIT)


def _const_spec(shape):
    nd = len(shape)
    return pl.BlockSpec(shape, lambda *_: (0,) * nd, pipeline_mode=pl.Buffered(1))


PROJ_COL_CHUNK = 512


def _proj_kernel(x_ref, g_ref, w_ref, z_ref):
    x = x_ref[...]
    ms = jnp.mean(x * x, axis=-1, keepdims=True)
    h = (x * lax.rsqrt(ms + NORM_EPS) * g_ref[...]).astype(BF16)
    for c0 in range(0, Z_WIDTH, PROJ_COL_CHUNK):
        c1 = min(c0 + PROJ_COL_CHUNK, Z_WIDTH)
        z_ref[:, c0:c1] = _dot(h, w_ref[:, c0:c1])


def _proj(x2d, g, w_packed, tm):
    n, d = x2d.shape
    return pl.pallas_call(
        _proj_kernel,
        grid=(n // tm,),
        in_specs=[pl.BlockSpec((tm, d), lambda i: (i, 0)),
                  _const_spec((1, d)),
                  _const_spec((d, Z_WIDTH))],
        out_specs=pl.BlockSpec((tm, Z_WIDTH), lambda i: (i, 0)),
        out_shape=jax.ShapeDtypeStruct((n, Z_WIDTH), F32),
        compiler_params=_params(("arbitrary",)),
        name="norm_proj",
    )(x2d, g.reshape(1, d), w_packed)


FFN_COL_CHUNK = 256


def _out_ffn_kernel(x_ref, oa_ref, ob_ref, oc_ref, od_ref, wo_ref, g2_ref, wg_ref, wu_ref, wd_ref, y_ref):
    x1 = x_ref[...]
    for gi, o_ref in enumerate((oa_ref, ob_ref, oc_ref, od_ref)):
        x1 = x1 + _dot(o_ref[...].astype(BF16), wo_ref[gi * GROUP_WIDTH:(gi + 1) * GROUP_WIDTH, :])
    ms = jnp.mean(x1 * x1, axis=-1, keepdims=True)
    h2 = (x1 * lax.rsqrt(ms + NORM_EPS) * g2_ref[...]).astype(BF16)
    y_ref[...] = x1
    d_ff = wg_ref.shape[1]
    for c0 in range(0, d_ff, FFN_COL_CHUNK):
        c1 = c0 + FFN_COL_CHUNK
        gate = _dot(h2, wg_ref[:, c0:c1])
        up = _dot(h2, wu_ref[:, c0:c1])
        act = (gate * _sigmoid(gate) * up).astype(BF16)
        y_ref[...] += _dot(act, wd_ref[c0:c1, :])


def _out_ffn(x2d, mix, wo, g2, wg, wu, wd, tm):
    n, d = x2d.shape
    d_ff = wg.shape[1]
    assert d_ff % FFN_COL_CHUNK == 0
    row = lambda w: pl.BlockSpec((tm, w), lambda i: (i, 0))
    return pl.pallas_call(
        _out_ffn_kernel,
        grid=(n // tm,),
        in_specs=[row(d)] + [row(GROUP_WIDTH)] * 4 + [
            _const_spec((d, d)), _const_spec((1, d)), _const_spec((d, d_ff)),
            _const_spec((d, d_ff)), _const_spec((d_ff, d))],
        out_specs=row(d),
        out_shape=jax.ShapeDtypeStruct((n, d), F32),
        compiler_params=_params(("arbitrary",)),
        name="out_ffn",
    )(x2d, *mix, wo, g2.reshape(1, d), wg, wu, wd)


def _lane_iota(rows):
    return lax.broadcasted_iota(jnp.int32, (rows, LANES), 1)


def _head_rms(x, w, gsum):
    y = x * x
    hi = y.astype(BF16)
    lo = (y - hi.astype(F32)).astype(BF16)
    ss = _dot(hi, gsum) + _dot(lo, gsum)
    return x * lax.rsqrt(ss * (1.0 / HEAD_DIM) + NORM_EPS) * w


def _rotary(x, cos, sin_signed):
    half = HEAD_DIM // 2
    out = []
    for p in range(GROUP_WIDTH // LANES):
        sl = slice(p * LANES, (p + 1) * LANES)
        v = x[:, sl]
        lane = _lane_iota(v.shape[0])
        partner = jnp.where(lane % HEAD_DIM < half,
                            pltpu.roll(v, LANES - half, 1), pltpu.roll(v, half, 1))
        out.append(v * cos[:, sl] + partner * sin_signed[:, sl])
    return jnp.concatenate(out, axis=-1)


def _head_group(x, h, aug):
    p, e = divmod(h, 2)
    piece = x[:, p * LANES:(p + 1) * LANES]
    if e:
        piece = pltpu.roll(piece, HEAD_DIM, 1)
    return jnp.where(_lane_iota(x.shape[0]) < HEAD_DIM, piece, aug)


def _prep_prompt_kernel(zfq, zfk, zfv, zmq, zmk, zmv, zg, fqn, fkn, mqn, mkn, fb, cos, sin, gsum, tri,
                        fqa, fka, fva, fk32, flf, mqa, mka, mva, mk32, carry_ref, kmean_ref):
    j = pl.program_id(1)
    tm = zfq.shape[0]
    lane = _lane_iota(tm)
    g = gsum[...]

    @pl.when(j == 0)
    def _():
        carry_ref[...] = jnp.zeros_like(carry_ref)
        kmean_ref[...] = jnp.zeros_like(kmean_ref)

    fq = _head_rms(zfq[...], fqn[...], g) * QK_SCALE
    fk = _head_rms(zfk[...], fkn[...], g)
    fk32[...] = fk
    lf = _log_sigmoid(zg[...] + fb[...])
    flf[...] = lf
    c = _dot3_rhs(tri[...], lf) + carry_ref[...]
    carry_ref[...] = c[tm - 1:tm, :]
    fv = zfv[...]
    for h in range(N_HEADS):
        col = c[:, GL_FOX_F + h:GL_FOX_F + h + 1]
        hi = col.astype(BF16).astype(F32)
        r1 = col - hi
        mid = r1.astype(BF16).astype(F32)
        lo = (r1 - mid).astype(BF16).astype(F32)
        base = HEAD_DIM
        one_q = jnp.where((lane >= base + 3) & (lane < base + 6), 1.0, 0.0)
        aug_q = jnp.where(lane == base, hi, jnp.where(lane == base + 1, mid, jnp.where(lane == base + 2, lo, one_q)))
        one_k = jnp.where((lane >= base) & (lane < base + 3), 1.0, 0.0)
        aug_k = jnp.where(lane == base + 3, -hi, jnp.where(lane == base + 4, -mid, jnp.where(lane == base + 5, -lo, one_k)))
        aug_v = jnp.where(lane == base, 1.0, 0.0)
        sl = slice(h * LANES, (h + 1) * LANES)
        fqa[:, sl] = _head_group(fq, h, aug_q).astype(BF16)
        fka[:, sl] = _head_group(fk, h, aug_k).astype(BF16)
        fva[:, sl] = _head_group(fv, h, aug_v).astype(BF16)

    mq = _rotary(_head_rms(zmq[...], mqn[...], g), cos[...], sin[...])
    mk = _rotary(_head_rms(zmk[...], mkn[...], g), cos[...], sin[...])
    mk32[...] = mk
    mv = zmv[...]
    km = kmean_ref[...]
    lane_w = lax.broadcasted_iota(jnp.int32, km.shape, 1)
    kbd = jnp.concatenate([jnp.where(lane_w // HEAD_DIM == h, km, 0.0) for h in range(N_HEADS)], axis=0)
    qh = mq.astype(BF16)
    ql = (mq - qh.astype(F32)).astype(BF16)
    kh = kbd.astype(BF16)
    kl = (kbd - kh.astype(F32)).astype(BF16)
    gate = _dot_nt(qh, kh) + _dot_nt(qh, kl) + _dot_nt(ql, kh)
    blk = lane % MOBA_SEL_LANES
    gate = jnp.where(blk < j, gate, NEG)
    sel = jnp.zeros((tm, LANES), F32)
    for h in range(N_HEADS):
        gh = jnp.where(lane // MOBA_SEL_LANES == h, gate, -jnp.inf)
        for _ in range(MOBA_TOPK):
            mx = jnp.max(gh, axis=-1, keepdims=True)
            first = jnp.min(jnp.where(gh == mx, lane, 2 * LANES), axis=-1, keepdims=True)
            pick = lane == first
            sel = jnp.where(pick, 1.0, sel)
            gh = jnp.where(pick, -jnp.inf, gh)
    pen = jnp.where((sel > 0.0) | (blk == j), 0.0, NEG)
    in_aug = (lane >= HEAD_DIM) & (lane < HEAD_DIM + MOBA_SEL_LANES)
    aug_k = jnp.where(in_aug & (lane - HEAD_DIM == j), 1.0, 0.0)
    aug_v = jnp.where(lane == HEAD_DIM, 1.0, 0.0)
    mqs = mq * QK_SCALE
    for h in range(N_HEADS):
        shift = (HEAD_DIM - MOBA_SEL_LANES * h) % LANES
        moved = pltpu.roll(pen, shift, 1) if shift else pen
        aug_q = jnp.where(in_aug, moved, 0.0)
        sl = slice(h * LANES, (h + 1) * LANES)
        mqa[:, sl] = _head_group(mqs, h, aug_q).astype(BF16)
        mka[:, sl] = _head_group(mk, h, aug_k).astype(BF16)
        mva[:, sl] = _head_group(mv, h, aug_v).astype(BF16)
    row = lax.broadcasted_iota(jnp.int32, km.shape, 0)
    kmean_ref[...] = jnp.where(row == j, jnp.mean(mk, axis=0, keepdims=True), km)


def _prep_prompt(z, nb, t, norms, fbias, cos, sin, gsum, tri):
    tm = MOBA_BLOCK
    nt = t // tm
    assert t % tm == 0 and nt <= MOBA_SEL_LANES
    n = nb * t
    zb = lambda blk: pl.BlockSpec((tm, GROUP_WIDTH), lambda b, j: (b * nt + j, blk))
    rowspec = lambda w: pl.BlockSpec((tm, w), lambda b, j: (b * nt + j, 0))
    wide = 4 * LANES
    outs = [(wide, BF16)] * 3 + [(GROUP_WIDTH, F32), (LANES, F32)] + [(wide, BF16)] * 3 + [(GROUP_WIDTH, F32)]
    return pl.pallas_call(
        _prep_prompt_kernel,
        grid=(nb, nt),
        in_specs=[zb(ZB_FOX_Q), zb(ZB_FOX_K), zb(ZB_FOX_V), zb(ZB_MOBA_Q), zb(ZB_MOBA_K), zb(ZB_MOBA_V),
                  pl.BlockSpec((tm, LANES), lambda b, j: (b * nt + j, ZB_GATES))]
                 + [_const_spec((1, GROUP_WIDTH))] * 4 + [_const_spec((1, LANES))]
                 + [pl.BlockSpec((tm, GROUP_WIDTH), lambda b, j: (j, 0))] * 2
                 + [_const_spec((GROUP_WIDTH, GROUP_WIDTH)), _const_spec((tm, tm))],
        out_specs=[rowspec(w) for w, _ in outs],
        out_shape=[jax.ShapeDtypeStruct((n, w), dt) for w, dt in outs],
        scratch_shapes=[pltpu.VMEM((1, LANES), F32), pltpu.VMEM((MOBA_SEL_LANES, GROUP_WIDTH), F32)],
        compiler_params=_params(("arbitrary", "arbitrary")),
        name="prep_prompt",
    )(z, z, z, z, z, z, z, *norms, fbias, cos, sin, gsum, tri)


def _prep_sample_kernel(zfq, zfk, zmq, zmk, zg, fqn, fkn, mqn, mkn, fb, cos, sin, gsum,
                        fq32, fk32, flf, mq32, mk32):
    g = gsum[...]
    fq32[...] = _head_rms(zfq[...], fqn[...], g) * QK_SCALE
    fk32[...] = _head_rms(zfk[...], fkn[...], g)
    flf[...] = _log_sigmoid(zg[...] + fb[...])
    mq32[...] = _rotary(_head_rms(zmq[...], mqn[...], g), cos[...], sin[...]) * QK_SCALE
    mk32[...] = _rotary(_head_rms(zmk[...], mkn[...], g), cos[...], sin[...])


def _prep_sample(z, norms, fbias, cos, sin, gsum):
    n = z.shape[0]
    zb = lambda blk: pl.BlockSpec((n, GROUP_WIDTH), lambda i: (0, blk))
    full = lambda w: pl.BlockSpec((n, w), lambda i: (0, 0))
    outs = [GROUP_WIDTH, GROUP_WIDTH, LANES, GROUP_WIDTH, GROUP_WIDTH]
    return pl.pallas_call(
        _prep_sample_kernel,
        grid=(1,),
        in_specs=[zb(ZB_FOX_Q), zb(ZB_FOX_K), zb(ZB_MOBA_Q), zb(ZB_MOBA_K),
                  pl.BlockSpec((n, LANES), lambda i: (0, ZB_GATES))]
                 + [full(GROUP_WIDTH)] * 0 + [_const_spec((1, GROUP_WIDTH))] * 4 + [_const_spec((1, LANES))]
                 + [full(GROUP_WIDTH)] * 2 + [_const_spec((GROUP_WIDTH, GROUP_WIDTH))],
        out_specs=[full(w) for w in outs],
        out_shape=[jax.ShapeDtypeStruct((n, w), F32) for w in outs],
        compiler_params=_params(("arbitrary",)),
        name="prep_sample",
    )(z, z, z, z, z, *norms, fbias, cos, sin, gsum)


FLASH_ROWS = 256


def _flash_kernel(qi_ref, kj_ref, q_ref, k_ref, v_ref, o_ref, m_ref, acc_ref):
    s_idx = pl.program_id(1)
    i = qi_ref[s_idx]
    j = kj_ref[s_idx]
    tq = q_ref.shape[0]

    @pl.when(j == 0)
    def _():
        m_ref[...] = jnp.full_like(m_ref, NEG)
        acc_ref[...] = jnp.zeros_like(acc_ref)

    def step(diagonal):
        tk = k_ref.shape[0]
        chains = [(h, r0) for h in range(N_HEADS) for r0 in range(0, tq, FLASH_ROWS)]
        state = [(m_ref[h, r0:r0 + FLASH_ROWS, :], acc_ref[h, r0:r0 + FLASH_ROWS, :]) for h, r0 in chains]
        nks = [min(tk, -(-(r0 + FLASH_ROWS) // LANES) * LANES) if diagonal else tk for _, r0 in chains]
        scores = [_dot_nt(q_ref[r0:r0 + FLASH_ROWS, h * LANES:(h + 1) * LANES], k_ref[0:nk, h * LANES:(h + 1) * LANES])
                  for (h, r0), nk in zip(chains, nks)]
        probs = []
        for (h, r0), nk, s, (m_prev, _) in zip(chains, nks, scores, state):
            if diagonal:
                r = lax.broadcasted_iota(jnp.int32, s.shape, 0) + r0
                c = lax.broadcasted_iota(jnp.int32, s.shape, 1)
                s = jnp.where(r >= c, s, NEG)
            m_new = jnp.maximum(m_prev, jnp.max(s, axis=-1, keepdims=True))
            p = jnp.concatenate([jnp.exp(s[:, c0:c0 + LANES] - m_new) for c0 in range(0, nk, LANES)], axis=-1)
            probs.append((m_new, p.astype(BF16)))
        results = []
        for (h, r0), nk, (m_new, p), (m_prev, acc_prev) in zip(chains, nks, probs, state):
            acc = acc_prev * jnp.exp(m_prev - m_new) + _dot(p, v_ref[0:nk, h * LANES:(h + 1) * LANES])
            results.append((m_new, acc))
        for (h, r0), (m_new, acc) in zip(chains, results):
            rows = slice(r0, r0 + FLASH_ROWS)
            acc_ref[h, rows, :] = acc
            m_ref[h, rows, :] = m_new
            if diagonal:
                o_ref[rows, h * HEAD_DIM:(h + 1) * HEAD_DIM] = acc[:, :HEAD_DIM] / acc[:, HEAD_DIM:HEAD_DIM + 1]

    @pl.when(j < i)
    def _():
        step(False)

    @pl.when(j == i)
    def _():
        step(True)


def _flash(qa, ka, va, nb, t, ta):
    nt = t // ta
    assert t % ta == 0
    pairs = [(i, j) for i in range(nt) for j in range(i + 1)]
    qi = jnp.asarray([p[0] for p in pairs], jnp.int32)
    kj = jnp.asarray([p[1] for p in pairs], jnp.int32)
    wide = 4 * LANES
    qspec = pl.BlockSpec((ta, wide), lambda b, s, qi, kj: (b * nt + qi[s], 0))
    kspec = pl.BlockSpec((ta, wide), lambda b, s, qi, kj: (b * nt + kj[s], 0))
    return pl.pallas_call(
        _flash_kernel,
        grid_spec=pltpu.PrefetchScalarGridSpec(
            num_scalar_prefetch=2,
            grid=(nb, len(pairs)),
            in_specs=[qspec, kspec, kspec],
            out_specs=pl.BlockSpec((ta, GROUP_WIDTH), lambda b, s, qi, kj: (b * nt + qi[s], 0)),
            scratch_shapes=[pltpu.VMEM((N_HEADS, ta, LANES), F32), pltpu.VMEM((N_HEADS, ta, LANES), F32)]),
        out_shape=jax.ShapeDtypeStruct((nb * t, GROUP_WIDTH), F32),
        compiler_params=_params(("arbitrary", "arbitrary")),
        name="flash_attn",
    )(qi, kj, qa, ka, va)


CHUNKS_PER_STEP = 4


def _chunk_masks():
    r = lax.broadcasted_iota(jnp.int32, (CHUNK, CHUNK), 0)
    c = lax.broadcasted_iota(jnp.int32, (CHUNK, CHUNK), 1)
    return r >= c, r > c, r == c


def _valid_rows(t_valid, t_padded, rows):
    if t_valid == t_padded:
        return None
    return pl.program_id(1) * rows + lax.broadcasted_iota(jnp.int32, (rows, 1), 0) < t_valid


def _chunks_per_step(n_chunks):
    return math.gcd(n_chunks, CHUNKS_PER_STEP)


def _rms_heads_out(x, w):
    return x * lax.rsqrt(jnp.mean(x * x, axis=-1, keepdims=True) + NORM_EPS) * w


def _gdn_kernel(x_ref, gate_ref, zg_ref, hist_ref, w_ref, arow_ref, dtrow_ref, ng_ref, s0_ref,
                o_ref, sout_ref, s_ref, xprev_ref, xp_ref, *, t_valid, t_padded, n_sub):
    n = pl.program_id(1)
    hd, gw = HEAD_DIM, GROUP_WIDTH

    @pl.when(n == 0)
    def _():
        s_ref[...] = s0_ref[...]
        xprev_ref[...] = hist_ref[...]

    rows = n_sub * CHUNK
    x = x_ref[...]
    xp_ref[0:SUBLANES, :] = xprev_ref[...]
    xp_ref[SUBLANES:, :] = x
    xprev_ref[...] = x[rows - SUBLANES:, :]
    first = SUBLANES - (GDN_CONV - 1)
    y = w_ref[0:1, :] * xp_ref[first:first + rows, :]
    for jj in range(1, GDN_CONV):
        y = y + w_ref[jj:jj + 1, :] * xp_ref[first + jj:first + jj + rows, :]
    qkv = y * _sigmoid(y)

    gates = zg_ref[...]
    g_all = -jnp.exp(arow_ref[...]) * _softplus(gates + dtrow_ref[...])
    beta_all = _sigmoid(gates)
    valid = _valid_rows(t_valid, t_padded, rows)
    if valid is not None:
        g_all = jnp.where(valid, g_all, 0.0)
    incl, strict, eye = _chunk_masks()
    tri = incl.astype(BF16)
    gate_act = gate_ref[...]
    gate_act = gate_act * _sigmoid(gate_act)

    chains = [(c, h) for c in range(n_sub) for h in range(N_HEADS)]
    gc_all = [_dot3_rhs(tri, g_all[c * CHUNK:(c + 1) * CHUNK]) for c in range(n_sub)]
    pre = []
    for c, h in chains:
        rs = slice(c * CHUNK, (c + 1) * CHUNK)
        q = qkv[rs, h * hd:(h + 1) * hd]
        k = qkv[rs, gw + h * hd:gw + (h + 1) * hd]
        v = qkv[rs, 2 * gw + h * hd:2 * gw + (h + 1) * hd]
        q = q * lax.rsqrt(jnp.sum(q * q, axis=-1, keepdims=True) + NORM_EPS) * QK_SCALE
        k = k * lax.rsqrt(jnp.sum(k * k, axis=-1, keepdims=True) + NORM_EPS)
        beta = beta_all[rs, GL_GDN_B + h:GL_GDN_B + h + 1]
        if valid is not None:
            k = jnp.where(valid[rs], k, 0.0)
            v = jnp.where(valid[rs], v, 0.0)
            beta = jnp.where(valid[rs], beta, 0.0)
        gc = gc_all[c][:, GL_GDN_A + h:GL_GDN_A + h + 1]
        decay = jnp.exp(jnp.where(incl, gc - _row_of_col(gc, CHUNK), NEG))
        pre.append((q, k, v, beta, gc, decay, k * beta, jnp.exp(gc)))
    kk = [_dot_nt(kb, k) for (_, k, _, _, _, _, kb, _) in pre]
    qk = [_dot_nt(q, k) for (q, k, _, _, _, _, _, _) in pre]
    lms = [jnp.where(strict, a * p[5], 0.0) for a, p in zip(kk, pre)]
    invs = [jnp.where(eye, 1.0, 0.0) - lm for lm in lms]
    pws = lms
    for _ in range((CHUNK - 1).bit_length() - 1):
        pws = [_dot(pw, pw) for pw in pws]
        invs = [inv + _dot(inv, pw) for inv, pw in zip(invs, pws)]
    uws = [_dot(inv, jnp.concatenate([p[2] * p[3], p[6] * p[7]], axis=-1)) for inv, p in zip(invs, pre)]
    outs = {}
    for c in range(n_sub):
        idx = [c * N_HEADS + h for h in range(N_HEADS)]
        states = [s_ref[h] for h in range(N_HEADS)]
        v_new = [uws[i][:, :hd] - _dot(uws[i][:, hd:], s) for i, s in zip(idx, states)]
        o_s = [_dot(pre[i][0] * pre[i][7], s) for i, s in zip(idx, states)]
        o_v = [_dot(qk[i] * pre[i][5], vn) for i, vn in zip(idx, v_new)]
        g_last = [pre[i][4][CHUNK - 1:CHUNK, :] for i in idx]
        upd = [_dot_tn(pre[i][1] * jnp.exp(gl - pre[i][4]), vn) for i, gl, vn in zip(idx, g_last, v_new)]
        for h in range(N_HEADS):
            s_ref[h] = states[h] * jnp.exp(g_last[h]) + upd[h]
            rs = slice(c * CHUNK, (c + 1) * CHUNK)
            outs[(c, h)] = _rms_heads_out(o_s[h] + o_v[h], ng_ref[...]) * gate_act[rs, h * hd:(h + 1) * hd]
    o_ref[...] = jnp.concatenate(
        [jnp.concatenate([outs[(c, h)] for h in range(N_HEADS)], axis=-1) for c in range(n_sub)], axis=0)

    @pl.when(n == pl.num_programs(1) - 1)
    def _():
        sout_ref[...] = s_ref[...]


def _gdn(z, nb, t_padded, t_valid, hist, conv_w8, arow, dtrow, ng, s0):
    assert t_padded % CHUNK == 0
    n_sub = _chunks_per_step(t_padded // CHUNK)
    rows = n_sub * CHUNK
    nc = t_padded // rows
    qkv_w = 3 * GROUP_WIDTH
    assert (ZB_GDN_QKV * GROUP_WIDTH) % qkv_w == 0
    state = pl.BlockSpec((None, N_HEADS, HEAD_DIM, HEAD_DIM), lambda b, n: (b, 0, 0, 0))
    return pl.pallas_call(
        functools.partial(_gdn_kernel, t_valid=t_valid, t_padded=t_padded, n_sub=n_sub),
        grid=(nb, nc),
        in_specs=[pl.BlockSpec((rows, qkv_w), lambda b, n: (b * nc + n, ZB_GDN_QKV * GROUP_WIDTH // qkv_w)),
                  pl.BlockSpec((rows, GROUP_WIDTH), lambda b, n: (b * nc + n, ZB_GDN_G)),
                  pl.BlockSpec((rows, LANES), lambda b, n: (b * nc + n, ZB_GATES)),
                  pl.BlockSpec((None, SUBLANES, qkv_w), lambda b, n: (b, 0, 0)),
                  _const_spec((SUBLANES, qkv_w)), _const_spec((1, LANES)), _const_spec((1, LANES)),
                  _const_spec((1, HEAD_DIM)), state],
        out_specs=[pl.BlockSpec((rows, GROUP_WIDTH), lambda b, n: (b * nc + n, 0)), state],
        out_shape=[jax.ShapeDtypeStruct((nb * t_padded, GROUP_WIDTH), F32),
                   jax.ShapeDtypeStruct((nb, N_HEADS, HEAD_DIM, HEAD_DIM), F32)],
        scratch_shapes=[pltpu.VMEM((N_HEADS, HEAD_DIM, HEAD_DIM), F32),
                        pltpu.VMEM((SUBLANES, qkv_w), F32),
                        pltpu.VMEM((rows + SUBLANES, qkv_w), F32)],
        compiler_params=_params(("arbitrary", "arbitrary")),
        name="gdn_chunks",
    )(z, z, z, hist, conv_w8, arow, dtrow, ng, s0)


def _mlstm_kernel(q_ref, k_ref, v_ref, og_ref, zg_ref, ibrow_ref, fbrow_ref, ng_ref, c0_ref, n0_ref, m0_ref,
                  o_ref, cout_ref, nout_ref, mout_ref, c_ref, n_ref, m_ref, *, t_valid, t_padded, n_sub):
    step = pl.program_id(1)
    hd = HEAD_DIM

    @pl.when(step == 0)
    def _():
        c_ref[...] = c0_ref[...]
        n_ref[...] = n0_ref[...]
        m_ref[...] = m0_ref[...]

    rows = n_sub * CHUNK
    gates = zg_ref[...]
    ig_all = gates + ibrow_ref[...]
    lf_all = _log_sigmoid(gates + fbrow_ref[...])
    valid = _valid_rows(t_valid, t_padded, rows)
    if valid is not None:
        ig_all = jnp.where(valid, ig_all, NEG)
        lf_all = jnp.where(valid, lf_all, 0.0)
    incl, _, _ = _chunk_masks()
    tri = incl.astype(BF16)
    og = _sigmoid(og_ref[...])

    chains = [(c, h) for c in range(n_sub) for h in range(N_HEADS)]
    b_all = [_dot3_rhs(tri, lf_all[c * CHUNK:(c + 1) * CHUNK]) for c in range(n_sub)]
    pre = []
    for c, h in chains:
        rs = slice(c * CHUNK, (c + 1) * CHUNK)
        sl = slice(h * hd, (h + 1) * hd)
        b = b_all[c][:, GL_ML_F + h:GL_ML_F + h + 1]
        ig = ig_all[rs, GL_ML_I + h:GL_ML_I + h + 1]
        dm = jnp.where(incl, b - _row_of_col(b, CHUNK) + _row_of_col(ig, CHUNK), NEG)
        b_last = b[CHUNK - 1:CHUNK, :]
        pre.append((q_ref[rs, sl], k_ref[rs, sl] * QK_SCALE, v_ref[rs, sl], b, dm,
                    jnp.max(dm, axis=-1, keepdims=True), b_last, b_last - b + ig))
    qk = [_dot_nt(p[0], p[1]) for p in pre]
    m_start, m_next = {}, {}
    for h in range(N_HEADS):
        m = m_ref[h][:, :1]
        for c in range(n_sub):
            _, _, _, _, _, _, b_last, a_end = pre[c * N_HEADS + h]
            m_start[(c, h)] = m
            m = jnp.maximum(b_last + m, jnp.max(a_end, axis=0, keepdims=True))
            m_next[(c, h)] = m
        m_ref[h] = jnp.broadcast_to(m, (1, LANES))
    m_t = [jnp.maximum(p[3] + m_start[ch], p[5]) for ch, p in zip(chains, pre)]
    inter = [jnp.exp(p[3] + m_start[ch] - mt) for ch, p, mt in zip(chains, pre, m_t)]
    wmat = [jnp.exp(p[4] - mt) * a for p, mt, a in zip(pre, m_t, qk)]
    wv = [_dot(w, p[2]) for w, p in zip(wmat, pre)]
    wk = [jnp.exp(p[7] - m_next[ch]) * p[1] for ch, p in zip(chains, pre)]
    sc = [jnp.exp(p[6] + m_start[ch] - m_next[ch]) for ch, p in zip(chains, pre)]
    upd = [_dot_tn(a, p[2]) for a, p in zip(wk, pre)]
    c_start, n_start = {}, {}
    for h in range(N_HEADS):
        cs = c_ref[h]
        ns = n_ref[h]
        for c in range(n_sub):
            i = c * N_HEADS + h
            c_start[i], n_start[i] = cs, ns
            cs = sc[i] * cs + upd[i]
            ns = sc[i] * ns + jnp.sum(wk[i], axis=0, keepdims=True)
        c_ref[h] = cs
        n_ref[h] = ns
    qc = [_dot(p[0], c_start[i]) for i, p in enumerate(pre)]
    outs = []
    for i, ((c, h), p) in enumerate(zip(chains, pre)):
        num = inter[i] * qc[i] + wv[i]
        den = inter[i] * jnp.sum(p[0] * n_start[i], axis=-1, keepdims=True) + jnp.sum(wmat[i], axis=-1, keepdims=True)
        hh = num / jnp.maximum(jnp.abs(den), jnp.exp(-m_t[i]))
        outs.append(_rms_heads_out(hh, ng_ref[...]) * og[c * CHUNK:(c + 1) * CHUNK, h * hd:(h + 1) * hd])
    o_ref[...] = jnp.concatenate(
        [jnp.concatenate(outs[c * N_HEADS:(c + 1) * N_HEADS], axis=-1) for c in range(n_sub)], axis=0)

    @pl.when(step == pl.num_programs(1) - 1)
    def _():
        cout_ref[...] = c_ref[...]
        nout_ref[...] = n_ref[...]
        mout_ref[...] = m_ref[...]


def _mlstm(z, nb, t_padded, t_valid, ibrow, fbrow, ng, c0, n0, m0):
    assert t_padded % CHUNK == 0
    n_sub = _chunks_per_step(t_padded // CHUNK)
    rows = n_sub * CHUNK
    nc = t_padded // rows
    zb = lambda blk: pl.BlockSpec((rows, GROUP_WIDTH), lambda b, n: (b * nc + n, blk))
    st = lambda r, w: pl.BlockSpec((None, N_HEADS, r, w), lambda b, n: (b, 0, 0, 0))
    shp = lambda r, w: jax.ShapeDtypeStruct((nb, N_HEADS, r, w), F32)
    return pl.pallas_call(
        functools.partial(_mlstm_kernel, t_valid=t_valid, t_padded=t_padded, n_sub=n_sub),
        grid=(nb, nc),
        in_specs=[zb(ZB_ML_Q), zb(ZB_ML_K), zb(ZB_ML_V), zb(ZB_ML_O),
                  pl.BlockSpec((rows, LANES), lambda b, n: (b * nc + n, ZB_GATES)),
                  _const_spec((1, LANES)), _const_spec((1, LANES)), _const_spec((1, HEAD_DIM)),
                  st(HEAD_DIM, HEAD_DIM), st(1, HEAD_DIM), st(1, LANES)],
        out_specs=[pl.BlockSpec((rows, GROUP_WIDTH), lambda b, n: (b * nc + n, 0)),
                   st(HEAD_DIM, HEAD_DIM), st(1, HEAD_DIM), st(1, LANES)],
        out_shape=[jax.ShapeDtypeStruct((nb * t_padded, GROUP_WIDTH), F32),
                   shp(HEAD_DIM, HEAD_DIM), shp(1, HEAD_DIM), shp(1, LANES)],
        scratch_shapes=[pltpu.VMEM((N_HEADS, HEAD_DIM, HEAD_DIM), F32),
                        pltpu.VMEM((N_HEADS, 1, HEAD_DIM), F32),
                        pltpu.VMEM((N_HEADS, 1, LANES), F32)],
        compiler_params=_params(("arbitrary", "arbitrary")),
        name="mlstm_chunks",
    )(z, z, z, z, z, ibrow, fbrow, ng, c0, n0, m0)


N_QROWS = 16
PAGES_PER_STEP = 8


def _suffix_scan(x):
    lane = lax.broadcasted_iota(jnp.int32, x.shape, 1)
    y = x
    s = 1
    while s < LANES:
        y = y + jnp.where(lane + s < LANES, pltpu.roll(y, LANES - s, 1), 0.0)
        s *= 2
    return y


def _dup_rows(x8):
    return jnp.concatenate([x8, x8], axis=0)


def _col_of_row(row, n):
    r = lax.broadcasted_iota(jnp.int32, (n, row.shape[1]), 0)
    c = lax.broadcasted_iota(jnp.int32, (n, row.shape[1]), 1)
    return jnp.sum(jnp.where(r == c, row, 0.0), axis=1, keepdims=True)


def _row_of_col_padded(col):
    r = lax.broadcasted_iota(jnp.int32, (N_QROWS, LANES), 0)
    c = lax.broadcasted_iota(jnp.int32, (N_QROWS, LANES), 1)
    return jnp.sum(jnp.where(r == c, col, 0.0), axis=0, keepdims=True)


def _decode_kernel(*refs, fox, n_pages_step, n_blocks):
    pt_ref = refs[0]
    del pt_ref
    q_ref, knew_ref, vnew_ref = refs[1:4]
    pos = 4
    lfnew_ref = qt_ref = None
    if fox:
        lfnew_ref = refs[pos]
    else:
        qt_ref = refs[pos]
    pos += 1
    k_refs = refs[pos:pos + n_pages_step]
    pos += n_pages_step
    v_refs = refs[pos:pos + n_pages_step]
    pos += n_pages_step
    lf_refs = ()
    if fox:
        lf_refs = refs[pos:pos + n_pages_step]
        pos += n_pages_step
    o_ref = refs[pos]
    scratch = refs[pos + 1:]
    g = pl.program_id(1)
    last = pl.num_programs(1) - 1

    row = lax.broadcasted_iota(jnp.int32, (N_QROWS, GROUP_WIDTH), 0)
    lane_w = lax.broadcasted_iota(jnp.int32, (N_QROWS, GROUP_WIDTH), 1)
    qbd = jnp.where(lane_w // HEAD_DIM == row % N_HEADS, q_ref[...], 0.0)
    qbd16 = qbd.astype(BF16)
    tok = lax.broadcasted_iota(jnp.int32, (N_QROWS, LANES), 0) // N_HEADS
    key = lax.broadcasted_iota(jnp.int32, (N_QROWS, LANES), 1)
    new_ok = key <= tok

    def scores(kt_page):
        return _dot(qbd16, kt_page.astype(BF16))

    def weighted_values(p, vt_page):
        return _dot_nt(p.astype(BF16), vt_page.astype(BF16))

    if fox:
        m_ref, l_ref, acc_ref, run_ref, sq_ref = scratch

        def online(s_list, v_list):
            s_all = jnp.concatenate(s_list, axis=-1) if len(s_list) > 1 else s_list[0]
            m_prev = m_ref[:, :1]
            m_new = jnp.maximum(m_prev, jnp.max(s_all, axis=-1, keepdims=True))
            alpha = jnp.exp(m_prev - m_new)
            p = jnp.exp(s_all - m_new)
            acc = acc_ref[...] * alpha
            for i, v_page in enumerate(v_list):
                acc = acc + weighted_values(p[:, i * LANES:(i + 1) * LANES], v_page)
            acc_ref[...] = acc
            l_ref[...] = jnp.broadcast_to(l_ref[:, :1] * alpha + jnp.sum(p, axis=-1, keepdims=True), (N_QROWS, LANES))
            m_ref[...] = jnp.broadcast_to(m_new, (N_QROWS, LANES))

        @pl.when(g == 0)
        def _():
            m_ref[...] = jnp.full_like(m_ref, NEG)
            l_ref[...] = jnp.zeros_like(l_ref)
            acc_ref[...] = jnp.zeros_like(acc_ref)
            lf8 = lfnew_ref[...]
            incl = _suffix_scan(lf8)
            excl16 = _dup_rows(incl - lf8)
            sq = jnp.sum(jnp.where(key == tok, excl16, 0.0), axis=-1, keepdims=True)
            sq_ref[...] = jnp.broadcast_to(sq, (N_QROWS, LANES))
            run_ref[...] = jnp.broadcast_to(_dup_rows(incl[:, :1]), (N_QROWS, LANES))
            s = jnp.where(new_ok, scores(knew_ref[...]) + excl16 - sq, NEG)
            online([s], [vnew_ref[...]])

        run = run_ref[:, :1]
        sq = sq_ref[:, :1]
        s_list, v_list = [], []
        for i in range(n_pages_step):
            lf8 = lf_refs[i][...]
            incl = _suffix_scan(lf8)
            s_list.append(scores(k_refs[i][...]) + _dup_rows(incl - lf8) + (run - sq))
            run = run + _dup_rows(incl[:, :1])
            v_list.append(v_refs[i][...])
        run_ref[...] = jnp.broadcast_to(run, (N_QROWS, LANES))
        online(s_list, v_list)

        @pl.when(g == last)
        def _():
            o_ref[...] = acc_ref[...] / l_ref[:, :1]
    else:
        opart_ref, mt_ref, lt_ref, gt_ref, mown_ref, lown_ref, oown_ref = scratch
        blk_row = lax.broadcasted_iota(jnp.int32, (n_blocks, LANES), 0)

        @pl.when(g == 0)
        def _():
            mt_ref[...] = jnp.zeros_like(mt_ref)
            lt_ref[...] = jnp.zeros_like(lt_ref)
            gt_ref[...] = jnp.zeros_like(gt_ref)
            s = jnp.where(new_ok, scores(knew_ref[...]), NEG)
            m = jnp.max(s, axis=-1, keepdims=True)
            p = jnp.exp(s - m)
            mown_ref[...] = jnp.broadcast_to(m, (N_QROWS, LANES))
            lown_ref[...] = jnp.broadcast_to(jnp.sum(p, axis=-1, keepdims=True), (N_QROWS, LANES))
            oown_ref[...] = weighted_values(p, vnew_ref[...])

        crow = lax.broadcasted_iota(jnp.int32, (GROUP_WIDTH, LANES), 0)
        qlane = lax.broadcasted_iota(jnp.int32, (GROUP_WIDTH, LANES), 1)
        qbd_t = jnp.where(crow // HEAD_DIM == qlane % N_HEADS, qt_ref[...], 0.0)
        pages_per_block = MOBA_BLOCK // LANES
        for bi in range(n_pages_step // pages_per_block):
            n = g * (n_pages_step // pages_per_block) + bi
            kp = [k_refs[bi * pages_per_block + e][...] for e in range(pages_per_block)]
            vp = [v_refs[bi * pages_per_block + e][...] for e in range(pages_per_block)]
            s = jnp.concatenate([scores(k) for k in kp], axis=-1)
            m = jnp.max(s, axis=-1, keepdims=True)
            p = jnp.exp(s - m)
            l = jnp.sum(p, axis=-1, keepdims=True)
            o = weighted_values(p[:, :LANES], vp[0])
            for e in range(1, pages_per_block):
                o = o + weighted_values(p[:, e * LANES:(e + 1) * LANES], vp[e])
            kmean = jnp.sum(sum(kp), axis=-1, keepdims=True) * (1.0 / MOBA_BLOCK)
            gate = jnp.sum(qbd_t * kmean, axis=0, keepdims=True)
            opart_ref[n] = o
            here = blk_row == n
            mt_ref[...] = jnp.where(here, _row_of_col_padded(m), mt_ref[...])
            lt_ref[...] = jnp.where(here, _row_of_col_padded(l), lt_ref[...])
            gt_ref[...] = jnp.where(here, gate, gt_ref[...])

        @pl.when(g == last)
        def _():
            gt = gt_ref[...]
            sel = jnp.zeros(gt.shape, F32)
            for _ in range(MOBA_TOPK):
                mx = jnp.max(gt, axis=0, keepdims=True)
                first = jnp.min(jnp.where(gt == mx, blk_row, n_blocks), axis=0, keepdims=True)
                pick = blk_row == first
                sel = jnp.where(pick, 1.0, sel)
                gt = jnp.where(pick, -jnp.inf, gt)
            mt = jnp.where(sel > 0.0, mt_ref[...], NEG)
            m_own = _row_of_col_padded(mown_ref[:, :1])
            l_own = _row_of_col_padded(lown_ref[:, :1])
            m_all = jnp.maximum(jnp.max(mt, axis=0, keepdims=True), m_own)
            w = jnp.where(sel > 0.0, jnp.exp(mt - m_all), 0.0)
            w_own = jnp.exp(m_own - m_all)
            denom = jnp.sum(w * lt_ref[...], axis=0, keepdims=True) + w_own * l_own
            w = w / denom
            out = oown_ref[...] * _col_of_row(w_own / denom, N_QROWS)
            for n in range(n_blocks):
                out = out + opart_ref[n] * _col_of_row(w[n:n + 1, :], N_QROWS)
            o_ref[...] = out


def _decode(page_table, layer, q16, knew_t, vnew_t, extra, cache_kt, cache_vt, cache_lf8, fox):
    nb, n_pages = page_table.shape
    pps = PAGES_PER_STEP
    assert n_pages % pps == 0 and cache_kt.shape[2:] == (GROUP_WIDTH, LANES)
    n_blocks = n_pages * LANES // MOBA_BLOCK
    assert (n_pages * LANES) % MOBA_BLOCK == 0 and n_blocks >= MOBA_TOPK and n_blocks % SUBLANES == 0
    steps = n_pages // pps

    def page_spec(i, rows):
        if fox:
            idx = lambda b, g, pt: (layer, pt[b * n_pages + n_pages - 1 - (g * pps + i)], 0, 0)
        else:
            idx = lambda b, g, pt: (layer, pt[b * n_pages + g * pps + i], 0, 0)
        return pl.BlockSpec((None, None, rows, LANES), idx)

    seq = lambda r, w: pl.BlockSpec((None, r, w), lambda b, g, pt: (b, 0, 0))
    in_specs = [seq(N_QROWS, GROUP_WIDTH), seq(GROUP_WIDTH, LANES), seq(GROUP_WIDTH, LANES),
                seq(SUBLANES if fox else GROUP_WIDTH, LANES)]
    args = [q16, knew_t, vnew_t, extra]
    in_specs += [page_spec(i, GROUP_WIDTH) for i in range(pps)] * 2
    args += [cache_kt] * pps + [cache_vt] * pps
    if fox:
        in_specs += [page_spec(i, SUBLANES) for i in range(pps)]
        args += [cache_lf8] * pps
        scratch = [pltpu.VMEM((N_QROWS, LANES), F32), pltpu.VMEM((N_QROWS, LANES), F32),
                   pltpu.VMEM((N_QROWS, GROUP_WIDTH), F32), pltpu.VMEM((N_QROWS, LANES), F32),
                   pltpu.VMEM((N_QROWS, LANES), F32)]
    else:
        scratch = [pltpu.VMEM((n_blocks, N_QROWS, GROUP_WIDTH), F32)] + [pltpu.VMEM((n_blocks, LANES), F32)] * 3 + [
            pltpu.VMEM((N_QROWS, LANES), F32), pltpu.VMEM((N_QROWS, LANES), F32),
            pltpu.VMEM((N_QROWS, GROUP_WIDTH), F32)]
    return pl.pallas_call(
        functools.partial(_decode_kernel, fox=fox, n_pages_step=pps, n_blocks=n_blocks),
        grid_spec=pltpu.PrefetchScalarGridSpec(
            num_scalar_prefetch=1, grid=(nb, steps), in_specs=in_specs,
            out_specs=seq(N_QROWS, GROUP_WIDTH), scratch_shapes=scratch),
        out_shape=jax.ShapeDtypeStruct((nb, N_QROWS, GROUP_WIDTH), F32),
        compiler_params=_params(("arbitrary", "arbitrary")),
        name="fox_decode" if fox else "moba_decode",
    )(page_table.reshape(-1), *args)


def _pack_w_in(w):
    gw, nh = GROUP_WIDTH, N_HEADS
    widths = [('fox_q', gw), ('fox_k', gw), ('fox_v', gw), ('fox_f', nh), ('gdn_qkv', 3 * gw), ('gdn_a', nh),
              ('gdn_b', nh), ('gdn_g', gw), ('moba_q', gw), ('moba_k', gw), ('moba_v', gw), ('mlstm_q', gw),
              ('mlstm_k', gw), ('mlstm_v', gw), ('mlstm_i', nh), ('mlstm_f', nh), ('mlstm_o', gw)]
    cols, off = {}, 0
    for name, width in widths:
        cols[name] = w[:, off:off + width]
        off += width
    assert off == w.shape[1]
    wide = [cols[k] for k in ('fox_q', 'fox_k', 'fox_v', 'gdn_qkv', 'gdn_g', 'moba_q', 'moba_k', 'moba_v',
                              'mlstm_q', 'mlstm_k', 'mlstm_v', 'mlstm_o')]
    gates = [cols[k] for k in ('fox_f', 'gdn_a', 'gdn_b', 'mlstm_i', 'mlstm_f')]
    pad = jnp.zeros((w.shape[0], LANES - 5 * nh), w.dtype)
    return jnp.concatenate(wide + gates + [pad], axis=1).astype(BF16)


def _gate_row(vec, lane0):
    return jnp.zeros((1, LANES), F32).at[0, lane0:lane0 + N_HEADS].set(vec.astype(F32))


def _tile_heads(vec):
    return jnp.tile(vec.astype(F32), N_HEADS).reshape(1, GROUP_WIDTH)


def _rope_tables(pos):
    half = HEAD_DIM // 2
    inv_freq = jnp.power(ROPE_THETA, -jnp.arange(half, dtype=F32) / half)
    ang = pos.astype(F32)[:, None] * inv_freq[None, :]
    cos = jnp.cos(ang)
    sin = jnp.sin(ang)
    cos_h = jnp.concatenate([cos, cos], axis=-1)
    sin_h = jnp.concatenate([-sin, sin], axis=-1)
    return jnp.tile(cos_h, (1, N_HEADS)), jnp.tile(sin_h, (1, N_HEADS))


def _head_sum_matrix():
    lane = jnp.arange(GROUP_WIDTH)
    return (lane[:, None] // HEAD_DIM == lane[None, :] // HEAD_DIM).astype(BF16)


def _lower_tri(n):
    r = jnp.arange(n)
    return (r[:, None] >= r[None, :]).astype(BF16)


def _pad_rows(a, rows, front=0):
    return jnp.pad(a, ((0, 0), (front, rows - front - a.shape[1]), (0, 0)))


ROW_TILE = 256
ATTN_TILE = 512


def _layer_weights(w_in, w_out, w_gate, w_up, w_down):
    return (_pack_w_in(w_in), w_out.astype(BF16), w_gate.astype(BF16), w_up.astype(BF16), w_down.astype(BF16))


def _zcols(z, blk, width=GROUP_WIDTH):
    return z[:, blk * GROUP_WIDTH:blk * GROUP_WIDTH + width]


def _prompt_layer(x, lw, lp, consts):
    nb, t, d = x.shape
    wp, wo, wg, wu, wd = lw
    x2 = x.reshape(nb * t, d)
    z = _proj(x2, lp['ln1'], wp, ROW_TILE)
    fqa, fka, fva, fk32, flf, mqa, mka, mva, mk32 = _prep_prompt(
        z, nb, t, lp['attn_norms'], lp['fox_fb_row'], consts['cos_p'], consts['sin_p'], consts['gsum'], consts['tri'])
    o_a = _flash(fqa, fka, fva, nb, t, ATTN_TILE)
    o_c = _flash(mqa, mka, mva, nb, t, ATTN_TILE)
    zeros_state = jnp.zeros((nb, N_HEADS, HEAD_DIM, HEAD_DIM), F32)
    o_b, gdn_s = _gdn(z, nb, t, t, jnp.zeros((nb, SUBLANES, 3 * GROUP_WIDTH), F32), lp['conv_w8'],
                      lp['gdn_a_row'], lp['gdn_dt_row'], lp['gdn_ng'], zeros_state)
    o_d, ml_c, ml_n, ml_m = _mlstm(z, nb, t, t, lp['ml_ib_row'], lp['ml_fb_row'], lp['ml_ng'], zeros_state,
                                   jnp.zeros((nb, N_HEADS, 1, HEAD_DIM), F32), jnp.zeros((nb, N_HEADS, 1, LANES), F32))
    y = _out_ffn(x2, (o_a, o_b, o_c, o_d), wo, lp['ln2'], wg, wu, wd, ROW_TILE).reshape(nb, t, d)
    heads = lambda a: a.reshape(nb, t, N_HEADS, HEAD_DIM)
    z3 = z.reshape(nb, t, Z_WIDTH)
    qkv0 = ZB_GDN_QKV * GROUP_WIDTH
    states = (heads(fk32), heads(_zcols(z, ZB_FOX_V)), flf[:, :N_HEADS].reshape(nb, t, N_HEADS),
              heads(mk32), heads(_zcols(z, ZB_MOBA_V)),
              z3[:, t - (GDN_CONV - 1):, qkv0:qkv0 + 3 * GROUP_WIDTH], gdn_s,
              ml_c, ml_n[:, :, 0, :], ml_m[:, :, 0, 0])
    return y, states


def _sample_layer(x, layer, lw, lp, consts, caches, page_table, st):
    nb, t, d = x.shape
    assert t <= N_HEADS and t >= GDN_CONV - 1 and nb * t % SUBLANES == 0
    wp, wo, wg, wu, wd = lw
    x2 = x.reshape(nb * t, d)
    z = _proj(x2, lp['ln1'], wp, nb * t)
    fq32, fk32, flf, mq32, mk32 = _prep_sample(z, lp['attn_norms'], lp['fox_fb_row'], consts['cos_s'], consts['sin_s'],
                                               consts['gsum'])
    seq = lambda a: a.reshape(nb, t, a.shape[-1])
    tok_rows = lambda a: _pad_rows(jnp.repeat(seq(a), N_HEADS, axis=1), N_QROWS)
    new_page = lambda a: jnp.pad(jnp.swapaxes(seq(a), 1, 2), ((0, 0), (0, 0), (0, LANES - t)))
    rows_t = lambda a: jnp.pad(jnp.swapaxes(tok_rows(a), 1, 2), ((0, 0), (0, 0), (0, LANES - N_QROWS)))
    lf_new = jnp.swapaxes(seq(flf[:, :N_HEADS]), 1, 2)
    lf_new = jnp.pad(lf_new, ((0, 0), (0, 0), (0, LANES - t)))
    lf_new8 = jnp.concatenate([lf_new, lf_new], axis=1)
    fox_k, fox_v, fox_lf8, moba_k, moba_v = caches

    def own_head(o16):
        o5 = o16.reshape(nb, N_HEADS, N_HEADS, N_HEADS, HEAD_DIM)[:, :t]
        hh = jnp.arange(N_HEADS)
        return o5[:, :, hh, hh, :].reshape(nb * t, GROUP_WIDTH)

    o_a = own_head(_decode(page_table, layer, tok_rows(fq32), new_page(fk32), new_page(_zcols(z, ZB_FOX_V)),
                           lf_new8, fox_k, fox_v, fox_lf8, True))
    o_c = own_head(_decode(page_table, layer, tok_rows(mq32), new_page(mk32), new_page(_zcols(z, ZB_MOBA_V)),
                           rows_t(mq32), moba_k, moba_v, None, False))
    zp = _pad_rows(seq(z), CHUNK).reshape(nb * CHUNK, Z_WIDTH)
    conv_buf, gdn_s0, ml_c0, ml_n0, ml_m0 = st
    real_rows = lambda a: a.reshape(nb, CHUNK, GROUP_WIDTH)[:, :t].reshape(nb * t, GROUP_WIDTH)
    o_b, gdn_s = _gdn(zp, nb, CHUNK, t, _pad_rows(conv_buf, SUBLANES, front=SUBLANES - (GDN_CONV - 1)), lp['conv_w8'],
                      lp['gdn_a_row'], lp['gdn_dt_row'], lp['gdn_ng'], gdn_s0)
    o_d, ml_c, ml_n, ml_m = _mlstm(zp, nb, CHUNK, t, lp['ml_ib_row'], lp['ml_fb_row'], lp['ml_ng'], ml_c0,
                                   ml_n0[:, :, None, :],
                                   jnp.broadcast_to(ml_m0[:, :, None, None], (nb, N_HEADS, 1, LANES)))
    y = _out_ffn(x2, (o_a, real_rows(o_b), o_c, real_rows(o_d)), wo, lp['ln2'], wg, wu, wd, nb * t).reshape(nb, t, d)
    heads = lambda a: a.reshape(nb, t, N_HEADS, HEAD_DIM)
    qkv0 = ZB_GDN_QKV * GROUP_WIDTH
    conv_rows = jnp.concatenate([conv_buf, seq(z)[:, :, qkv0:qkv0 + 3 * GROUP_WIDTH]], axis=1)
    states = (heads(fk32), heads(_zcols(z, ZB_FOX_V)), seq(flf[:, :N_HEADS]),
              heads(mk32), heads(_zcols(z, ZB_MOBA_V)),
              conv_rows[:, conv_rows.shape[1] - (GDN_CONV - 1):], gdn_s,
              ml_c, ml_n[:, :, 0, :], ml_m[:, :, 0, 0])
    return y, states


def kernel(x_prompt, x_sample, cache_fox_k, cache_fox_v, cache_fox_logf, cache_moba_k, cache_moba_v,
           state_gdn_conv, state_gdn_s, state_mlstm_c, state_mlstm_n, state_mlstm_m, page_table,
           ln1, w_in, fox_q_norm, fox_k_norm, fox_f_bias, gdn_conv_w, gdn_a_log, gdn_dt_bias, gdn_norm,
           moba_q_norm, moba_k_norm, mlstm_i_bias, mlstm_f_bias, mlstm_norm, w_out, ln2,
           w_gate, w_up, w_down):
    depth = w_in.shape[0]
    t_prompt = x_prompt.shape[1]
    nb_s, t_sample = x_sample.shape[:2]
    n_pool, page = cache_fox_k.shape[1:3]
    past_len = page_table.shape[1] * page
    assert page == LANES and past_len % MOBA_BLOCK == 0
    cos_p, sin_p = _rope_tables(jnp.arange(t_prompt, dtype=jnp.int32))
    cos_s, sin_s = _rope_tables(past_len + jnp.arange(t_sample, dtype=jnp.int32))
    consts = {'cos_p': cos_p, 'sin_p': sin_p, 'cos_s': jnp.tile(cos_s, (nb_s, 1)), 'sin_s': jnp.tile(sin_s, (nb_s, 1)),
              'gsum': _head_sum_matrix(), 'tri': _lower_tri(MOBA_BLOCK)}
    pages = lambda c: jnp.transpose(c, (0, 1, 3, 4, 2)).reshape(depth, n_pool, GROUP_WIDTH, page)
    lf_t = jnp.swapaxes(cache_fox_logf, 2, 3)
    caches = (pages(cache_fox_k), pages(cache_fox_v), jnp.concatenate([lf_t, lf_t], axis=2),
              pages(cache_moba_k), pages(cache_moba_v))
    y_p, y_s = x_prompt, x_sample
    p_states, s_states = [], []
    for i in range(depth):
        lw = _layer_weights(w_in[i], w_out[i], w_gate[i], w_up[i], w_down[i])
        lp = {
            'ln1': ln1[i], 'ln2': ln2[i],
            'attn_norms': [_tile_heads(v[i]) for v in (fox_q_norm, fox_k_norm, moba_q_norm, moba_k_norm)],
            'fox_fb_row': _gate_row(fox_f_bias[i], GL_FOX_F),
            'conv_w8': jnp.pad(gdn_conv_w[i].astype(F32), ((0, SUBLANES - GDN_CONV), (0, 0))),
            'gdn_a_row': _gate_row(gdn_a_log[i], GL_GDN_A), 'gdn_dt_row': _gate_row(gdn_dt_bias[i], GL_GDN_A),
            'gdn_ng': gdn_norm[i].reshape(1, HEAD_DIM).astype(F32),
            'ml_ib_row': _gate_row(mlstm_i_bias[i], GL_ML_I), 'ml_fb_row': _gate_row(mlstm_f_bias[i], GL_ML_F),
            'ml_ng': mlstm_norm[i].reshape(1, HEAD_DIM).astype(F32),
        }
        y_p, st_p = _prompt_layer(y_p, lw, lp, consts)
        st = (state_gdn_conv[i], state_gdn_s[i], state_mlstm_c[i], state_mlstm_n[i], state_mlstm_m[i])
        y_s, st_s = _sample_layer(y_s, i, lw, lp, consts, caches, page_table, st)
        p_states.append(st_p)
        s_states.append(st_s)
    stack = lambda states: [jnp.stack(a) for a in zip(*states)]
    return (y_p, y_s, *stack(p_states), *stack(s_states))
```

```python
import functools
import math

import jax
import jax.numpy as jnp
from jax import lax
from jax.experimental import pallas as pl
from jax.experimental.pallas import tpu as pltpu

F32 = jnp.float32
BF16 = jnp.bfloat16

HEAD_DIM = 64
N_HEADS = 4
GROUP_WIDTH = N_HEADS * HEAD_DIM
LANES = 128
SUBLANES = 8
MOBA_BLOCK = 256
MOBA_TOPK = 3
MOBA_SEL_LANES = 32
GDN_CONV = 4
CHUNK = 64
NORM_EPS = 1e-6
ROPE_THETA = 10000.0
NEG = -1e30
QK_SCALE = HEAD_DIM ** -0.5
VMEM_LIMIT = 48 * 1024 * 1024

ZB_FOX_Q, ZB_FOX_K, ZB_FOX_V = 0, 1, 2
ZB_GDN_QKV = 3
ZB_GDN_G = 6
ZB_MOBA_Q, ZB_MOBA_K, ZB_MOBA_V = 7, 8, 9
ZB_ML_Q, ZB_ML_K, ZB_ML_V, ZB_ML_O = 10, 11, 12, 13
Z_WIDE = 14 * GROUP_WIDTH
Z_WIDTH = Z_WIDE + LANES
ZB_GATES = Z_WIDE // LANES
GL_FOX_F, GL_GDN_A, GL_GDN_B, GL_ML_I, GL_ML_F = 0, 4, 8, 12, 16


def _dot(a, b):
    return jnp.dot(a, b, preferred_element_type=F32)


def _dot_nt(a, b):
    return lax.dot_general(a, b, (((1,), (1,)), ((), ())), preferred_element_type=F32)


def _dot_tn(a, b):
    return lax.dot_general(a, b, (((0,), (0,)), ((), ())), preferred_element_type=F32)


def _split3(x):
    hi = x.astype(BF16)
    r1 = x - hi.astype(F32)
    mid = r1.astype(BF16)
    lo = (r1 - mid.astype(F32)).astype(BF16)
    return hi, mid, lo


def _dot3_rhs(a_bf16, x):
    hi, mid, lo = _split3(x)
    return _dot(a_bf16, hi) + _dot(a_bf16, mid) + _dot(a_bf16, lo)


def _dot3_lhs(x, b_bf16):
    hi, mid, lo = _split3(x)
    return _dot(hi, b_bf16) + _dot(mid, b_bf16) + _dot(lo, b_bf16)


def _sigmoid(x):
    return 1.0 / (1.0 + jnp.exp(-x))


def _log_sigmoid(x):
    return jnp.minimum(x, 0.0) - jnp.log1p(jnp.exp(-jnp.abs(x)))


def _softplus(x):
    return jnp.maximum(x, 0.0) + jnp.log1p(jnp.exp(-jnp.abs(x)))


def _row_of_col(col, n):
    r = lax.broadcasted_iota(jnp.int32, (n, n), 0)
    c = lax.broadcasted_iota(jnp.int32, (n, n), 1)
    return jnp.sum(jnp.where(r == c, col, 0.0), axis=0, keepdims=True)


def _params(sem):
    return pltpu.CompilerParams(dimension_semantics=sem, vmem_limit_bytes=VMEM_LIMIT)


def _const_spec(shape):
    nd = len(shape)
    return pl.BlockSpec(shape, lambda *_: (0,) * nd, pipeline_mode=pl.Buffered(1))


PROJ_COL_CHUNK = 512


def _proj_kernel(x_ref, g_ref, w_ref, z_ref):
    x = x_ref[...]
    ms = jnp.mean(x * x, axis=-1, keepdims=True)
    h = (x * lax.rsqrt(ms + NORM_EPS) * g_ref[...]).astype(BF16)
    for c0 in range(0, Z_WIDTH, PROJ_COL_CHUNK):
        c1 = min(c0 + PROJ_COL_CHUNK, Z_WIDTH)
        z_ref[:, c0:c1] = _dot_nt(h, w_ref[c0:c1, :])


def _proj(x2d, g, w_packed, tm):
    n, d = x2d.shape
    return pl.pallas_call(
        _proj_kernel,
        grid=(n // tm,),
        in_specs=[pl.BlockSpec((tm, d), lambda i: (i, 0)),
                  _const_spec((1, d)),
                  _const_spec((Z_WIDTH, d))],
        out_specs=pl.BlockSpec((tm, Z_WIDTH), lambda i: (i, 0)),
        out_shape=jax.ShapeDtypeStruct((n, Z_WIDTH), F32),
        compiler_params=_params(("arbitrary",)),
        name="norm_proj",
    )(x2d, g.reshape(1, d), w_packed)


FFN_COL_CHUNK = 256


def _out_ffn_kernel(x_ref, oa_ref, ob_ref, oc_ref, od_ref, wo_ref, g2_ref, wg_ref, wu_ref, wd_ref, y_ref):
    x1 = x_ref[...]
    for gi, o_ref in enumerate((oa_ref, ob_ref, oc_ref, od_ref)):
        x1 = x1 + _dot(o_ref[...].astype(BF16), wo_ref[gi * GROUP_WIDTH:(gi + 1) * GROUP_WIDTH, :])
    ms = jnp.mean(x1 * x1, axis=-1, keepdims=True)
    h2 = (x1 * lax.rsqrt(ms + NORM_EPS) * g2_ref[...]).astype(BF16)
    y_ref[...] = x1
    d_ff = wg_ref.shape[1]
    for c0 in range(0, d_ff, FFN_COL_CHUNK):
        c1 = c0 + FFN_COL_CHUNK
        gate = _dot(h2, wg_ref[:, c0:c1])
        up = _dot(h2, wu_ref[:, c0:c1])
        act = (gate * _sigmoid(gate) * up).astype(BF16)
        y_ref[...] += _dot(act, wd_ref[c0:c1, :])


def _out_ffn(x2d, mix, wo, g2, wg, wu, wd, tm):
    n, d = x2d.shape
    d_ff = wg.shape[1]
    assert d_ff % FFN_COL_CHUNK == 0
    row = lambda w: pl.BlockSpec((tm, w), lambda i: (i, 0))
    return pl.pallas_call(
        _out_ffn_kernel,
        grid=(n // tm,),
        in_specs=[row(d)] + [row(GROUP_WIDTH)] * 4 + [
            _const_spec((d, d)), _const_spec((1, d)), _const_spec((d, d_ff)),
            _const_spec((d, d_ff)), _const_spec((d_ff, d))],
        out_specs=row(d),
        out_shape=jax.ShapeDtypeStruct((n, d), F32),
        compiler_params=_params(("arbitrary",)),
        name="out_ffn",
    )(x2d, *mix, wo, g2.reshape(1, d), wg, wu, wd)


def _lane_iota(rows):
    return lax.broadcasted_iota(jnp.int32, (rows, LANES), 1)


def _head_rms(x, w, gsum):
    y = x * x
    hi = y.astype(BF16)
    lo = (y - hi.astype(F32)).astype(BF16)
    ss = _dot(hi, gsum) + _dot(lo, gsum)
    return x * lax.rsqrt(ss * (1.0 / HEAD_DIM) + NORM_EPS) * w


def _rotary(x, cos, sin_signed):
    half = HEAD_DIM // 2
    out = []
    for p in range(GROUP_WIDTH // LANES):
        sl = slice(p * LANES, (p + 1) * LANES)
        v = x[:, sl]
        lane = _lane_iota(v.shape[0])
        partner = jnp.where(lane % HEAD_DIM < half,
                            pltpu.roll(v, LANES - half, 1), pltpu.roll(v, half, 1))
        out.append(v * cos[:, sl] + partner * sin_signed[:, sl])
    return jnp.concatenate(out, axis=-1)


def _head_group(x, h, aug):
    p, e = divmod(h, 2)
    piece = x[:, p * LANES:(p + 1) * LANES]
    if e:
        piece = pltpu.roll(piece, HEAD_DIM, 1)
    return jnp.where(_lane_iota(x.shape[0]) < HEAD_DIM, piece, aug)


def _prep_prompt_kernel(zfq, zfk, zfv, zmq, zmk, zmv, zg, fqn, fkn, mqn, mkn, fb, cos, sin, gsum, tri,
                        fqa, fka, fva, fk32, flf, mqa, mka, mva, mk32, carry_ref, kmean_ref):
    j = pl.program_id(1)
    tm = zfq.shape[0]
    lane = _lane_iota(tm)
    g = gsum[...]

    @pl.when(j == 0)
    def _():
        carry_ref[...] = jnp.zeros_like(carry_ref)
        kmean_ref[...] = jnp.zeros_like(kmean_ref)

    fq = _head_rms(zfq[...], fqn[...], g) * QK_SCALE
    fk = _head_rms(zfk[...], fkn[...], g)
    fk32[...] = fk
    lf = _log_sigmoid(zg[...] + fb[...])
    flf[...] = lf
    c = _dot3_rhs(tri[...], lf) + carry_ref[...]
    carry_ref[...] = c[tm - 1:tm, :]
    fv = zfv[...]
    for h in range(N_HEADS):
        col = c[:, GL_FOX_F + h:GL_FOX_F + h + 1]
        hi = col.astype(BF16).astype(F32)
        r1 = col - hi
        mid = r1.astype(BF16).astype(F32)
        lo = (r1 - mid).astype(BF16).astype(F32)
        base = HEAD_DIM
        one_q = jnp.where((lane >= base + 3) & (lane < base + 6), 1.0, 0.0)
        aug_q = jnp.where(lane == base, hi, jnp.where(lane == base + 1, mid, jnp.where(lane == base + 2, lo, one_q)))
        one_k = jnp.where((lane >= base) & (lane < base + 3), 1.0, 0.0)
        aug_k = jnp.where(lane == base + 3, -hi, jnp.where(lane == base + 4, -mid, jnp.where(lane == base + 5, -lo, one_k)))
        aug_v = jnp.where(lane == base, 1.0, 0.0)
        sl = slice(h * LANES, (h + 1) * LANES)
        fqa[:, sl] = _head_group(fq, h, aug_q).astype(BF16)
        fka[:, sl] = _head_group(fk, h, aug_k).astype(BF16)
        fva[:, sl] = _head_group(fv, h, aug_v).astype(BF16)

    mq = _rotary(_head_rms(zmq[...], mqn[...], g), cos[...], sin[...])
    mk = _rotary(_head_rms(zmk[...], mkn[...], g), cos[...], sin[...])
    mk32[...] = mk
    mv = zmv[...]
    km = kmean_ref[...]
    lane_w = lax.broadcasted_iota(jnp.int32, km.shape, 1)
    kbd = jnp.concatenate([jnp.where(lane_w // HEAD_DIM == h, km, 0.0) for h in range(N_HEADS)], axis=0)
    qh = mq.astype(BF16)
    ql = (mq - qh.astype(F32)).astype(BF16)
    kh = kbd.astype(BF16)
    kl = (kbd - kh.astype(F32)).astype(BF16)
    gate = _dot_nt(qh, kh) + _dot_nt(qh, kl) + _dot_nt(ql, kh)
    blk = lane % MOBA_SEL_LANES
    gate = jnp.where(blk < j, gate, NEG)
    sel = jnp.zeros((tm, LANES), F32)
    for h in range(N_HEADS):
        gh = jnp.where(lane // MOBA_SEL_LANES == h, gate, -jnp.inf)
        for _ in range(MOBA_TOPK):
            mx = jnp.max(gh, axis=-1, keepdims=True)
            first = jnp.min(jnp.where(gh == mx, lane, 2 * LANES), axis=-1, keepdims=True)
            pick = lane == first
            sel = jnp.where(pick, 1.0, sel)
            gh = jnp.where(pick, -jnp.inf, gh)
    pen = jnp.where((sel > 0.0) | (blk == j), 0.0, NEG)
    in_aug = (lane >= HEAD_DIM) & (lane < HEAD_DIM + MOBA_SEL_LANES)
    aug_k = jnp.where(in_aug & (lane - HEAD_DIM == j), 1.0, 0.0)
    aug_v = jnp.where(lane == HEAD_DIM, 1.0, 0.0)
    mqs = mq * QK_SCALE
    for h in range(N_HEADS):
        shift = (HEAD_DIM - MOBA_SEL_LANES * h) % LANES
        moved = pltpu.roll(pen, shift, 1) if shift else pen
        aug_q = jnp.where(in_aug, moved, 0.0)
        sl = slice(h * LANES, (h + 1) * LANES)
        mqa[:, sl] = _head_group(mqs, h, aug_q).astype(BF16)
        mka[:, sl] = _head_group(mk, h, aug_k).astype(BF16)
        mva[:, sl] = _head_group(mv, h, aug_v).astype(BF16)
    row = lax.broadcasted_iota(jnp.int32, km.shape, 0)
    kmean_ref[...] = jnp.where(row == j, jnp.mean(mk, axis=0, keepdims=True), km)


def _prep_prompt(z, nb, t, norms, fbias, cos, sin, gsum, tri):
    tm = MOBA_BLOCK
    nt = t // tm
    assert t % tm == 0 and nt <= MOBA_SEL_LANES
    n = nb * t
    zb = lambda blk: pl.BlockSpec((tm, GROUP_WIDTH), lambda b, j: (b * nt + j, blk))
    rowspec = lambda w: pl.BlockSpec((tm, w), lambda b, j: (b * nt + j, 0))
    wide = 4 * LANES
    outs = [(wide, BF16)] * 3 + [(GROUP_WIDTH, F32), (LANES, F32)] + [(wide, BF16)] * 3 + [(GROUP_WIDTH, F32)]
    return pl.pallas_call(
        _prep_prompt_kernel,
        grid=(nb, nt),
        in_specs=[zb(ZB_FOX_Q), zb(ZB_FOX_K), zb(ZB_FOX_V), zb(ZB_MOBA_Q), zb(ZB_MOBA_K), zb(ZB_MOBA_V),
                  pl.BlockSpec((tm, LANES), lambda b, j: (b * nt + j, ZB_GATES))]
                 + [_const_spec((1, GROUP_WIDTH))] * 4 + [_const_spec((1, LANES))]
                 + [pl.BlockSpec((tm, GROUP_WIDTH), lambda b, j: (j, 0))] * 2
                 + [_const_spec((GROUP_WIDTH, GROUP_WIDTH)), _const_spec((tm, tm))],
        out_specs=[rowspec(w) for w, _ in outs],
        out_shape=[jax.ShapeDtypeStruct((n, w), dt) for w, dt in outs],
        scratch_shapes=[pltpu.VMEM((1, LANES), F32), pltpu.VMEM((MOBA_SEL_LANES, GROUP_WIDTH), F32)],
        compiler_params=_params(("arbitrary", "arbitrary")),
        name="prep_prompt",
    )(z, z, z, z, z, z, z, *norms, fbias, cos, sin, gsum, tri)


def _prep_sample_kernel(zfq, zfk, zmq, zmk, zg, fqn, fkn, mqn, mkn, fb, cos, sin, gsum,
                        fq32, fk32, flf, mq32, mk32):
    g = gsum[...]
    fq32[...] = _head_rms(zfq[...], fqn[...], g) * QK_SCALE
    fk32[...] = _head_rms(zfk[...], fkn[...], g)
    flf[...] = _log_sigmoid(zg[...] + fb[...])
    mq32[...] = _rotary(_head_rms(zmq[...], mqn[...], g), cos[...], sin[...]) * QK_SCALE
    mk32[...] = _rotary(_head_rms(zmk[...], mkn[...], g), cos[...], sin[...])


def _prep_sample(z, norms, fbias, cos, sin, gsum):
    n = z.shape[0]
    zb = lambda blk: pl.BlockSpec((n, GROUP_WIDTH), lambda i: (0, blk))
    full = lambda w: pl.BlockSpec((n, w), lambda i: (0, 0))
    outs = [GROUP_WIDTH, GROUP_WIDTH, LANES, GROUP_WIDTH, GROUP_WIDTH]
    return pl.pallas_call(
        _prep_sample_kernel,
        grid=(1,),
        in_specs=[zb(ZB_FOX_Q), zb(ZB_FOX_K), zb(ZB_MOBA_Q), zb(ZB_MOBA_K),
                  pl.BlockSpec((n, LANES), lambda i: (0, ZB_GATES))]
                 + [full(GROUP_WIDTH)] * 0 + [_const_spec((1, GROUP_WIDTH))] * 4 + [_const_spec((1, LANES))]
                 + [full(GROUP_WIDTH)] * 2 + [_const_spec((GROUP_WIDTH, GROUP_WIDTH))],
        out_specs=[full(w) for w in outs],
        out_shape=[jax.ShapeDtypeStruct((n, w), F32) for w in outs],
        compiler_params=_params(("arbitrary",)),
        name="prep_sample",
    )(z, z, z, z, z, *norms, fbias, cos, sin, gsum)


FLASH_ROWS = 256


def _flash_kernel(qi_ref, kj_ref, q_ref, k_ref, v_ref, o_ref, m_ref, acc_ref):
    s_idx = pl.program_id(1)
    i = qi_ref[s_idx]
    j = kj_ref[s_idx]
    tq = q_ref.shape[0]

    @pl.when(j == 0)
    def _():
        m_ref[...] = jnp.full_like(m_ref, NEG)
        acc_ref[...] = jnp.zeros_like(acc_ref)

    def step(diagonal):
        tk = k_ref.shape[0]
        chains = [(h, r0) for h in range(N_HEADS) for r0 in range(0, tq, FLASH_ROWS)]
        state = [(m_ref[h, r0:r0 + FLASH_ROWS, :], acc_ref[h, r0:r0 + FLASH_ROWS, :]) for h, r0 in chains]
        nks = [min(tk, -(-(r0 + FLASH_ROWS) // LANES) * LANES) if diagonal else tk for _, r0 in chains]
        scores = [_dot_nt(q_ref[r0:r0 + FLASH_ROWS, h * LANES:(h + 1) * LANES], k_ref[0:nk, h * LANES:(h + 1) * LANES])
                  for (h, r0), nk in zip(chains, nks)]
        probs = []
        for (h, r0), nk, s, (m_prev, _) in zip(chains, nks, scores, state):
            if diagonal:
                r = lax.broadcasted_iota(jnp.int32, s.shape, 0) + r0
                c = lax.broadcasted_iota(jnp.int32, s.shape, 1)
                s = jnp.where(r >= c, s, NEG)
            m_new = jnp.maximum(m_prev, jnp.max(s, axis=-1, keepdims=True))
            p = jnp.concatenate([jnp.exp(s[:, c0:c0 + LANES] - m_new) for c0 in range(0, nk, LANES)], axis=-1)
            probs.append((m_new, p.astype(BF16)))
        results = []
        for (h, r0), nk, (m_new, p), (m_prev, acc_prev) in zip(chains, nks, probs, state):
            acc = acc_prev * jnp.exp(m_prev - m_new) + _dot(p, v_ref[0:nk, h * LANES:(h + 1) * LANES])
            results.append((m_new, acc))
        for (h, r0), (m_new, acc) in zip(chains, results):
            rows = slice(r0, r0 + FLASH_ROWS)
            acc_ref[h, rows, :] = acc
            m_ref[h, rows, :] = m_new
            if diagonal:
                o_ref[rows, h * HEAD_DIM:(h + 1) * HEAD_DIM] = acc[:, :HEAD_DIM] / acc[:, HEAD_DIM:HEAD_DIM + 1]

    @pl.when(j < i)
    def _():
        step(False)

    @pl.when(j == i)
    def _():
        step(True)


def _flash(qa, ka, va, nb, t, ta):
    nt = t // ta
    assert t % ta == 0
    pairs = [(i, j) for i in range(nt) for j in range(i + 1)]
    qi = jnp.asarray([p[0] for p in pairs], jnp.int32)
    kj = jnp.asarray([p[1] for p in pairs], jnp.int32)
    wide = 4 * LANES
    qspec = pl.BlockSpec((ta, wide), lambda b, s, qi, kj: (b * nt + qi[s], 0))
    kspec = pl.BlockSpec((ta, wide), lambda b, s, qi, kj: (b * nt + kj[s], 0))
    return pl.pallas_call(
        _flash_kernel,
        grid_spec=pltpu.PrefetchScalarGridSpec(
            num_scalar_prefetch=2,
            grid=(nb, len(pairs)),
            in_specs=[qspec, kspec, kspec],
            out_specs=pl.BlockSpec((ta, GROUP_WIDTH), lambda b, s, qi, kj: (b * nt + qi[s], 0)),
            scratch_shapes=[pltpu.VMEM((N_HEADS, ta, LANES), F32), pltpu.VMEM((N_HEADS, ta, LANES), F32)]),
        out_shape=jax.ShapeDtypeStruct((nb * t, GROUP_WIDTH), F32),
        compiler_params=_params(("arbitrary", "arbitrary")),
        name="flash_attn",
    )(qi, kj, qa, ka, va)


CHUNKS_PER_STEP = 4


def _chunk_masks():
    r = lax.broadcasted_iota(jnp.int32, (CHUNK, CHUNK), 0)
    c = lax.broadcasted_iota(jnp.int32, (CHUNK, CHUNK), 1)
    return r >= c, r > c, r == c


def _valid_rows(t_valid, t_padded, rows):
    if t_valid == t_padded:
        return None
    return pl.program_id(1) * rows + lax.broadcasted_iota(jnp.int32, (rows, 1), 0) < t_valid


def _chunks_per_step(n_chunks):
    return math.gcd(n_chunks, CHUNKS_PER_STEP)


def _rms_heads_out(x, w):
    return x * lax.rsqrt(jnp.mean(x * x, axis=-1, keepdims=True) + NORM_EPS) * w


def _gdn_kernel(x_ref, gate_ref, zg_ref, hist_ref, w_ref, arow_ref, dtrow_ref, ng_ref, s0_ref,
                o_ref, sout_ref, s_ref, xprev_ref, xp_ref, *, t_valid, t_padded, n_sub):
    n = pl.program_id(1)
    hd, gw = HEAD_DIM, GROUP_WIDTH

    @pl.when(n == 0)
    def _():
        s_ref[...] = s0_ref[...]
        xprev_ref[...] = hist_ref[...]

    rows = n_sub * CHUNK
    x = x_ref[...]
    xp_ref[0:SUBLANES, :] = xprev_ref[...]
    xp_ref[SUBLANES:, :] = x
    xprev_ref[...] = x[rows - SUBLANES:, :]
    first = SUBLANES - (GDN_CONV - 1)
    y = w_ref[0:1, :] * xp_ref[first:first + rows, :]
    for jj in range(1, GDN_CONV):
        y = y + w_ref[jj:jj + 1, :] * xp_ref[first + jj:first + jj + rows, :]
    qkv = y * _sigmoid(y)

    gates = zg_ref[...]
    g_all = -jnp.exp(arow_ref[...]) * _softplus(gates + dtrow_ref[...])
    beta_all = _sigmoid(gates)
    valid = _valid_rows(t_valid, t_padded, rows)
    if valid is not None:
        g_all = jnp.where(valid, g_all, 0.0)
    incl, strict, eye = _chunk_masks()
    tri = incl.astype(BF16)
    gate_act = gate_ref[...]
    gate_act = gate_act * _sigmoid(gate_act)

    chains = [(c, h) for c in range(n_sub) for h in range(N_HEADS)]
    gc_all = [_dot3_rhs(tri, g_all[c * CHUNK:(c + 1) * CHUNK]) for c in range(n_sub)]
    pre = []
    for c, h in chains:
        rs = slice(c * CHUNK, (c + 1) * CHUNK)
        q = qkv[rs, h * hd:(h + 1) * hd]
        k = qkv[rs, gw + h * hd:gw + (h + 1) * hd]
        v = qkv[rs, 2 * gw + h * hd:2 * gw + (h + 1) * hd]
        q = q * lax.rsqrt(jnp.sum(q * q, axis=-1, keepdims=True) + NORM_EPS) * QK_SCALE
        k = k * lax.rsqrt(jnp.sum(k * k, axis=-1, keepdims=True) + NORM_EPS)
        beta = beta_all[rs, GL_GDN_B + h:GL_GDN_B + h + 1]
        if valid is not None:
            k = jnp.where(valid[rs], k, 0.0)
            v = jnp.where(valid[rs], v, 0.0)
            beta = jnp.where(valid[rs], beta, 0.0)
        gc = gc_all[c][:, GL_GDN_A + h:GL_GDN_A + h + 1]
        decay = jnp.exp(jnp.where(incl, gc - _row_of_col(gc, CHUNK), NEG))
        pre.append((q, k, v, beta, gc, decay, k * beta, jnp.exp(gc)))
    kk = [_dot_nt(kb, k) for (_, k, _, _, _, _, kb, _) in pre]
    qk = [_dot_nt(q, k) for (q, k, _, _, _, _, _, _) in pre]
    lms = [jnp.where(strict, a * p[5], 0.0) for a, p in zip(kk, pre)]
    invs = [jnp.where(eye, 1.0, 0.0) - lm for lm in lms]
    pws = lms
    for _ in range((CHUNK - 1).bit_length() - 1):
        pws = [_dot(pw, pw) for pw in pws]
        invs = [inv + _dot(inv, pw) for inv, pw in zip(invs, pws)]
    uws = [_dot(inv, jnp.concatenate([p[2] * p[3], p[6] * p[7]], axis=-1)) for inv, p in zip(invs, pre)]
    outs = {}
    for c in range(n_sub):
        idx = [c * N_HEADS + h for h in range(N_HEADS)]
        states = [s_ref[h] for h in range(N_HEADS)]
        v_new = [uws[i][:, :hd] - _dot(uws[i][:, hd:], s) for i, s in zip(idx, states)]
        o_s = [_dot(pre[i][0] * pre[i][7], s) for i, s in zip(idx, states)]
        o_v = [_dot(qk[i] * pre[i][5], vn) for i, vn in zip(idx, v_new)]
        g_last = [pre[i][4][CHUNK - 1:CHUNK, :] for i in idx]
        upd = [_dot_tn(pre[i][1] * jnp.exp(gl - pre[i][4]), vn) for i, gl, vn in zip(idx, g_last, v_new)]
        for h in range(N_HEADS):
            s_ref[h] = states[h] * jnp.exp(g_last[h]) + upd[h]
            rs = slice(c * CHUNK, (c + 1) * CHUNK)
            outs[(c, h)] = _rms_heads_out(o_s[h] + o_v[h], ng_ref[...]) * gate_act[rs, h * hd:(h + 1) * hd]
    o_ref[...] = jnp.concatenate(
        [jnp.concatenate([outs[(c, h)] for h in range(N_HEADS)], axis=-1) for c in range(n_sub)], axis=0)

    @pl.when(n == pl.num_programs(1) - 1)
    def _():
        sout_ref[...] = s_ref[...]


def _gdn(z, nb, t_padded, t_valid, hist, conv_w8, arow, dtrow, ng, s0):
    assert t_padded % CHUNK == 0
    n_sub = _chunks_per_step(t_padded // CHUNK)
    rows = n_sub * CHUNK
    nc = t_padded // rows
    qkv_w = 3 * GROUP_WIDTH
    assert (ZB_GDN_QKV * GROUP_WIDTH) % qkv_w == 0
    state = pl.BlockSpec((None, N_HEADS, HEAD_DIM, HEAD_DIM), lambda b, n: (b, 0, 0, 0))
    return pl.pallas_call(
        functools.partial(_gdn_kernel, t_valid=t_valid, t_padded=t_padded, n_sub=n_sub),
        grid=(nb, nc),
        in_specs=[pl.BlockSpec((rows, qkv_w), lambda b, n: (b * nc + n, ZB_GDN_QKV * GROUP_WIDTH // qkv_w)),
                  pl.BlockSpec((rows, GROUP_WIDTH), lambda b, n: (b * nc + n, ZB_GDN_G)),
                  pl.BlockSpec((rows, LANES), lambda b, n: (b * nc + n, ZB_GATES)),
                  pl.BlockSpec((None, SUBLANES, qkv_w), lambda b, n: (b, 0, 0)),
                  _const_spec((SUBLANES, qkv_w)), _const_spec((1, LANES)), _const_spec((1, LANES)),
                  _const_spec((1, HEAD_DIM)), state],
        out_specs=[pl.BlockSpec((rows, GROUP_WIDTH), lambda b, n: (b * nc + n, 0)), state],
        out_shape=[jax.ShapeDtypeStruct((nb * t_padded, GROUP_WIDTH), F32),
                   jax.ShapeDtypeStruct((nb, N_HEADS, HEAD_DIM, HEAD_DIM), F32)],
        scratch_shapes=[pltpu.VMEM((N_HEADS, HEAD_DIM, HEAD_DIM), F32),
                        pltpu.VMEM((SUBLANES, qkv_w), F32),
                        pltpu.VMEM((rows + SUBLANES, qkv_w), F32)],
        compiler_params=_params(("arbitrary", "arbitrary")),
        name="gdn_chunks",
    )(z, z, z, hist, conv_w8, arow, dtrow, ng, s0)


def _mlstm_kernel(q_ref, k_ref, v_ref, og_ref, zg_ref, ibrow_ref, fbrow_ref, ng_ref, c0_ref, n0_ref, m0_ref,
                  o_ref, cout_ref, nout_ref, mout_ref, c_ref, n_ref, m_ref, *, t_valid, t_padded, n_sub):
    step = pl.program_id(1)
    hd = HEAD_DIM

    @pl.when(step == 0)
    def _():
        c_ref[...] = c0_ref[...]
        n_ref[...] = n0_ref[...]
        m_ref[...] = m0_ref[...]

    rows = n_sub * CHUNK
    gates = zg_ref[...]
    ig_all = gates + ibrow_ref[...]
    lf_all = _log_sigmoid(gates + fbrow_ref[...])
    valid = _valid_rows(t_valid, t_padded, rows)
    if valid is not None:
        ig_all = jnp.where(valid, ig_all, NEG)
        lf_all = jnp.where(valid, lf_all, 0.0)
    incl, _, _ = _chunk_masks()
    tri = incl.astype(BF16)
    og = _sigmoid(og_ref[...])

    chains = [(c, h) for c in range(n_sub) for h in range(N_HEADS)]
    b_all = [_dot3_rhs(tri, lf_all[c * CHUNK:(c + 1) * CHUNK]) for c in range(n_sub)]
    pre = []
    for c, h in chains:
        rs = slice(c * CHUNK, (c + 1) * CHUNK)
        sl = slice(h * hd, (h + 1) * hd)
        b = b_all[c][:, GL_ML_F + h:GL_ML_F + h + 1]
        ig = ig_all[rs, GL_ML_I + h:GL_ML_I + h + 1]
        dm = jnp.where(incl, b - _row_of_col(b, CHUNK) + _row_of_col(ig, CHUNK), NEG)
        b_last = b[CHUNK - 1:CHUNK, :]
        pre.append((q_ref[rs, sl], k_ref[rs, sl] * QK_SCALE, v_ref[rs, sl], b, dm,
                    jnp.max(dm, axis=-1, keepdims=True), b_last, b_last - b + ig))
    qk = [_dot_nt(p[0], p[1]) for p in pre]
    m_start, m_next = {}, {}
    for h in range(N_HEADS):
        m = m_ref[h][:, :1]
        for c in range(n_sub):
            _, _, _, _, _, _, b_last, a_end = pre[c * N_HEADS + h]
            m_start[(c, h)] = m
            m = jnp.maximum(b_last + m, jnp.max(a_end, axis=0, keepdims=True))
            m_next[(c, h)] = m
        m_ref[h] = jnp.broadcast_to(m, (1, LANES))
    m_t = [jnp.maximum(p[3] + m_start[ch], p[5]) for ch, p in zip(chains, pre)]
    inter = [jnp.exp(p[3] + m_start[ch] - mt) for ch, p, mt in zip(chains, pre, m_t)]
    wmat = [jnp.exp(p[4] - mt) * a for p, mt, a in zip(pre, m_t, qk)]
    wv = [_dot(w, p[2]) for w, p in zip(wmat, pre)]
    wk = [jnp.exp(p[7] - m_next[ch]) * p[1] for ch, p in zip(chains, pre)]
    sc = [jnp.exp(p[6] + m_start[ch] - m_next[ch]) for ch, p in zip(chains, pre)]
    upd = [_dot_tn(a, p[2]) for a, p in zip(wk, pre)]
    c_start, n_start = {}, {}
    for h in range(N_HEADS):
        cs = c_ref[h]
        ns = n_ref[h]
        for c in range(n_sub):
            i = c * N_HEADS + h
            c_start[i], n_start[i] = cs, ns
            cs = sc[i] * cs + upd[i]
            ns = sc[i] * ns + jnp.sum(wk[i], axis=0, keepdims=True)
        c_ref[h] = cs
        n_ref[h] = ns
    qc = [_dot(p[0], c_start[i]) for i, p in enumerate(pre)]
    outs = []
    for i, ((c, h), p) in enumerate(zip(chains, pre)):
        num = inter[i] * qc[i] + wv[i]
        den = inter[i] * jnp.sum(p[0] * n_start[i], axis=-1, keepdims=True) + jnp.sum(wmat[i], axis=-1, keepdims=True)
        hh = num / jnp.maximum(jnp.abs(den), jnp.exp(-m_t[i]))
        outs.append(_rms_heads_out(hh, ng_ref[...]) * og[c * CHUNK:(c + 1) * CHUNK, h * hd:(h + 1) * hd])
    o_ref[...] = jnp.concatenate(
        [jnp.concatenate(outs[c * N_HEADS:(c + 1) * N_HEADS], axis=-1) for c in range(n_sub)], axis=0)

    @pl.when(step == pl.num_programs(1) - 1)
    def _():
        cout_ref[...] = c_ref[...]
        nout_ref[...] = n_ref[...]
        mout_ref[...] = m_ref[...]


def _mlstm(z, nb, t_padded, t_valid, ibrow, fbrow, ng, c0, n0, m0):
    assert t_padded % CHUNK == 0
    n_sub = _chunks_per_step(t_padded // CHUNK)
    rows = n_sub * CHUNK
    nc = t_padded // rows
    zb = lambda blk: pl.BlockSpec((rows, GROUP_WIDTH), lambda b, n: (b * nc + n, blk))
    st = lambda r, w: pl.BlockSpec((None, N_HEADS, r, w), lambda b, n: (b, 0, 0, 0))
    shp = lambda r, w: jax.ShapeDtypeStruct((nb, N_HEADS, r, w), F32)
    return pl.pallas_call(
        functools.partial(_mlstm_kernel, t_valid=t_valid, t_padded=t_padded, n_sub=n_sub),
        grid=(nb, nc),
        in_specs=[zb(ZB_ML_Q), zb(ZB_ML_K), zb(ZB_ML_V), zb(ZB_ML_O),
                  pl.BlockSpec((rows, LANES), lambda b, n: (b * nc + n, ZB_GATES)),
                  _const_spec((1, LANES)), _const_spec((1, LANES)), _const_spec((1, HEAD_DIM)),
                  st(HEAD_DIM, HEAD_DIM), st(1, HEAD_DIM), st(1, LANES)],
        out_specs=[pl.BlockSpec((rows, GROUP_WIDTH), lambda b, n: (b * nc + n, 0)),
                   st(HEAD_DIM, HEAD_DIM), st(1, HEAD_DIM), st(1, LANES)],
        out_shape=[jax.ShapeDtypeStruct((nb * t_padded, GROUP_WIDTH), F32),
                   shp(HEAD_DIM, HEAD_DIM), shp(1, HEAD_DIM), shp(1, LANES)],
        scratch_shapes=[pltpu.VMEM((N_HEADS, HEAD_DIM, HEAD_DIM), F32),
                        pltpu.VMEM((N_HEADS, 1, HEAD_DIM), F32),
                        pltpu.VMEM((N_HEADS, 1, LANES), F32)],
        compiler_params=_params(("arbitrary", "arbitrary")),
        name="mlstm_chunks",
    )(z, z, z, z, z, ibrow, fbrow, ng, c0, n0, m0)


N_QROWS = 16
PAGES_PER_STEP = 16


def _dup_rows(x8):
    return jnp.concatenate([x8, x8], axis=0)


def _col_of_row(row, n):
    r = lax.broadcasted_iota(jnp.int32, (n, row.shape[1]), 0)
    c = lax.broadcasted_iota(jnp.int32, (n, row.shape[1]), 1)
    return jnp.sum(jnp.where(r == c, row, 0.0), axis=1, keepdims=True)


def _row_of_col_padded(col):
    r = lax.broadcasted_iota(jnp.int32, (N_QROWS, LANES), 0)
    c = lax.broadcasted_iota(jnp.int32, (N_QROWS, LANES), 1)
    return jnp.sum(jnp.where(r == c, col, 0.0), axis=0, keepdims=True)


def _decode_kernel(*refs, fox, n_pages_step, n_blocks):
    pt_ref = refs[0]
    del pt_ref
    q_ref, knew_ref, vnew_ref = refs[1:4]
    pos = 4
    lfnew_ref = qt_ref = sfx_ref = None
    if fox:
        lfnew_ref, sfx_ref = refs[pos], refs[pos + 1]
        pos += 2
    else:
        qt_ref = refs[pos]
        pos += 1
    k_refs = refs[pos:pos + n_pages_step]
    pos += n_pages_step
    v_refs = refs[pos:pos + n_pages_step]
    pos += n_pages_step
    lf_refs = ()
    if fox:
        lf_refs = refs[pos:pos + n_pages_step]
        pos += n_pages_step
    o_ref = refs[pos]
    scratch = refs[pos + 1:]
    g = pl.program_id(1)
    last = pl.num_programs(1) - 1

    row = lax.broadcasted_iota(jnp.int32, (N_QROWS, GROUP_WIDTH), 0)
    lane_w = lax.broadcasted_iota(jnp.int32, (N_QROWS, GROUP_WIDTH), 1)
    qbd = jnp.where(lane_w // HEAD_DIM == row % N_HEADS, q_ref[...], 0.0)
    qbd16 = qbd.astype(BF16)
    tok = lax.broadcasted_iota(jnp.int32, (N_QROWS, LANES), 0) // N_HEADS
    key = lax.broadcasted_iota(jnp.int32, (N_QROWS, LANES), 1)
    new_ok = key <= tok

    def scores(kt_page):
        return _dot(qbd16, kt_page.astype(BF16))

    def weighted_values(p, vt_page):
        return _dot_nt(p.astype(BF16), vt_page.astype(BF16))

    if fox:
        m_ref, l_ref, acc_ref, run_ref, sq_ref = scratch

        def suffix_sums(lf_rows):
            both = _dot3_lhs(lf_rows, sfx_ref[...])
            return both[:, :LANES], both[:, LANES:]

        def online(s_list, v_list):
            mx = s_list[0]
            for s in s_list[1:]:
                mx = jnp.maximum(mx, s)
            m_prev = m_ref[...]
            m_new = jnp.maximum(m_prev, jnp.max(mx, axis=-1, keepdims=True))
            alpha = jnp.exp(m_prev - m_new)
            p_list = [jnp.exp(s - m_new) for s in s_list]
            pv = [weighted_values(p, v_page) for p, v_page in zip(p_list, v_list)]
            acc = acc_ref[...] * jnp.concatenate([alpha, alpha], axis=-1)
            for o in pv:
                acc = acc + o
            acc_ref[...] = acc
            l_ref[...] = l_ref[...] * alpha + jnp.sum(sum(p_list), axis=-1, keepdims=True)
            m_ref[...] = m_new

        @pl.when(g == 0)
        def _():
            m_ref[...] = jnp.full_like(m_ref, NEG)
            l_ref[...] = jnp.zeros_like(l_ref)
            acc_ref[...] = jnp.zeros_like(acc_ref)
            excl, tot = suffix_sums(lfnew_ref[...])
            excl16 = _dup_rows(excl)
            sq = jnp.broadcast_to(jnp.sum(jnp.where(key == tok, excl16, 0.0), axis=-1, keepdims=True),
                                  (N_QROWS, LANES))
            sq_ref[...] = sq
            run_ref[...] = _dup_rows(tot)
            s = jnp.where(new_ok, scores(knew_ref[...]) + excl16 - sq, NEG)
            online([s], [vnew_ref[...]])

        excl_all, tot_all = suffix_sums(jnp.concatenate([r[...] for r in lf_refs], axis=0))
        raw = [scores(k_refs[i][...]) for i in range(n_pages_step)]
        run = run_ref[...]
        sq = sq_ref[...]
        s_list = []
        for i in range(n_pages_step):
            rs = slice(i * SUBLANES, (i + 1) * SUBLANES)
            s_list.append(raw[i] + _dup_rows(excl_all[rs]) + (run - sq))
            run = run + _dup_rows(tot_all[rs])
        run_ref[...] = run
        online(s_list, [v_refs[i][...] for i in range(n_pages_step)])

        @pl.when(g == last)
        def _():
            l = l_ref[...]
            o_ref[...] = acc_ref[...] / jnp.concatenate([l, l], axis=-1)
    else:
        opart_ref, mt_ref, lt_ref, gt_ref, mown_ref, lown_ref, oown_ref = scratch
        blk_row = lax.broadcasted_iota(jnp.int32, (n_blocks, LANES), 0)

        @pl.when(g == 0)
        def _():
            mt_ref[...] = jnp.zeros_like(mt_ref)
            lt_ref[...] = jnp.zeros_like(lt_ref)
            gt_ref[...] = jnp.zeros_like(gt_ref)
            s = jnp.where(new_ok, scores(knew_ref[...]), NEG)
            m = jnp.max(s, axis=-1, keepdims=True)
            p = jnp.exp(s - m)
            mown_ref[...] = jnp.broadcast_to(m, (N_QROWS, LANES))
            lown_ref[...] = jnp.broadcast_to(jnp.sum(p, axis=-1, keepdims=True), (N_QROWS, LANES))
            oown_ref[...] = weighted_values(p, vnew_ref[...])

        crow = lax.broadcasted_iota(jnp.int32, (GROUP_WIDTH, LANES), 0)
        qlane = lax.broadcasted_iota(jnp.int32, (GROUP_WIDTH, LANES), 1)
        qbd_t = jnp.where(crow // HEAD_DIM == qlane % N_HEADS, qt_ref[...], 0.0)
        pages_per_block = MOBA_BLOCK // LANES
        blocks_step = n_pages_step // pages_per_block
        raw = [scores(k_refs[i][...]) for i in range(n_pages_step)]
        stats = []
        for bi in range(blocks_step):
            s_pages = raw[bi * pages_per_block:(bi + 1) * pages_per_block]
            mx = s_pages[0]
            for s in s_pages[1:]:
                mx = jnp.maximum(mx, s)
            m = jnp.max(mx, axis=-1, keepdims=True)
            p_pages = [jnp.exp(s - m) for s in s_pages]
            stats.append((m, jnp.sum(sum(p_pages), axis=-1, keepdims=True), p_pages))
        pv = [[weighted_values(p, v_refs[bi * pages_per_block + e][...]) for e, p in enumerate(st[2])]
              for bi, st in enumerate(stats)]
        mt, lt, gt = mt_ref[...], lt_ref[...], gt_ref[...]
        for bi in range(blocks_step):
            n = g * blocks_step + bi
            m, l, _ = stats[bi]
            ksum = sum(k_refs[bi * pages_per_block + e][...] for e in range(pages_per_block))
            kmean = jnp.sum(ksum, axis=-1, keepdims=True) * (1.0 / MOBA_BLOCK)
            gate = jnp.sum(qbd_t * kmean, axis=0, keepdims=True)
            opart_ref[n] = sum(pv[bi])
            here = blk_row == n
            mt = jnp.where(here, _row_of_col_padded(m), mt)
            lt = jnp.where(here, _row_of_col_padded(l), lt)
            gt = jnp.where(here, gate, gt)
        mt_ref[...] = mt
        lt_ref[...] = lt
        gt_ref[...] = gt

        @pl.when(g == last)
        def _():
            gt = gt_ref[...]
            sel = jnp.zeros(gt.shape, F32)
            for _ in range(MOBA_TOPK):
                mx = jnp.max(gt, axis=0, keepdims=True)
                first = jnp.min(jnp.where(gt == mx, blk_row, n_blocks), axis=0, keepdims=True)
                pick = blk_row == first
                sel = jnp.where(pick, 1.0, sel)
                gt = jnp.where(pick, -jnp.inf, gt)
            mt = jnp.where(sel > 0.0, mt_ref[...], NEG)
            m_own = _row_of_col_padded(mown_ref[:, :1])
            l_own = _row_of_col_padded(lown_ref[:, :1])
            m_all = jnp.maximum(jnp.max(mt, axis=0, keepdims=True), m_own)
            w = jnp.where(sel > 0.0, jnp.exp(mt - m_all), 0.0)
            w_own = jnp.exp(m_own - m_all)
            denom = jnp.sum(w * lt_ref[...], axis=0, keepdims=True) + w_own * l_own
            w = w / denom
            out = oown_ref[...] * _col_of_row(w_own / denom, N_QROWS)
            for n in range(n_blocks):
                out = out + opart_ref[n] * _col_of_row(w[n:n + 1, :], N_QROWS)
            o_ref[...] = out


def _decode(page_table, layer, q16, knew_t, vnew_t, extra, cache_kt, cache_vt, cache_lf8, fox):
    nb, n_pages = page_table.shape
    pps = PAGES_PER_STEP
    assert n_pages % pps == 0 and cache_kt.shape[2:] == (GROUP_WIDTH, LANES)
    n_blocks = n_pages * LANES // MOBA_BLOCK
    assert (n_pages * LANES) % MOBA_BLOCK == 0 and n_blocks >= MOBA_TOPK and n_blocks % SUBLANES == 0
    steps = n_pages // pps

    def page_spec(i, rows):
        if fox:
            idx = lambda b, g, pt: (layer, pt[b * n_pages + n_pages - 1 - (g * pps + i)], 0, 0)
        else:
            idx = lambda b, g, pt: (layer, pt[b * n_pages + g * pps + i], 0, 0)
        return pl.BlockSpec((None, None, rows, LANES), idx)

    seq = lambda r, w: pl.BlockSpec((None, r, w), lambda b, g, pt: (b, 0, 0))
    in_specs = [seq(N_QROWS, GROUP_WIDTH), seq(GROUP_WIDTH, LANES), seq(GROUP_WIDTH, LANES),
                seq(SUBLANES if fox else GROUP_WIDTH, LANES)]
    args = [q16, knew_t, vnew_t, extra]
    if fox:
        pos_i = jnp.arange(LANES)
        later = (pos_i[:, None] > pos_i[None, :]).astype(BF16)
        in_specs.append(pl.BlockSpec((LANES, 2 * LANES), lambda b, g, pt: (0, 0)))
        args.append(jnp.concatenate([later, jnp.ones((LANES, LANES), BF16)], axis=1))
    in_specs += [page_spec(i, GROUP_WIDTH) for i in range(pps)] * 2
    args += [cache_kt] * pps + [cache_vt] * pps
    if fox:
        in_specs += [page_spec(i, SUBLANES) for i in range(pps)]
        args += [cache_lf8] * pps
        scratch = [pltpu.VMEM((N_QROWS, LANES), F32), pltpu.VMEM((N_QROWS, LANES), F32),
                   pltpu.VMEM((N_QROWS, GROUP_WIDTH), F32), pltpu.VMEM((N_QROWS, LANES), F32),
                   pltpu.VMEM((N_QROWS, LANES), F32)]
    else:
        scratch = [pltpu.VMEM((n_blocks, N_QROWS, GROUP_WIDTH), F32)] + [pltpu.VMEM((n_blocks, LANES), F32)] * 3 + [
            pltpu.VMEM((N_QROWS, LANES), F32), pltpu.VMEM((N_QROWS, LANES), F32),
            pltpu.VMEM((N_QROWS, GROUP_WIDTH), F32)]
    return pl.pallas_call(
        functools.partial(_decode_kernel, fox=fox, n_pages_step=pps, n_blocks=n_blocks),
        grid_spec=pltpu.PrefetchScalarGridSpec(
            num_scalar_prefetch=1, grid=(nb, steps), in_specs=in_specs,
            out_specs=seq(N_QROWS, GROUP_WIDTH), scratch_shapes=scratch),
        out_shape=jax.ShapeDtypeStruct((nb, N_QROWS, GROUP_WIDTH), F32),
        compiler_params=_params(("arbitrary", "arbitrary")),
        name="fox_decode" if fox else "moba_decode",
    )(page_table.reshape(-1), *args)


def _pack_w_in(w):
    gw, nh = GROUP_WIDTH, N_HEADS
    widths = [('fox_q', gw), ('fox_k', gw), ('fox_v', gw), ('fox_f', nh), ('gdn_qkv', 3 * gw), ('gdn_a', nh),
              ('gdn_b', nh), ('gdn_g', gw), ('moba_q', gw), ('moba_k', gw), ('moba_v', gw), ('mlstm_q', gw),
              ('mlstm_k', gw), ('mlstm_v', gw), ('mlstm_i', nh), ('mlstm_f', nh), ('mlstm_o', gw)]
    wt = w.T
    cols, off = {}, 0
    for name, width in widths:
        cols[name] = wt[off:off + width]
        off += width
    assert off == wt.shape[0]
    wide = [cols[k] for k in ('fox_q', 'fox_k', 'fox_v', 'gdn_qkv', 'gdn_g', 'moba_q', 'moba_k', 'moba_v',
                              'mlstm_q', 'mlstm_k', 'mlstm_v', 'mlstm_o')]
    gates = [cols[k] for k in ('fox_f', 'gdn_a', 'gdn_b', 'mlstm_i', 'mlstm_f')]
    pad = jnp.zeros((LANES - 5 * nh, wt.shape[1]), w.dtype)
    return jnp.concatenate(wide + gates + [pad], axis=0).astype(BF16)


def _gate_row(vec, lane0):
    return jnp.zeros((1, LANES), F32).at[0, lane0:lane0 + N_HEADS].set(vec.astype(F32))


def _tile_heads(vec):
    return jnp.tile(vec.astype(F32), N_HEADS).reshape(1, GROUP_WIDTH)


def _rope_tables(pos):
    half = HEAD_DIM // 2
    inv_freq = jnp.power(ROPE_THETA, -jnp.arange(half, dtype=F32) / half)
    ang = pos.astype(F32)[:, None] * inv_freq[None, :]
    cos = jnp.cos(ang)
    sin = jnp.sin(ang)
    cos_h = jnp.concatenate([cos, cos], axis=-1)
    sin_h = jnp.concatenate([-sin, sin], axis=-1)
    return jnp.tile(cos_h, (1, N_HEADS)), jnp.tile(sin_h, (1, N_HEADS))


def _head_sum_matrix():
    lane = jnp.arange(GROUP_WIDTH)
    return (lane[:, None] // HEAD_DIM == lane[None, :] // HEAD_DIM).astype(BF16)


def _lower_tri(n):
    r = jnp.arange(n)
    return (r[:, None] >= r[None, :]).astype(BF16)


def _pad_rows(a, rows, front=0):
    return jnp.pad(a, ((0, 0), (front, rows - front - a.shape[1]), (0, 0)))


ROW_TILE = 256
ATTN_TILE = 512


def _layer_weights(w_in, w_out, w_gate, w_up, w_down):
    return (_pack_w_in(w_in), w_out.astype(BF16), w_gate.astype(BF16), w_up.astype(BF16), w_down.astype(BF16))


def _zcols(z, blk, width=GROUP_WIDTH):
    return z[:, blk * GROUP_WIDTH:blk * GROUP_WIDTH + width]


def _prompt_layer(x, lw, lp, consts):
    nb, t, d = x.shape
    wp, wo, wg, wu, wd = lw
    x2 = x.reshape(nb * t, d)
    z = _proj(x2, lp['ln1'], wp, ROW_TILE)
    fqa, fka, fva, fk32, flf, mqa, mka, mva, mk32 = _prep_prompt(
        z, nb, t, lp['attn_norms'], lp['fox_fb_row'], consts['cos_p'], consts['sin_p'], consts['gsum'], consts['tri'])
    o_a = _flash(fqa, fka, fva, nb, t, ATTN_TILE)
    o_c = _flash(mqa, mka, mva, nb, t, ATTN_TILE)
    zeros_state = jnp.zeros((nb, N_HEADS, HEAD_DIM, HEAD_DIM), F32)
    o_b, gdn_s = _gdn(z, nb, t, t, jnp.zeros((nb, SUBLANES, 3 * GROUP_WIDTH), F32), lp['conv_w8'],
                      lp['gdn_a_row'], lp['gdn_dt_row'], lp['gdn_ng'], zeros_state)
    o_d, ml_c, ml_n, ml_m = _mlstm(z, nb, t, t, lp['ml_ib_row'], lp['ml_fb_row'], lp['ml_ng'], zeros_state,
                                   jnp.zeros((nb, N_HEADS, 1, HEAD_DIM), F32), jnp.zeros((nb, N_HEADS, 1, LANES), F32))
    y = _out_ffn(x2, (o_a, o_b, o_c, o_d), wo, lp['ln2'], wg, wu, wd, ROW_TILE).reshape(nb, t, d)
    heads = lambda a: a.reshape(nb, t, N_HEADS, HEAD_DIM)
    z3 = z.reshape(nb, t, Z_WIDTH)
    qkv0 = ZB_GDN_QKV * GROUP_WIDTH
    states = (heads(fk32), heads(_zcols(z, ZB_FOX_V)), flf[:, :N_HEADS].reshape(nb, t, N_HEADS),
              heads(mk32), heads(_zcols(z, ZB_MOBA_V)),
              z3[:, t - (GDN_CONV - 1):, qkv0:qkv0 + 3 * GROUP_WIDTH], gdn_s,
              ml_c, ml_n[:, :, 0, :], ml_m[:, :, 0, 0])
    return y, states


def _sample_layer(x, layer, lw, lp, consts, caches, page_table, st):
    nb, t, d = x.shape
    assert t <= N_HEADS and t >= GDN_CONV - 1 and nb * t % SUBLANES == 0
    wp, wo, wg, wu, wd = lw
    x2 = x.reshape(nb * t, d)
    z = _proj(x2, lp['ln1'], wp, nb * t)
    fq32, fk32, flf, mq32, mk32 = _prep_sample(z, lp['attn_norms'], lp['fox_fb_row'], consts['cos_s'], consts['sin_s'],
                                               consts['gsum'])
    seq = lambda a: a.reshape(nb, t, a.shape[-1])
    tok_rows = lambda a: _pad_rows(jnp.repeat(seq(a), N_HEADS, axis=1), N_QROWS)
    new_page = lambda a: jnp.pad(jnp.swapaxes(seq(a), 1, 2), ((0, 0), (0, 0), (0, LANES - t)))
    rows_t = lambda a: jnp.pad(jnp.swapaxes(tok_rows(a), 1, 2), ((0, 0), (0, 0), (0, LANES - N_QROWS)))
    lf_new = jnp.swapaxes(seq(flf[:, :N_HEADS]), 1, 2)
    lf_new = jnp.pad(lf_new, ((0, 0), (0, 0), (0, LANES - t)))
    lf_new8 = jnp.concatenate([lf_new, lf_new], axis=1)
    fox_k, fox_v, fox_lf8, moba_k, moba_v = caches

    def own_head(o16):
        o5 = o16.reshape(nb, N_HEADS, N_HEADS, N_HEADS, HEAD_DIM)[:, :t]
        hh = jnp.arange(N_HEADS)
        return o5[:, :, hh, hh, :].reshape(nb * t, GROUP_WIDTH)

    o_a = own_head(_decode(page_table, layer, tok_rows(fq32), new_page(fk32), new_page(_zcols(z, ZB_FOX_V)),
                           lf_new8, fox_k, fox_v, fox_lf8, True))
    o_c = own_head(_decode(page_table, layer, tok_rows(mq32), new_page(mk32), new_page(_zcols(z, ZB_MOBA_V)),
                           rows_t(mq32), moba_k, moba_v, None, False))
    zp = _pad_rows(seq(z), CHUNK).reshape(nb * CHUNK, Z_WIDTH)
    conv_buf, gdn_s0, ml_c0, ml_n0, ml_m0 = st
    real_rows = lambda a: a.reshape(nb, CHUNK, GROUP_WIDTH)[:, :t].reshape(nb * t, GROUP_WIDTH)
    o_b, gdn_s = _gdn(zp, nb, CHUNK, t, _pad_rows(conv_buf, SUBLANES, front=SUBLANES - (GDN_CONV - 1)), lp['conv_w8'],
                      lp['gdn_a_row'], lp['gdn_dt_row'], lp['gdn_ng'], gdn_s0)
    o_d, ml_c, ml_n, ml_m = _mlstm(zp, nb, CHUNK, t, lp['ml_ib_row'], lp['ml_fb_row'], lp['ml_ng'], ml_c0,
                                   ml_n0[:, :, None, :],
                                   jnp.broadcast_to(ml_m0[:, :, None, None], (nb, N_HEADS, 1, LANES)))
    y = _out_ffn(x2, (o_a, real_rows(o_b), o_c, real_rows(o_d)), wo, lp['ln2'], wg, wu, wd, nb * t).reshape(nb, t, d)
    heads = lambda a: a.reshape(nb, t, N_HEADS, HEAD_DIM)
    qkv0 = ZB_GDN_QKV * GROUP_WIDTH
    conv_rows = jnp.concatenate([conv_buf, seq(z)[:, :, qkv0:qkv0 + 3 * GROUP_WIDTH]], axis=1)
    states = (heads(fk32), heads(_zcols(z, ZB_FOX_V)), seq(flf[:, :N_HEADS]),
              heads(mk32), heads(_zcols(z, ZB_MOBA_V)),
              conv_rows[:, conv_rows.shape[1] - (GDN_CONV - 1):], gdn_s,
              ml_c, ml_n[:, :, 0, :], ml_m[:, :, 0, 0])
    return y, states


def kernel(x_prompt, x_sample, cache_fox_k, cache_fox_v, cache_fox_logf, cache_moba_k, cache_moba_v,
           state_gdn_conv, state_gdn_s, state_mlstm_c, state_mlstm_n, state_mlstm_m, page_table,
           ln1, w_in, fox_q_norm, fox_k_norm, fox_f_bias, gdn_conv_w, gdn_a_log, gdn_dt_bias, gdn_norm,
           moba_q_norm, moba_k_norm, mlstm_i_bias, mlstm_f_bias, mlstm_norm, w_out, ln2,
           w_gate, w_up, w_down):
    depth = w_in.shape[0]
    t_prompt = x_prompt.shape[1]
    nb_s, t_sample = x_sample.shape[:2]
    n_pool, page = cache_fox_k.shape[1:3]
    past_len = page_table.shape[1] * page
    assert page == LANES and past_len % MOBA_BLOCK == 0
    cos_p, sin_p = _rope_tables(jnp.arange(t_prompt, dtype=jnp.int32))
    cos_s, sin_s = _rope_tables(past_len + jnp.arange(t_sample, dtype=jnp.int32))
    consts = {'cos_p': cos_p, 'sin_p': sin_p, 'cos_s': jnp.tile(cos_s, (nb_s, 1)), 'sin_s': jnp.tile(sin_s, (nb_s, 1)),
              'gsum': _head_sum_matrix(), 'tri': _lower_tri(MOBA_BLOCK)}
    pages = lambda c: jnp.transpose(c, (0, 1, 3, 4, 2)).reshape(depth, n_pool, GROUP_WIDTH, page)
    lf_t = jnp.swapaxes(cache_fox_logf, 2, 3)
    caches = (pages(cache_fox_k), pages(cache_fox_v), jnp.concatenate([lf_t, lf_t], axis=2),
              pages(cache_moba_k), pages(cache_moba_v))
    y_p, y_s = x_prompt, x_sample
    p_states, s_states = [], []
    for i in range(depth):
        lw = _layer_weights(w_in[i], w_out[i], w_gate[i], w_up[i], w_down[i])
        lp = {
            'ln1': ln1[i], 'ln2': ln2[i],
            'attn_norms': [_tile_heads(v[i]) for v in (fox_q_norm, fox_k_norm, moba_q_norm, moba_k_norm)],
            'fox_fb_row': _gate_row(fox_f_bias[i], GL_FOX_F),
            'conv_w8': jnp.pad(gdn_conv_w[i].astype(F32), ((0, SUBLANES - GDN_CONV), (0, 0))),
            'gdn_a_row': _gate_row(gdn_a_log[i], GL_GDN_A), 'gdn_dt_row': _gate_row(gdn_dt_bias[i], GL_GDN_A),
            'gdn_ng': gdn_norm[i].reshape(1, HEAD_DIM).astype(F32),
            'ml_ib_row': _gate_row(mlstm_i_bias[i], GL_ML_I), 'ml_fb_row': _gate_row(mlstm_f_bias[i], GL_ML_F),
            'ml_ng': mlstm_norm[i].reshape(1, HEAD_DIM).astype(F32),
        }
        y_p, st_p = _prompt_layer(y_p, lw, lp, consts)
        st = (state_gdn_conv[i], state_gdn_s[i], state_mlstm_c[i], state_mlstm_n[i], state_mlstm_m[i])
        y_s, st_s = _sample_layer(y_s, i, lw, lp, consts, caches, page_table, st)
        p_states.append(st_p)
        s_states.append(st_s)
    stack = lambda states: [jnp.stack(a) for a in zip(*states)]
    return (y_p, y_s, *stack(p_states), *stack(s_states))
```

```python
import functools
import math

import jax
import jax.numpy as jnp
from jax import lax
from jax.experimental import pallas as pl
from jax.experimental.pallas import tpu as pltpu

F32 = jnp.float32
BF16 = jnp.bfloat16

HEAD_DIM = 64
N_HEADS = 4
GROUP_WIDTH = N_HEADS * HEAD_DIM
LANES = 128
SUBLANES = 8
MOBA_BLOCK = 256
MOBA_TOPK = 3
MOBA_SEL_LANES = 32
GDN_CONV = 4
CHUNK = 64
NORM_EPS = 1e-6
ROPE_THETA = 10000.0
NEG = -1e30
QK_SCALE = HEAD_DIM ** -0.5
VMEM_LIMIT = 48 * 1024 * 1024

ZB_FOX_Q, ZB_FOX_K, ZB_FOX_V = 0, 1, 2
ZB_GDN_QKV = 3
ZB_GDN_G = 6
ZB_MOBA_Q, ZB_MOBA_K, ZB_MOBA_V = 7, 8, 9
ZB_ML_Q, ZB_ML_K, ZB_ML_V, ZB_ML_O = 10, 11, 12, 13
Z_WIDE = 14 * GROUP_WIDTH
Z_WIDTH = Z_WIDE + LANES
ZB_GATES = Z_WIDE // LANES
GL_FOX_F, GL_GDN_A, GL_GDN_B, GL_ML_I, GL_ML_F = 0, 4, 8, 12, 16


def _dot(a, b):
    return jnp.dot(a, b, preferred_element_type=F32)


def _dot_nt(a, b):
    return lax.dot_general(a, b, (((1,), (1,)), ((), ())), preferred_element_type=F32)


def _dot_tn(a, b):
    return lax.dot_general(a, b, (((0,), (0,)), ((), ())), preferred_element_type=F32)


def _split3(x):
    hi = x.astype(BF16)
    r1 = x - hi.astype(F32)
    mid = r1.astype(BF16)
    lo = (r1 - mid.astype(F32)).astype(BF16)
    return hi, mid, lo


def _dot3_rhs(a_bf16, x):
    hi, mid, lo = _split3(x)
    return _dot(a_bf16, hi) + _dot(a_bf16, mid) + _dot(a_bf16, lo)


def _dot3_lhs(x, b_bf16):
    hi, mid, lo = _split3(x)
    return _dot(hi, b_bf16) + _dot(mid, b_bf16) + _dot(lo, b_bf16)


def _sigmoid(x):
    return 1.0 / (1.0 + jnp.exp(-x))


def _log_sigmoid(x):
    return jnp.minimum(x, 0.0) - jnp.log1p(jnp.exp(-jnp.abs(x)))


def _softplus(x):
    return jnp.maximum(x, 0.0) + jnp.log1p(jnp.exp(-jnp.abs(x)))


def _row_of_col(col, n):
    r = lax.broadcasted_iota(jnp.int32, (n, n), 0)
    c = lax.broadcasted_iota(jnp.int32, (n, n), 1)
    return jnp.sum(jnp.where(r == c, col, 0.0), axis=0, keepdims=True)


def _params(sem):
    return pltpu.CompilerParams(dimension_semantics=sem, vmem_limit_bytes=VMEM_LIMIT)


def _const_spec(shape):
    nd = len(shape)
    return pl.BlockSpec(shape, lambda *_: (0,) * nd, pipeline_mode=pl.Buffered(1))


PROJ_COL_CHUNK = 512


def _proj_kernel(x_ref, g_ref, w_ref, z_ref):
    x = x_ref[...]
    ms = jnp.mean(x * x, axis=-1, keepdims=True)
    h = (x * lax.rsqrt(ms + NORM_EPS) * g_ref[...]).astype(BF16)
    for c0 in range(0, Z_WIDTH, PROJ_COL_CHUNK):
        c1 = min(c0 + PROJ_COL_CHUNK, Z_WIDTH)
        z_ref[:, c0:c1] = _dot_nt(h, w_ref[c0:c1, :])


def _proj(x2d, g, w_packed, tm):
    n, d = x2d.shape
    return pl.pallas_call(
        _proj_kernel,
        grid=(n // tm,),
        in_specs=[pl.BlockSpec((tm, d), lambda i: (i, 0)),
                  _const_spec((1, d)),
                  _const_spec((Z_WIDTH, d))],
        out_specs=pl.BlockSpec((tm, Z_WIDTH), lambda i: (i, 0)),
        out_shape=jax.ShapeDtypeStruct((n, Z_WIDTH), F32),
        compiler_params=_params(("arbitrary",)),
        name="norm_proj",
    )(x2d, g.reshape(1, d), w_packed)


FFN_COL_CHUNK = 256


def _out_ffn_kernel(x_ref, oa_ref, ob_ref, oc_ref, od_ref, wo_ref, g2_ref, wg_ref, wu_ref, wd_ref, y_ref):
    x1 = x_ref[...]
    for gi, o_ref in enumerate((oa_ref, ob_ref, oc_ref, od_ref)):
        x1 = x1 + _dot(o_ref[...].astype(BF16), wo_ref[gi * GROUP_WIDTH:(gi + 1) * GROUP_WIDTH, :])
    ms = jnp.mean(x1 * x1, axis=-1, keepdims=True)
    h2 = (x1 * lax.rsqrt(ms + NORM_EPS) * g2_ref[...]).astype(BF16)
    y_ref[...] = x1
    d_ff = wg_ref.shape[1]
    for c0 in range(0, d_ff, FFN_COL_CHUNK):
        c1 = c0 + FFN_COL_CHUNK
        gate = _dot(h2, wg_ref[:, c0:c1])
        up = _dot(h2, wu_ref[:, c0:c1])
        act = (gate * _sigmoid(gate) * up).astype(BF16)
        y_ref[...] += _dot(act, wd_ref[c0:c1, :])


def _out_ffn(x2d, mix, wo, g2, wg, wu, wd, tm):
    n, d = x2d.shape
    d_ff = wg.shape[1]
    assert d_ff % FFN_COL_CHUNK == 0
    row = lambda w: pl.BlockSpec((tm, w), lambda i: (i, 0))
    return pl.pallas_call(
        _out_ffn_kernel,
        grid=(n // tm,),
        in_specs=[row(d)] + [row(GROUP_WIDTH)] * 4 + [
            _const_spec((d, d)), _const_spec((1, d)), _const_spec((d, d_ff)),
            _const_spec((d, d_ff)), _const_spec((d_ff, d))],
        out_specs=row(d),
        out_shape=jax.ShapeDtypeStruct((n, d), F32),
        compiler_params=_params(("arbitrary",)),
        name="out_ffn",
    )(x2d, *mix, wo, g2.reshape(1, d), wg, wu, wd)


def _lane_iota(rows):
    return lax.broadcasted_iota(jnp.int32, (rows, LANES), 1)


def _head_rms(x, w, gsum):
    y = x * x
    hi = y.astype(BF16)
    lo = (y - hi.astype(F32)).astype(BF16)
    ss = _dot(hi, gsum) + _dot(lo, gsum)
    return x * lax.rsqrt(ss * (1.0 / HEAD_DIM) + NORM_EPS) * w


def _rotary(x, cos, sin_signed):
    half = HEAD_DIM // 2
    out = []
    for p in range(GROUP_WIDTH // LANES):
        sl = slice(p * LANES, (p + 1) * LANES)
        v = x[:, sl]
        lane = _lane_iota(v.shape[0])
        partner = jnp.where(lane % HEAD_DIM < half,
                            pltpu.roll(v, LANES - half, 1), pltpu.roll(v, half, 1))
        out.append(v * cos[:, sl] + partner * sin_signed[:, sl])
    return jnp.concatenate(out, axis=-1)


def _head_group(x, h, aug):
    p, e = divmod(h, 2)
    piece = x[:, p * LANES:(p + 1) * LANES]
    if e:
        piece = pltpu.roll(piece, HEAD_DIM, 1)
    return jnp.where(_lane_iota(x.shape[0]) < HEAD_DIM, piece, aug)


def _prep_prompt_kernel(zfq, zfk, zfv, zmq, zmk, zmv, zg, fqn, fkn, mqn, mkn, fb, cos, sin, gsum, tri,
                        fqa, fka, fva, fk32, flf, mqa, mka, mva, mk32, carry_ref, kmean_ref):
    j = pl.program_id(1)
    tm = zfq.shape[0]
    lane = _lane_iota(tm)
    g = gsum[...]

    @pl.when(j == 0)
    def _():
        carry_ref[...] = jnp.zeros_like(carry_ref)
        kmean_ref[...] = jnp.zeros_like(kmean_ref)

    fq = _head_rms(zfq[...], fqn[...], g) * QK_SCALE
    fk = _head_rms(zfk[...], fkn[...], g)
    fk32[...] = fk
    lf = _log_sigmoid(zg[...] + fb[...])
    flf[...] = lf
    c = _dot3_rhs(tri[...], lf) + carry_ref[...]
    carry_ref[...] = c[tm - 1:tm, :]
    fv = zfv[...]
    for h in range(N_HEADS):
        col = c[:, GL_FOX_F + h:GL_FOX_F + h + 1]
        hi = col.astype(BF16).astype(F32)
        r1 = col - hi
        mid = r1.astype(BF16).astype(F32)
        lo = (r1 - mid).astype(BF16).astype(F32)
        base = HEAD_DIM
        one_q = jnp.where((lane >= base + 3) & (lane < base + 6), 1.0, 0.0)
        aug_q = jnp.where(lane == base, hi, jnp.where(lane == base + 1, mid, jnp.where(lane == base + 2, lo, one_q)))
        one_k = jnp.where((lane >= base) & (lane < base + 3), 1.0, 0.0)
        aug_k = jnp.where(lane == base + 3, -hi, jnp.where(lane == base + 4, -mid, jnp.where(lane == base + 5, -lo, one_k)))
        aug_v = jnp.where(lane == base, 1.0, 0.0)
        sl = slice(h * LANES, (h + 1) * LANES)
        fqa[:, sl] = _head_group(fq, h, aug_q).astype(BF16)
        fka[:, sl] = _head_group(fk, h, aug_k).astype(BF16)
        fva[:, sl] = _head_group(fv, h, aug_v).astype(BF16)

    mq = _rotary(_head_rms(zmq[...], mqn[...], g), cos[...], sin[...])
    mk = _rotary(_head_rms(zmk[...], mkn[...], g), cos[...], sin[...])
    mk32[...] = mk
    mv = zmv[...]
    km = kmean_ref[...]
    lane_w = lax.broadcasted_iota(jnp.int32, km.shape, 1)
    kbd = jnp.concatenate([jnp.where(lane_w // HEAD_DIM == h, km, 0.0) for h in range(N_HEADS)], axis=0)
    qh = mq.astype(BF16)
    ql = (mq - qh.astype(F32)).astype(BF16)
    kh = kbd.astype(BF16)
    kl = (kbd - kh.astype(F32)).astype(BF16)
    gate = _dot_nt(qh, kh) + _dot_nt(qh, kl) + _dot_nt(ql, kh)
    blk = lane % MOBA_SEL_LANES
    gate = jnp.where(blk < j, gate, NEG)
    sel = jnp.zeros((tm, LANES), F32)
    ghs = [jnp.where(lane // MOBA_SEL_LANES == h, gate, -jnp.inf) for h in range(N_HEADS)]
    for _ in range(MOBA_TOPK):
        mxs = [jnp.max(gh, axis=-1, keepdims=True) for gh in ghs]
        firsts = [jnp.min(jnp.where(gh == mx, lane, 2 * LANES), axis=-1, keepdims=True) for gh, mx in zip(ghs, mxs)]
        for h in range(N_HEADS):
            pick = lane == firsts[h]
            sel = jnp.where(pick, 1.0, sel)
            ghs[h] = jnp.where(pick, -jnp.inf, ghs[h])
    pen = jnp.where((sel > 0.0) | (blk == j), 0.0, NEG)
    in_aug = (lane >= HEAD_DIM) & (lane < HEAD_DIM + MOBA_SEL_LANES)
    aug_k = jnp.where(in_aug & (lane - HEAD_DIM == j), 1.0, 0.0)
    aug_v = jnp.where(lane == HEAD_DIM, 1.0, 0.0)
    mqs = mq * QK_SCALE
    for h in range(N_HEADS):
        shift = (HEAD_DIM - MOBA_SEL_LANES * h) % LANES
        moved = pltpu.roll(pen, shift, 1) if shift else pen
        aug_q = jnp.where(in_aug, moved, 0.0)
        sl = slice(h * LANES, (h + 1) * LANES)
        mqa[:, sl] = _head_group(mqs, h, aug_q).astype(BF16)
        mka[:, sl] = _head_group(mk, h, aug_k).astype(BF16)
        mva[:, sl] = _head_group(mv, h, aug_v).astype(BF16)
    row = lax.broadcasted_iota(jnp.int32, km.shape, 0)
    kmean_ref[...] = jnp.where(row == j, jnp.mean(mk, axis=0, keepdims=True), km)


def _prep_prompt(z, nb, t, norms, fbias, cos, sin, gsum, tri):
    tm = MOBA_BLOCK
    nt = t // tm
    assert t % tm == 0 and nt <= MOBA_SEL_LANES
    n = nb * t
    zb = lambda blk: pl.BlockSpec((tm, GROUP_WIDTH), lambda b, j: (b * nt + j, blk))
    rowspec = lambda w: pl.BlockSpec((tm, w), lambda b, j: (b * nt + j, 0))
    wide = 4 * LANES
    outs = [(wide, BF16)] * 3 + [(GROUP_WIDTH, F32), (LANES, F32)] + [(wide, BF16)] * 3 + [(GROUP_WIDTH, F32)]
    return pl.pallas_call(
        _prep_prompt_kernel,
        grid=(nb, nt),
        in_specs=[zb(ZB_FOX_Q), zb(ZB_FOX_K), zb(ZB_FOX_V), zb(ZB_MOBA_Q), zb(ZB_MOBA_K), zb(ZB_MOBA_V),
                  pl.BlockSpec((tm, LANES), lambda b, j: (b * nt + j, ZB_GATES))]
                 + [_const_spec((1, GROUP_WIDTH))] * 4 + [_const_spec((1, LANES))]
                 + [pl.BlockSpec((tm, GROUP_WIDTH), lambda b, j: (j, 0))] * 2
                 + [_const_spec((GROUP_WIDTH, GROUP_WIDTH)), _const_spec((tm, tm))],
        out_specs=[rowspec(w) for w, _ in outs],
        out_shape=[jax.ShapeDtypeStruct((n, w), dt) for w, dt in outs],
        scratch_shapes=[pltpu.VMEM((1, LANES), F32), pltpu.VMEM((MOBA_SEL_LANES, GROUP_WIDTH), F32)],
        compiler_params=_params(("arbitrary", "arbitrary")),
        name="prep_prompt",
    )(z, z, z, z, z, z, z, *norms, fbias, cos, sin, gsum, tri)


def _prep_sample_kernel(zfq, zfk, zmq, zmk, zg, fqn, fkn, mqn, mkn, fb, cos, sin, gsum,
                        fq32, fk32, flf, mq32, mk32):
    g = gsum[...]
    fq32[...] = _head_rms(zfq[...], fqn[...], g) * QK_SCALE
    fk32[...] = _head_rms(zfk[...], fkn[...], g)
    flf[...] = _log_sigmoid(zg[...] + fb[...])
    mq32[...] = _rotary(_head_rms(zmq[...], mqn[...], g), cos[...], sin[...]) * QK_SCALE
    mk32[...] = _rotary(_head_rms(zmk[...], mkn[...], g), cos[...], sin[...])


def _prep_sample(z, norms, fbias, cos, sin, gsum):
    n = z.shape[0]
    zb = lambda blk: pl.BlockSpec((n, GROUP_WIDTH), lambda i: (0, blk))
    full = lambda w: pl.BlockSpec((n, w), lambda i: (0, 0))
    outs = [GROUP_WIDTH, GROUP_WIDTH, LANES, GROUP_WIDTH, GROUP_WIDTH]
    return pl.pallas_call(
        _prep_sample_kernel,
        grid=(1,),
        in_specs=[zb(ZB_FOX_Q), zb(ZB_FOX_K), zb(ZB_MOBA_Q), zb(ZB_MOBA_K),
                  pl.BlockSpec((n, LANES), lambda i: (0, ZB_GATES))]
                 + [full(GROUP_WIDTH)] * 0 + [_const_spec((1, GROUP_WIDTH))] * 4 + [_const_spec((1, LANES))]
                 + [full(GROUP_WIDTH)] * 2 + [_const_spec((GROUP_WIDTH, GROUP_WIDTH))],
        out_specs=[full(w) for w in outs],
        out_shape=[jax.ShapeDtypeStruct((n, w), F32) for w in outs],
        compiler_params=_params(("arbitrary",)),
        name="prep_sample",
    )(z, z, z, z, z, *norms, fbias, cos, sin, gsum)


FLASH_ROWS = 256


def _flash_kernel(qi_ref, kj_ref, q_ref, k_ref, v_ref, o_ref, m_ref, acc_ref):
    s_idx = pl.program_id(1)
    i = qi_ref[s_idx]
    j = kj_ref[s_idx]
    tq = q_ref.shape[0]

    @pl.when(j == 0)
    def _():
        m_ref[...] = jnp.full_like(m_ref, NEG)
        acc_ref[...] = jnp.zeros_like(acc_ref)

    def step(diagonal):
        tk = k_ref.shape[0]
        chains = [(h, r0) for h in range(N_HEADS) for r0 in range(0, tq, FLASH_ROWS)]
        state = [(m_ref[h, r0:r0 + FLASH_ROWS, :], acc_ref[h, r0:r0 + FLASH_ROWS, :]) for h, r0 in chains]
        nks = [min(tk, -(-(r0 + FLASH_ROWS) // LANES) * LANES) if diagonal else tk for _, r0 in chains]
        scores = [_dot_nt(q_ref[r0:r0 + FLASH_ROWS, h * LANES:(h + 1) * LANES], k_ref[0:nk, h * LANES:(h + 1) * LANES])
                  for (h, r0), nk in zip(chains, nks)]
        probs = []
        for (h, r0), nk, s, (m_prev, _) in zip(chains, nks, scores, state):
            if diagonal:
                r = lax.broadcasted_iota(jnp.int32, s.shape, 0) + r0
                c = lax.broadcasted_iota(jnp.int32, s.shape, 1)
                s = jnp.where(r >= c, s, NEG)
            m_new = jnp.maximum(m_prev, jnp.max(s, axis=-1, keepdims=True))
            p = jnp.concatenate([jnp.exp(s[:, c0:c0 + LANES] - m_new) for c0 in range(0, nk, LANES)], axis=-1)
            probs.append((m_new, p.astype(BF16)))
        results = []
        for (h, r0), nk, (m_new, p), (m_prev, acc_prev) in zip(chains, nks, probs, state):
            acc = acc_prev * jnp.exp(m_prev - m_new) + _dot(p, v_ref[0:nk, h * LANES:(h + 1) * LANES])
            results.append((m_new, acc))
        for (h, r0), (m_new, acc) in zip(chains, results):
            rows = slice(r0, r0 + FLASH_ROWS)
            acc_ref[h, rows, :] = acc
            m_ref[h, rows, :] = m_new
            if diagonal:
                o_ref[rows, h * HEAD_DIM:(h + 1) * HEAD_DIM] = acc[:, :HEAD_DIM] / acc[:, HEAD_DIM:HEAD_DIM + 1]

    @pl.when(j < i)
    def _():
        step(False)

    @pl.when(j == i)
    def _():
        step(True)


def _flash(qa, ka, va, nb, t, ta):
    nt = t // ta
    assert t % ta == 0
    pairs = [(i, j) for i in range(nt) for j in range(i + 1)]
    qi = jnp.asarray([p[0] for p in pairs], jnp.int32)
    kj = jnp.asarray([p[1] for p in pairs], jnp.int32)
    wide = 4 * LANES
    qspec = pl.BlockSpec((ta, wide), lambda b, s, qi, kj: (b * nt + qi[s], 0))
    kspec = pl.BlockSpec((ta, wide), lambda b, s, qi, kj: (b * nt + kj[s], 0))
    return pl.pallas_call(
        _flash_kernel,
        grid_spec=pltpu.PrefetchScalarGridSpec(
            num_scalar_prefetch=2,
            grid=(nb, len(pairs)),
            in_specs=[qspec, kspec, kspec],
            out_specs=pl.BlockSpec((ta, GROUP_WIDTH), lambda b, s, qi, kj: (b * nt + qi[s], 0)),
            scratch_shapes=[pltpu.VMEM((N_HEADS, ta, LANES), F32), pltpu.VMEM((N_HEADS, ta, LANES), F32)]),
        out_shape=jax.ShapeDtypeStruct((nb * t, GROUP_WIDTH), F32),
        compiler_params=_params(("arbitrary", "arbitrary")),
        name="flash_attn",
    )(qi, kj, qa, ka, va)


CHUNKS_PER_STEP = 4


def _chunk_masks():
    r = lax.broadcasted_iota(jnp.int32, (CHUNK, CHUNK), 0)
    c = lax.broadcasted_iota(jnp.int32, (CHUNK, CHUNK), 1)
    return r >= c, r > c, r == c


def _valid_rows(t_valid, t_padded, rows):
    if t_valid == t_padded:
        return None
    return pl.program_id(1) * rows + lax.broadcasted_iota(jnp.int32, (rows, 1), 0) < t_valid


def _chunks_per_step(n_chunks):
    return math.gcd(n_chunks, CHUNKS_PER_STEP)


def _rms_heads_out(x, w):
    return x * lax.rsqrt(jnp.mean(x * x, axis=-1, keepdims=True) + NORM_EPS) * w


def _gdn_kernel(x_ref, gate_ref, zg_ref, hist_ref, w_ref, arow_ref, dtrow_ref, ng_ref, s0_ref,
                o_ref, sout_ref, s_ref, xprev_ref, xp_ref, *, t_valid, t_padded, n_sub):
    n = pl.program_id(1)
    hd, gw = HEAD_DIM, GROUP_WIDTH

    @pl.when(n == 0)
    def _():
        s_ref[...] = s0_ref[...]
        xprev_ref[...] = hist_ref[...]

    rows = n_sub * CHUNK
    x = x_ref[...]
    xp_ref[0:SUBLANES, :] = xprev_ref[...]
    xp_ref[SUBLANES:, :] = x
    xprev_ref[...] = x[rows - SUBLANES:, :]
    first = SUBLANES - (GDN_CONV - 1)
    y = w_ref[0:1, :] * xp_ref[first:first + rows, :]
    for jj in range(1, GDN_CONV):
        y = y + w_ref[jj:jj + 1, :] * xp_ref[first + jj:first + jj + rows, :]
    qkv = y * _sigmoid(y)

    gates = zg_ref[...]
    g_all = -jnp.exp(arow_ref[...]) * _softplus(gates + dtrow_ref[...])
    beta_all = _sigmoid(gates)
    valid = _valid_rows(t_valid, t_padded, rows)
    if valid is not None:
        g_all = jnp.where(valid, g_all, 0.0)
    incl, strict, eye = _chunk_masks()
    tri = incl.astype(BF16)
    gate_act = gate_ref[...]
    gate_act = gate_act * _sigmoid(gate_act)

    chains = [(c, h) for c in range(n_sub) for h in range(N_HEADS)]
    gc_all = [_dot3_rhs(tri, g_all[c * CHUNK:(c + 1) * CHUNK]) for c in range(n_sub)]
    rows_of = lambda c: slice(c * CHUNK, (c + 1) * CHUNK)
    qs = [qkv[rows_of(c), h * hd:(h + 1) * hd] for c, h in chains]
    ks = [qkv[rows_of(c), gw + h * hd:gw + (h + 1) * hd] for c, h in chains]
    qq = [jnp.sum(q * q, axis=-1, keepdims=True) for q in qs]
    kq = [jnp.sum(k * k, axis=-1, keepdims=True) for k in ks]
    qs = [q * lax.rsqrt(s + NORM_EPS) * QK_SCALE for q, s in zip(qs, qq)]
    ks = [k * lax.rsqrt(s + NORM_EPS) for k, s in zip(ks, kq)]
    gcs = [gc_all[c][:, GL_GDN_A + h:GL_GDN_A + h + 1] for c, h in chains]
    gc_rows = [_row_of_col(gc, CHUNK) for gc in gcs]
    pre = []
    for (c, h), q, k, gc, gc_row in zip(chains, qs, ks, gcs, gc_rows):
        rs = rows_of(c)
        v = qkv[rs, 2 * gw + h * hd:2 * gw + (h + 1) * hd]
        beta = beta_all[rs, GL_GDN_B + h:GL_GDN_B + h + 1]
        if valid is not None:
            k = jnp.where(valid[rs], k, 0.0)
            v = jnp.where(valid[rs], v, 0.0)
            beta = jnp.where(valid[rs], beta, 0.0)
        decay = jnp.exp(jnp.where(incl, gc - gc_row, NEG))
        pre.append((q, k, v, beta, gc, decay, k * beta, jnp.exp(gc)))
    kk = [_dot_nt(kb, k) for (_, k, _, _, _, _, kb, _) in pre]
    qk = [_dot_nt(q, k) for (q, k, _, _, _, _, _, _) in pre]
    lms = [jnp.where(strict, a * p[5], 0.0) for a, p in zip(kk, pre)]
    invs = [jnp.where(eye, 1.0, 0.0) - lm for lm in lms]
    pws = lms
    for _ in range((CHUNK - 1).bit_length() - 1):
        pws = [_dot(pw, pw) for pw in pws]
        invs = [inv + _dot(inv, pw) for inv, pw in zip(invs, pws)]
    uws = [_dot(inv, jnp.concatenate([p[2] * p[3], p[6] * p[7]], axis=-1)) for inv, p in zip(invs, pre)]
    outs = {}
    for c in range(n_sub):
        idx = [c * N_HEADS + h for h in range(N_HEADS)]
        states = [s_ref[h] for h in range(N_HEADS)]
        v_new = [uws[i][:, :hd] - _dot(uws[i][:, hd:], s) for i, s in zip(idx, states)]
        o_s = [_dot(pre[i][0] * pre[i][7], s) for i, s in zip(idx, states)]
        o_v = [_dot(qk[i] * pre[i][5], vn) for i, vn in zip(idx, v_new)]
        g_last = [pre[i][4][CHUNK - 1:CHUNK, :] for i in idx]
        upd = [_dot_tn(pre[i][1] * jnp.exp(gl - pre[i][4]), vn) for i, gl, vn in zip(idx, g_last, v_new)]
        for h in range(N_HEADS):
            s_ref[h] = states[h] * jnp.exp(g_last[h]) + upd[h]
            outs[(c, h)] = o_s[h] + o_v[h]
    o_raw = [outs[ch] for ch in chains]
    ms = [jnp.mean(o * o, axis=-1, keepdims=True) for o in o_raw]
    o_fin = [o * lax.rsqrt(v + NORM_EPS) * ng_ref[...] * gate_act[rows_of(c), h * hd:(h + 1) * hd]
             for o, v, (c, h) in zip(o_raw, ms, chains)]
    o_ref[...] = jnp.concatenate(
        [jnp.concatenate(o_fin[c * N_HEADS:(c + 1) * N_HEADS], axis=-1) for c in range(n_sub)], axis=0)

    @pl.when(n == pl.num_programs(1) - 1)
    def _():
        sout_ref[...] = s_ref[...]


def _gdn(z, nb, t_padded, t_valid, hist, conv_w8, arow, dtrow, ng, s0):
    assert t_padded % CHUNK == 0
    n_sub = _chunks_per_step(t_padded // CHUNK)
    rows = n_sub * CHUNK
    nc = t_padded // rows
    qkv_w = 3 * GROUP_WIDTH
    assert (ZB_GDN_QKV * GROUP_WIDTH) % qkv_w == 0
    state = pl.BlockSpec((None, N_HEADS, HEAD_DIM, HEAD_DIM), lambda b, n: (b, 0, 0, 0))
    return pl.pallas_call(
        functools.partial(_gdn_kernel, t_valid=t_valid, t_padded=t_padded, n_sub=n_sub),
        grid=(nb, nc),
        in_specs=[pl.BlockSpec((rows, qkv_w), lambda b, n: (b * nc + n, ZB_GDN_QKV * GROUP_WIDTH // qkv_w)),
                  pl.BlockSpec((rows, GROUP_WIDTH), lambda b, n: (b * nc + n, ZB_GDN_G)),
                  pl.BlockSpec((rows, LANES), lambda b, n: (b * nc + n, ZB_GATES)),
                  pl.BlockSpec((None, SUBLANES, qkv_w), lambda b, n: (b, 0, 0)),
                  _const_spec((SUBLANES, qkv_w)), _const_spec((1, LANES)), _const_spec((1, LANES)),
                  _const_spec((1, HEAD_DIM)), state],
        out_specs=[pl.BlockSpec((rows, GROUP_WIDTH), lambda b, n: (b * nc + n, 0)), state],
        out_shape=[jax.ShapeDtypeStruct((nb * t_padded, GROUP_WIDTH), F32),
                   jax.ShapeDtypeStruct((nb, N_HEADS, HEAD_DIM, HEAD_DIM), F32)],
        scratch_shapes=[pltpu.VMEM((N_HEADS, HEAD_DIM, HEAD_DIM), F32),
                        pltpu.VMEM((SUBLANES, qkv_w), F32),
                        pltpu.VMEM((rows + SUBLANES, qkv_w), F32)],
        compiler_params=_params(("arbitrary", "arbitrary")),
        name="gdn_chunks",
    )(z, z, z, hist, conv_w8, arow, dtrow, ng, s0)


def _mlstm_kernel(q_ref, k_ref, v_ref, og_ref, zg_ref, ibrow_ref, fbrow_ref, ng_ref, c0_ref, n0_ref, m0_ref,
                  o_ref, cout_ref, nout_ref, mout_ref, c_ref, n_ref, m_ref, *, t_valid, t_padded, n_sub):
    step = pl.program_id(1)
    hd = HEAD_DIM

    @pl.when(step == 0)
    def _():
        c_ref[...] = c0_ref[...]
        n_ref[...] = n0_ref[...]
        m_ref[...] = m0_ref[...]

    rows = n_sub * CHUNK
    gates = zg_ref[...]
    ig_all = gates + ibrow_ref[...]
    lf_all = _log_sigmoid(gates + fbrow_ref[...])
    valid = _valid_rows(t_valid, t_padded, rows)
    if valid is not None:
        ig_all = jnp.where(valid, ig_all, NEG)
        lf_all = jnp.where(valid, lf_all, 0.0)
    incl, _, _ = _chunk_masks()
    tri = incl.astype(BF16)
    og = _sigmoid(og_ref[...])

    chains = [(c, h) for c in range(n_sub) for h in range(N_HEADS)]
    b_all = [_dot3_rhs(tri, lf_all[c * CHUNK:(c + 1) * CHUNK]) for c in range(n_sub)]
    bs = [b_all[c][:, GL_ML_F + h:GL_ML_F + h + 1] for c, h in chains]
    igs = [ig_all[c * CHUNK:(c + 1) * CHUNK, GL_ML_I + h:GL_ML_I + h + 1] for c, h in chains]
    dms = [jnp.where(incl, b - _row_of_col(b, CHUNK) + _row_of_col(ig, CHUNK), NEG) for b, ig in zip(bs, igs)]
    m_intras = [jnp.max(dm, axis=-1, keepdims=True) for dm in dms]
    pre = []
    for (c, h), b, ig, dm, mi in zip(chains, bs, igs, dms, m_intras):
        rs = slice(c * CHUNK, (c + 1) * CHUNK)
        sl = slice(h * hd, (h + 1) * hd)
        b_last = b[CHUNK - 1:CHUNK, :]
        pre.append((q_ref[rs, sl], k_ref[rs, sl] * QK_SCALE, v_ref[rs, sl], b, dm, mi, b_last, b_last - b + ig))
    qk = [_dot_nt(p[0], p[1]) for p in pre]
    m_start, m_next = {}, {}
    for h in range(N_HEADS):
        m = m_ref[h][:, :1]
        for c in range(n_sub):
            _, _, _, _, _, _, b_last, a_end = pre[c * N_HEADS + h]
            m_start[(c, h)] = m
            m = jnp.maximum(b_last + m, jnp.max(a_end, axis=0, keepdims=True))
            m_next[(c, h)] = m
        m_ref[h] = jnp.broadcast_to(m, (1, LANES))
    m_t = [jnp.maximum(p[3] + m_start[ch], p[5]) for ch, p in zip(chains, pre)]
    inter = [jnp.exp(p[3] + m_start[ch] - mt) for ch, p, mt in zip(chains, pre, m_t)]
    wmat = [jnp.exp(p[4] - mt) * a for p, mt, a in zip(pre, m_t, qk)]
    wv = [_dot(w, p[2]) for w, p in zip(wmat, pre)]
    wk = [jnp.exp(p[7] - m_next[ch]) * p[1] for ch, p in zip(chains, pre)]
    sc = [jnp.exp(p[6] + m_start[ch] - m_next[ch]) for ch, p in zip(chains, pre)]
    upd = [_dot_tn(a, p[2]) for a, p in zip(wk, pre)]
    c_start, n_start = {}, {}
    for h in range(N_HEADS):
        cs = c_ref[h]
        ns = n_ref[h]
        for c in range(n_sub):
            i = c * N_HEADS + h
            c_start[i], n_start[i] = cs, ns
            cs = sc[i] * cs + upd[i]
            ns = sc[i] * ns + jnp.sum(wk[i], axis=0, keepdims=True)
        c_ref[h] = cs
        n_ref[h] = ns
    qc = [_dot(p[0], c_start[i]) for i, p in enumerate(pre)]
    qn = [jnp.sum(p[0] * n_start[i], axis=-1, keepdims=True) for i, p in enumerate(pre)]
    wsum = [jnp.sum(w, axis=-1, keepdims=True) for w in wmat]
    hh = [(inter[i] * qc[i] + wv[i]) / jnp.maximum(jnp.abs(inter[i] * qn[i] + wsum[i]), jnp.exp(-m_t[i]))
          for i in range(len(chains))]
    ms = [jnp.mean(x * x, axis=-1, keepdims=True) for x in hh]
    outs = [x * lax.rsqrt(v + NORM_EPS) * ng_ref[...] * og[c * CHUNK:(c + 1) * CHUNK, h * hd:(h + 1) * hd]
            for x, v, (c, h) in zip(hh, ms, chains)]
    o_ref[...] = jnp.concatenate(
        [jnp.concatenate(outs[c * N_HEADS:(c + 1) * N_HEADS], axis=-1) for c in range(n_sub)], axis=0)

    @pl.when(step == pl.num_programs(1) - 1)
    def _():
        cout_ref[...] = c_ref[...]
        nout_ref[...] = n_ref[...]
        mout_ref[...] = m_ref[...]


def _mlstm(z, nb, t_padded, t_valid, ibrow, fbrow, ng, c0, n0, m0):
    assert t_padded % CHUNK == 0
    n_sub = _chunks_per_step(t_padded // CHUNK)
    rows = n_sub * CHUNK
    nc = t_padded // rows
    zb = lambda blk: pl.BlockSpec((rows, GROUP_WIDTH), lambda b, n: (b * nc + n, blk))
    st = lambda r, w: pl.BlockSpec((None, N_HEADS, r, w), lambda b, n: (b, 0, 0, 0))
    shp = lambda r, w: jax.ShapeDtypeStruct((nb, N_HEADS, r, w), F32)
    return pl.pallas_call(
        functools.partial(_mlstm_kernel, t_valid=t_valid, t_padded=t_padded, n_sub=n_sub),
        grid=(nb, nc),
        in_specs=[zb(ZB_ML_Q), zb(ZB_ML_K), zb(ZB_ML_V), zb(ZB_ML_O),
                  pl.BlockSpec((rows, LANES), lambda b, n: (b * nc + n, ZB_GATES)),
                  _const_spec((1, LANES)), _const_spec((1, LANES)), _const_spec((1, HEAD_DIM)),
                  st(HEAD_DIM, HEAD_DIM), st(1, HEAD_DIM), st(1, LANES)],
        out_specs=[pl.BlockSpec((rows, GROUP_WIDTH), lambda b, n: (b * nc + n, 0)),
                   st(HEAD_DIM, HEAD_DIM), st(1, HEAD_DIM), st(1, LANES)],
        out_shape=[jax.ShapeDtypeStruct((nb * t_padded, GROUP_WIDTH), F32),
                   shp(HEAD_DIM, HEAD_DIM), shp(1, HEAD_DIM), shp(1, LANES)],
        scratch_shapes=[pltpu.VMEM((N_HEADS, HEAD_DIM, HEAD_DIM), F32),
                        pltpu.VMEM((N_HEADS, 1, HEAD_DIM), F32),
                        pltpu.VMEM((N_HEADS, 1, LANES), F32)],
        compiler_params=_params(("arbitrary", "arbitrary")),
        name="mlstm_chunks",
    )(z, z, z, z, z, ibrow, fbrow, ng, c0, n0, m0)


N_QROWS = 16
PAGES_PER_STEP = 16


def _dup_rows(x8):
    return jnp.concatenate([x8, x8], axis=0)


def _col_of_row(row, n):
    r = lax.broadcasted_iota(jnp.int32, (n, row.shape[1]), 0)
    c = lax.broadcasted_iota(jnp.int32, (n, row.shape[1]), 1)
    return jnp.sum(jnp.where(r == c, row, 0.0), axis=1, keepdims=True)


def _row_of_col_padded(col):
    r = lax.broadcasted_iota(jnp.int32, (N_QROWS, LANES), 0)
    c = lax.broadcasted_iota(jnp.int32, (N_QROWS, LANES), 1)
    return jnp.sum(jnp.where(r == c, col, 0.0), axis=0, keepdims=True)


def _decode_kernel(*refs, fox, layer, n_pages, n_pages_step, n_blocks):
    pt_ref = refs[0]
    q_ref, knew_ref, vnew_ref = refs[1:4]
    pos = 4
    lfnew_ref = qt_ref = sfx_ref = lfc_ref = lfbuf = None
    if fox:
        lfnew_ref, sfx_ref = refs[pos], refs[pos + 1]
        pos += 2
    else:
        qt_ref = refs[pos]
        pos += 1
    kc_ref, vc_ref = refs[pos], refs[pos + 1]
    pos += 2
    if fox:
        lfc_ref = refs[pos]
        pos += 1
    o_ref = refs[pos]
    if fox:
        kbuf, vbuf, lfbuf, sem = refs[-4:]
        scratch = refs[pos + 1:-4]
    else:
        kbuf, vbuf, sem = refs[-3:]
        scratch = refs[pos + 1:-3]
    b = pl.program_id(0)
    g = pl.program_id(1)
    steps = pl.num_programs(1)
    last = steps - 1

    def page_copies(bb, gg, slot):
        cps = []
        for i in range(n_pages_step):
            where = gg * n_pages_step + i
            if fox:
                where = n_pages - 1 - where
            page = pt_ref[bb * n_pages + where]
            cps.append(pltpu.make_async_copy(kc_ref.at[layer, page], kbuf.at[slot, i], sem.at[0, slot]))
            cps.append(pltpu.make_async_copy(vc_ref.at[layer, page], vbuf.at[slot, i], sem.at[1, slot]))
            if fox:
                cps.append(pltpu.make_async_copy(lfc_ref.at[layer, page], lfbuf.at[slot, i], sem.at[2, slot]))
        return cps

    step = b * steps + g
    slot = lax.rem(step, 2)

    @pl.when(step == 0)
    def _():
        for cp in page_copies(0, 0, 0):
            cp.start()

    @pl.when(step + 1 < pl.num_programs(0) * steps)
    def _():
        wrap = g == last
        for cp in page_copies(jnp.where(wrap, b + 1, b), jnp.where(wrap, 0, g + 1), 1 - slot):
            cp.start()

    for cp in page_copies(b, g, slot):
        cp.wait()
    k_page = lambda i: kbuf[slot, i]
    v_page = lambda i: vbuf[slot, i]

    row = lax.broadcasted_iota(jnp.int32, (N_QROWS, GROUP_WIDTH), 0)
    lane_w = lax.broadcasted_iota(jnp.int32, (N_QROWS, GROUP_WIDTH), 1)
    qbd = jnp.where(lane_w // HEAD_DIM == row % N_HEADS, q_ref[...], 0.0)
    qbd16 = qbd.astype(BF16)
    tok = lax.broadcasted_iota(jnp.int32, (N_QROWS, LANES), 0) // N_HEADS
    key = lax.broadcasted_iota(jnp.int32, (N_QROWS, LANES), 1)
    new_ok = key <= tok

    def scores(kt_page):
        return _dot(qbd16, kt_page.astype(BF16))

    def weighted_values(p, vt_page):
        return _dot_nt(p.astype(BF16), vt_page.astype(BF16))

    if fox:
        m_ref, l_ref, acc_ref, run_ref, sq_ref = scratch

        def suffix_sums(lf_rows):
            both = _dot3_lhs(lf_rows, sfx_ref[...])
            return both[:, :LANES], both[:, LANES:]

        def online(s_list, v_list):
            mx = s_list[0]
            for s in s_list[1:]:
                mx = jnp.maximum(mx, s)
            m_prev = m_ref[...]
            m_new = jnp.maximum(m_prev, jnp.max(mx, axis=-1, keepdims=True))
            alpha = jnp.exp(m_prev - m_new)
            p_list = [jnp.exp(s - m_new) for s in s_list]
            pv = [weighted_values(p, v_page) for p, v_page in zip(p_list, v_list)]
            acc = acc_ref[...] * jnp.concatenate([alpha, alpha], axis=-1)
            for o in pv:
                acc = acc + o
            acc_ref[...] = acc
            l_ref[...] = l_ref[...] * alpha + jnp.sum(sum(p_list), axis=-1, keepdims=True)
            m_ref[...] = m_new

        @pl.when(g == 0)
        def _():
            m_ref[...] = jnp.full_like(m_ref, NEG)
            l_ref[...] = jnp.zeros_like(l_ref)
            acc_ref[...] = jnp.zeros_like(acc_ref)
            excl, tot = suffix_sums(lfnew_ref[...])
            excl16 = _dup_rows(excl)
            sq = jnp.broadcast_to(jnp.sum(jnp.where(key == tok, excl16, 0.0), axis=-1, keepdims=True),
                                  (N_QROWS, LANES))
            sq_ref[...] = sq
            run_ref[...] = _dup_rows(tot)
            s = jnp.where(new_ok, scores(knew_ref[...]) + excl16 - sq, NEG)
            online([s], [vnew_ref[...]])

        excl_all, tot_all = suffix_sums(lfbuf[slot].reshape(n_pages_step * SUBLANES, LANES))
        raw = [scores(k_page(i)) for i in range(n_pages_step)]
        run = run_ref[...]
        sq = sq_ref[...]
        s_list = []
        for i in range(n_pages_step):
            rs = slice(i * SUBLANES, (i + 1) * SUBLANES)
            s_list.append(raw[i] + _dup_rows(excl_all[rs]) + (run - sq))
            run = run + _dup_rows(tot_all[rs])
        run_ref[...] = run
        online(s_list, [v_page(i) for i in range(n_pages_step)])

        @pl.when(g == last)
        def _():
            l = l_ref[...]
            o_ref[...] = acc_ref[...] / jnp.concatenate([l, l], axis=-1)
    else:
        opart_ref, mt_ref, lt_ref, gt_ref, mown_ref, lown_ref, oown_ref = scratch
        blk_row = lax.broadcasted_iota(jnp.int32, (n_blocks, LANES), 0)

        @pl.when(g == 0)
        def _():
            mt_ref[...] = jnp.zeros_like(mt_ref)
            lt_ref[...] = jnp.zeros_like(lt_ref)
            gt_ref[...] = jnp.zeros_like(gt_ref)
            s = jnp.where(new_ok, scores(knew_ref[...]), NEG)
            m = jnp.max(s, axis=-1, keepdims=True)
            p = jnp.exp(s - m)
            mown_ref[...] = jnp.broadcast_to(m, (N_QROWS, LANES))
            lown_ref[...] = jnp.broadcast_to(jnp.sum(p, axis=-1, keepdims=True), (N_QROWS, LANES))
            oown_ref[...] = weighted_values(p, vnew_ref[...])

        crow = lax.broadcasted_iota(jnp.int32, (GROUP_WIDTH, LANES), 0)
        qlane = lax.broadcasted_iota(jnp.int32, (GROUP_WIDTH, LANES), 1)
        qbd_t = jnp.where(crow // HEAD_DIM == qlane % N_HEADS, qt_ref[...], 0.0)
        pages_per_block = MOBA_BLOCK // LANES
        blocks_step = n_pages_step // pages_per_block
        raw = [scores(k_page(i)) for i in range(n_pages_step)]
        stats = []
        for bi in range(blocks_step):
            s_pages = raw[bi * pages_per_block:(bi + 1) * pages_per_block]
            mx = s_pages[0]
            for s in s_pages[1:]:
                mx = jnp.maximum(mx, s)
            m = jnp.max(mx, axis=-1, keepdims=True)
            p_pages = [jnp.exp(s - m) for s in s_pages]
            stats.append((m, jnp.sum(sum(p_pages), axis=-1, keepdims=True), p_pages))
        pv = [[weighted_values(p, v_page(bi * pages_per_block + e)) for e, p in enumerate(st[2])]
              for bi, st in enumerate(stats)]
        mt, lt, gt = mt_ref[...], lt_ref[...], gt_ref[...]
        for bi in range(blocks_step):
            n = g * blocks_step + bi
            m, l, _ = stats[bi]
            ksum = sum(k_page(bi * pages_per_block + e) for e in range(pages_per_block))
            kmean = jnp.sum(ksum, axis=-1, keepdims=True) * (1.0 / MOBA_BLOCK)
            gate = jnp.sum(qbd_t * kmean, axis=0, keepdims=True)
            opart_ref[n] = sum(pv[bi])
            here = blk_row == n
            mt = jnp.where(here, _row_of_col_padded(m), mt)
            lt = jnp.where(here, _row_of_col_padded(l), lt)
            gt = jnp.where(here, gate, gt)
        mt_ref[...] = mt
        lt_ref[...] = lt
        gt_ref[...] = gt

        @pl.when(g == last)
        def _():
            gt = gt_ref[...]
            sel = jnp.zeros(gt.shape, F32)
            for _ in range(MOBA_TOPK):
                mx = jnp.max(gt, axis=0, keepdims=True)
                first = jnp.min(jnp.where(gt == mx, blk_row, n_blocks), axis=0, keepdims=True)
                pick = blk_row == first
                sel = jnp.where(pick, 1.0, sel)
                gt = jnp.where(pick, -jnp.inf, gt)
            mt = jnp.where(sel > 0.0, mt_ref[...], NEG)
            m_own = _row_of_col_padded(mown_ref[:, :1])
            l_own = _row_of_col_padded(lown_ref[:, :1])
            m_all = jnp.maximum(jnp.max(mt, axis=0, keepdims=True), m_own)
            w = jnp.where(sel > 0.0, jnp.exp(mt - m_all), 0.0)
            w_own = jnp.exp(m_own - m_all)
            denom = jnp.sum(w * lt_ref[...], axis=0, keepdims=True) + w_own * l_own
            w = w / denom
            out = oown_ref[...] * _col_of_row(w_own / denom, N_QROWS)
            for n in range(n_blocks):
                out = out + opart_ref[n] * _col_of_row(w[n:n + 1, :], N_QROWS)
            o_ref[...] = out


def _decode(page_table, layer, q16, knew_t, vnew_t, extra, cache_kt, cache_vt, cache_lf8, fox):
    nb, n_pages = page_table.shape
    pps = PAGES_PER_STEP
    assert n_pages % pps == 0 and cache_kt.shape[2:] == (GROUP_WIDTH, LANES)
    n_blocks = n_pages * LANES // MOBA_BLOCK
    assert (n_pages * LANES) % MOBA_BLOCK == 0 and n_blocks >= MOBA_TOPK and n_blocks % SUBLANES == 0
    steps = n_pages // pps
    hbm = pl.BlockSpec(memory_space=pl.ANY)
    seq = lambda r, w: pl.BlockSpec((None, r, w), lambda b, g, pt: (b, 0, 0))
    in_specs = [seq(N_QROWS, GROUP_WIDTH), seq(GROUP_WIDTH, LANES), seq(GROUP_WIDTH, LANES),
                seq(SUBLANES if fox else GROUP_WIDTH, LANES)]
    args = [q16, knew_t, vnew_t, extra]
    if fox:
        pos_i = jnp.arange(LANES)
        later = (pos_i[:, None] > pos_i[None, :]).astype(BF16)
        in_specs.append(pl.BlockSpec((LANES, 2 * LANES), lambda b, g, pt: (0, 0)))
        args.append(jnp.concatenate([later, jnp.ones((LANES, LANES), BF16)], axis=1))
    in_specs += [hbm, hbm]
    args += [cache_kt, cache_vt]
    ring = [pltpu.VMEM((2, pps, GROUP_WIDTH, LANES), F32), pltpu.VMEM((2, pps, GROUP_WIDTH, LANES), F32)]
    if fox:
        in_specs.append(hbm)
        args.append(cache_lf8)
        scratch = [pltpu.VMEM((N_QROWS, LANES), F32), pltpu.VMEM((N_QROWS, LANES), F32),
                   pltpu.VMEM((N_QROWS, GROUP_WIDTH), F32), pltpu.VMEM((N_QROWS, LANES), F32),
                   pltpu.VMEM((N_QROWS, LANES), F32)]
        ring.append(pltpu.VMEM((2, pps, SUBLANES, LANES), F32))
    else:
        scratch = [pltpu.VMEM((n_blocks, N_QROWS, GROUP_WIDTH), F32)] + [pltpu.VMEM((n_blocks, LANES), F32)] * 3 + [
            pltpu.VMEM((N_QROWS, LANES), F32), pltpu.VMEM((N_QROWS, LANES), F32),
            pltpu.VMEM((N_QROWS, GROUP_WIDTH), F32)]
    scratch = scratch + ring + [pltpu.SemaphoreType.DMA((len(ring), 2))]
    return pl.pallas_call(
        functools.partial(_decode_kernel, fox=fox, layer=layer, n_pages=n_pages, n_pages_step=pps, n_blocks=n_blocks),
        grid_spec=pltpu.PrefetchScalarGridSpec(
            num_scalar_prefetch=1, grid=(nb, steps), in_specs=in_specs,
            out_specs=seq(N_QROWS, GROUP_WIDTH), scratch_shapes=scratch),
        out_shape=jax.ShapeDtypeStruct((nb, N_QROWS, GROUP_WIDTH), F32),
        compiler_params=_params(("arbitrary", "arbitrary")),
        name="fox_decode" if fox else "moba_decode",
    )(page_table.reshape(-1), *args)


def _pack_w_in(w):
    gw, nh = GROUP_WIDTH, N_HEADS
    widths = [('fox_q', gw), ('fox_k', gw), ('fox_v', gw), ('fox_f', nh), ('gdn_qkv', 3 * gw), ('gdn_a', nh),
              ('gdn_b', nh), ('gdn_g', gw), ('moba_q', gw), ('moba_k', gw), ('moba_v', gw), ('mlstm_q', gw),
              ('mlstm_k', gw), ('mlstm_v', gw), ('mlstm_i', nh), ('mlstm_f', nh), ('mlstm_o', gw)]
    wt = w.T
    cols, off = {}, 0
    for name, width in widths:
        cols[name] = wt[off:off + width]
        off += width
    assert off == wt.shape[0]
    wide = [cols[k] for k in ('fox_q', 'fox_k', 'fox_v', 'gdn_qkv', 'gdn_g', 'moba_q', 'moba_k', 'moba_v',
                              'mlstm_q', 'mlstm_k', 'mlstm_v', 'mlstm_o')]
    gates = [cols[k] for k in ('fox_f', 'gdn_a', 'gdn_b', 'mlstm_i', 'mlstm_f')]
    pad = jnp.zeros((LANES - 5 * nh, wt.shape[1]), w.dtype)
    return jnp.concatenate(wide + gates + [pad], axis=0).astype(BF16)


def _gate_row(vec, lane0):
    return jnp.zeros((1, LANES), F32).at[0, lane0:lane0 + N_HEADS].set(vec.astype(F32))


def _tile_heads(vec):
    return jnp.tile(vec.astype(F32), N_HEADS).reshape(1, GROUP_WIDTH)


def _rope_tables(pos):
    half = HEAD_DIM // 2
    inv_freq = jnp.power(ROPE_THETA, -jnp.arange(half, dtype=F32) / half)
    ang = pos.astype(F32)[:, None] * inv_freq[None, :]
    cos = jnp.cos(ang)
    sin = jnp.sin(ang)
    cos_h = jnp.concatenate([cos, cos], axis=-1)
    sin_h = jnp.concatenate([-sin, sin], axis=-1)
    return jnp.tile(cos_h, (1, N_HEADS)), jnp.tile(sin_h, (1, N_HEADS))


def _head_sum_matrix():
    lane = jnp.arange(GROUP_WIDTH)
    return (lane[:, None] // HEAD_DIM == lane[None, :] // HEAD_DIM).astype(BF16)


def _lower_tri(n):
    r = jnp.arange(n)
    return (r[:, None] >= r[None, :]).astype(BF16)


def _pad_rows(a, rows, front=0):
    return jnp.pad(a, ((0, 0), (front, rows - front - a.shape[1]), (0, 0)))


ROW_TILE = 256
ATTN_TILE = 512


def _layer_weights(w_in, w_out, w_gate, w_up, w_down):
    return (_pack_w_in(w_in), w_out.astype(BF16), w_gate.astype(BF16), w_up.astype(BF16), w_down.astype(BF16))


def _zcols(z, blk, width=GROUP_WIDTH):
    return z[:, blk * GROUP_WIDTH:blk * GROUP_WIDTH + width]


def _prompt_layer(x, lw, lp, consts):
    nb, t, d = x.shape
    wp, wo, wg, wu, wd = lw
    x2 = x.reshape(nb * t, d)
    z = _proj(x2, lp['ln1'], wp, ROW_TILE)
    fqa, fka, fva, fk32, flf, mqa, mka, mva, mk32 = _prep_prompt(
        z, nb, t, lp['attn_norms'], lp['fox_fb_row'], consts['cos_p'], consts['sin_p'], consts['gsum'], consts['tri'])
    o_a = _flash(fqa, fka, fva, nb, t, ATTN_TILE)
    o_c = _flash(mqa, mka, mva, nb, t, ATTN_TILE)
    zeros_state = jnp.zeros((nb, N_HEADS, HEAD_DIM, HEAD_DIM), F32)
    o_b, gdn_s = _gdn(z, nb, t, t, jnp.zeros((nb, SUBLANES, 3 * GROUP_WIDTH), F32), lp['conv_w8'],
                      lp['gdn_a_row'], lp['gdn_dt_row'], lp['gdn_ng'], zeros_state)
    o_d, ml_c, ml_n, ml_m = _mlstm(z, nb, t, t, lp['ml_ib_row'], lp['ml_fb_row'], lp['ml_ng'], zeros_state,
                                   jnp.zeros((nb, N_HEADS, 1, HEAD_DIM), F32), jnp.zeros((nb, N_HEADS, 1, LANES), F32))
    y = _out_ffn(x2, (o_a, o_b, o_c, o_d), wo, lp['ln2'], wg, wu, wd, ROW_TILE).reshape(nb, t, d)
    heads = lambda a: a.reshape(nb, t, N_HEADS, HEAD_DIM)
    z3 = z.reshape(nb, t, Z_WIDTH)
    qkv0 = ZB_GDN_QKV * GROUP_WIDTH
    states = (heads(fk32), heads(_zcols(z, ZB_FOX_V)), flf[:, :N_HEADS].reshape(nb, t, N_HEADS),
              heads(mk32), heads(_zcols(z, ZB_MOBA_V)),
              z3[:, t - (GDN_CONV - 1):, qkv0:qkv0 + 3 * GROUP_WIDTH], gdn_s,
              ml_c, ml_n[:, :, 0, :], ml_m[:, :, 0, 0])
    return y, states


def _sample_layer(x, layer, lw, lp, consts, caches, page_table, st):
    nb, t, d = x.shape
    assert t <= N_HEADS and t >= GDN_CONV - 1 and nb * t % SUBLANES == 0
    wp, wo, wg, wu, wd = lw
    x2 = x.reshape(nb * t, d)
    z = _proj(x2, lp['ln1'], wp, nb * t)
    fq32, fk32, flf, mq32, mk32 = _prep_sample(z, lp['attn_norms'], lp['fox_fb_row'], consts['cos_s'], consts['sin_s'],
                                               consts['gsum'])
    seq = lambda a: a.reshape(nb, t, a.shape[-1])
    tok_rows = lambda a: _pad_rows(jnp.repeat(seq(a), N_HEADS, axis=1), N_QROWS)
    new_page = lambda a: jnp.pad(jnp.swapaxes(seq(a), 1, 2), ((0, 0), (0, 0), (0, LANES - t)))
    rows_t = lambda a: jnp.pad(jnp.swapaxes(tok_rows(a), 1, 2), ((0, 0), (0, 0), (0, LANES - N_QROWS)))
    lf_new = jnp.swapaxes(seq(flf[:, :N_HEADS]), 1, 2)
    lf_new = jnp.pad(lf_new, ((0, 0), (0, 0), (0, LANES - t)))
    lf_new8 = jnp.concatenate([lf_new, lf_new], axis=1)
    fox_k, fox_v, fox_lf8, moba_k, moba_v = caches

    def own_head(o16):
        o5 = o16.reshape(nb, N_HEADS, N_HEADS, N_HEADS, HEAD_DIM)[:, :t]
        hh = jnp.arange(N_HEADS)
        return o5[:, :, hh, hh, :].reshape(nb * t, GROUP_WIDTH)

    o_a = own_head(_decode(page_table, layer, tok_rows(fq32), new_page(fk32), new_page(_zcols(z, ZB_FOX_V)),
                           lf_new8, fox_k, fox_v, fox_lf8, True))
    o_c = own_head(_decode(page_table, layer, tok_rows(mq32), new_page(mk32), new_page(_zcols(z, ZB_MOBA_V)),
                           rows_t(mq32), moba_k, moba_v, None, False))
    zp = _pad_rows(seq(z), CHUNK).reshape(nb * CHUNK, Z_WIDTH)
    conv_buf, gdn_s0, ml_c0, ml_n0, ml_m0 = st
    real_rows = lambda a: a.reshape(nb, CHUNK, GROUP_WIDTH)[:, :t].reshape(nb * t, GROUP_WIDTH)
    o_b, gdn_s = _gdn(zp, nb, CHUNK, t, _pad_rows(conv_buf, SUBLANES, front=SUBLANES - (GDN_CONV - 1)), lp['conv_w8'],
                      lp['gdn_a_row'], lp['gdn_dt_row'], lp['gdn_ng'], gdn_s0)
    o_d, ml_c, ml_n, ml_m = _mlstm(zp, nb, CHUNK, t, lp['ml_ib_row'], lp['ml_fb_row'], lp['ml_ng'], ml_c0,
                                   ml_n0[:, :, None, :],
                                   jnp.broadcast_to(ml_m0[:, :, None, None], (nb, N_HEADS, 1, LANES)))
    y = _out_ffn(x2, (o_a, real_rows(o_b), o_c, real_rows(o_d)), wo, lp['ln2'], wg, wu, wd, nb * t).reshape(nb, t, d)
    heads = lambda a: a.reshape(nb, t, N_HEADS, HEAD_DIM)
    qkv0 = ZB_GDN_QKV * GROUP_WIDTH
    conv_rows = jnp.concatenate([conv_buf, seq(z)[:, :, qkv0:qkv0 + 3 * GROUP_WIDTH]], axis=1)
    states = (heads(fk32), heads(_zcols(z, ZB_FOX_V)), seq(flf[:, :N_HEADS]),
              heads(mk32), heads(_zcols(z, ZB_MOBA_V)),
              conv_rows[:, conv_rows.shape[1] - (GDN_CONV - 1):], gdn_s,
              ml_c, ml_n[:, :, 0, :], ml_m[:, :, 0, 0])
    return y, states


def kernel(x_prompt, x_sample, cache_fox_k, cache_fox_v, cache_fox_logf, cache_moba_k, cache_moba_v,
           state_gdn_conv, state_gdn_s, state_mlstm_c, state_mlstm_n, state_mlstm_m, page_table,
           ln1, w_in, fox_q_norm, fox_k_norm, fox_f_bias, gdn_conv_w, gdn_a_log, gdn_dt_bias, gdn_norm,
           moba_q_norm, moba_k_norm, mlstm_i_bias, mlstm_f_bias, mlstm_norm, w_out, ln2,
           w_gate, w_up, w_down):
    depth = w_in.shape[0]
    t_prompt = x_prompt.shape[1]
    nb_s, t_sample = x_sample.shape[:2]
    n_pool, page = cache_fox_k.shape[1:3]
    past_len = page_table.shape[1] * page
    assert page == LANES and past_len % MOBA_BLOCK == 0
    cos_p, sin_p = _rope_tables(jnp.arange(t_prompt, dtype=jnp.int32))
    cos_s, sin_s = _rope_tables(past_len + jnp.arange(t_sample, dtype=jnp.int32))
    consts = {'cos_p': cos_p, 'sin_p': sin_p, 'cos_s': jnp.tile(cos_s, (nb_s, 1)), 'sin_s': jnp.tile(sin_s, (nb_s, 1)),
              'gsum': _head_sum_matrix(), 'tri': _lower_tri(MOBA_BLOCK)}
    pages = lambda c: jnp.transpose(c, (0, 1, 3, 4, 2)).reshape(depth, n_pool, GROUP_WIDTH, page)
    lf_t = jnp.swapaxes(cache_fox_logf, 2, 3)
    caches = (pages(cache_fox_k), pages(cache_fox_v), jnp.concatenate([lf_t, lf_t], axis=2),
              pages(cache_moba_k), pages(cache_moba_v))
    y_p, y_s = x_prompt, x_sample
    p_states, s_states = [], []
    for i in range(depth):
        lw = _layer_weights(w_in[i], w_out[i], w_gate[i], w_up[i], w_down[i])
        lp = {
            'ln1': ln1[i], 'ln2': ln2[i],
            'attn_norms': [_tile_heads(v[i]) for v in (fox_q_norm, fox_k_norm, moba_q_norm, moba_k_norm)],
            'fox_fb_row': _gate_row(fox_f_bias[i], GL_FOX_F),
            'conv_w8': jnp.pad(gdn_conv_w[i].astype(F32), ((0, SUBLANES - GDN_CONV), (0, 0))),
            'gdn_a_row': _gate_row(gdn_a_log[i], GL_GDN_A), 'gdn_dt_row': _gate_row(gdn_dt_bias[i], GL_GDN_A),
            'gdn_ng': gdn_norm[i].reshape(1, HEAD_DIM).astype(F32),
            'ml_ib_row': _gate_row(mlstm_i_bias[i], GL_ML_I), 'ml_fb_row': _gate_row(mlstm_f_bias[i], GL_ML_F),
            'ml_ng': mlstm_norm[i].reshape(1, HEAD_DIM).astype(F32),
        }
        y_p, st_p = _prompt_layer(y_p, lw, lp, consts)
        st = (state_gdn_conv[i], state_gdn_s[i], state_mlstm_c[i], state_mlstm_n[i], state_mlstm_m[i])
        y_s, st_s = _sample_layer(y_s, i, lw, lp, consts, caches, page_table, st)
        p_states.append(st_p)
        s_states.append(st_s)
    stack = lambda states: [jnp.stack(a) for a in zip(*states)]
    return (y_p, y_s, *stack(p_states), *stack(s_states))
```

```python
import functools
import math

import jax
import jax.numpy as jnp
from jax import lax
from jax.experimental import pallas as pl
from jax.experimental.pallas import tpu as pltpu

F32 = jnp.float32
BF16 = jnp.bfloat16

HEAD_DIM = 64
N_HEADS = 4
GROUP_WIDTH = N_HEADS * HEAD_DIM
LANES = 128
SUBLANES = 8
MOBA_BLOCK = 256
MOBA_TOPK = 3
MOBA_SEL_LANES = 32
GDN_CONV = 4
CHUNK = 64
NORM_EPS = 1e-6
ROPE_THETA = 10000.0
NEG = -1e30
QK_SCALE = HEAD_DIM ** -0.5
VMEM_LIMIT = 48 * 1024 * 1024

ZB_FOX_Q, ZB_FOX_K, ZB_FOX_V = 0, 1, 2
ZB_GDN_QKV = 3
ZB_GDN_G = 6
ZB_MOBA_Q, ZB_MOBA_K, ZB_MOBA_V = 7, 8, 9
ZB_ML_Q, ZB_ML_K, ZB_ML_V, ZB_ML_O = 10, 11, 12, 13
Z_WIDE = 14 * GROUP_WIDTH
Z_WIDTH = Z_WIDE + LANES
ZB_GATES = Z_WIDE // LANES
GL_FOX_F, GL_GDN_A, GL_GDN_B, GL_ML_I, GL_ML_F = 0, 4, 8, 12, 16


def _dot(a, b):
    return jnp.dot(a, b, preferred_element_type=F32)


def _dot_nt(a, b):
    return lax.dot_general(a, b, (((1,), (1,)), ((), ())), preferred_element_type=F32)


def _dot_tn(a, b):
    return lax.dot_general(a, b, (((0,), (0,)), ((), ())), preferred_element_type=F32)


def _bdot(a, b):
    return _dot(a.astype(BF16), b.astype(BF16))


def _bdot_nt(a, b):
    return _dot_nt(a.astype(BF16), b.astype(BF16))


def _bdot_tn(a, b):
    return _dot_tn(a.astype(BF16), b.astype(BF16))


def _split3(x):
    hi = x.astype(BF16)
    r1 = x - hi.astype(F32)
    mid = r1.astype(BF16)
    lo = (r1 - mid.astype(F32)).astype(BF16)
    return hi, mid, lo


def _dot3_rhs(a_bf16, x):
    hi, mid, lo = _split3(x)
    return _dot(a_bf16, hi) + _dot(a_bf16, mid) + _dot(a_bf16, lo)


def _dot3_lhs(x, b_bf16):
    hi, mid, lo = _split3(x)
    return _dot(hi, b_bf16) + _dot(mid, b_bf16) + _dot(lo, b_bf16)


def _sigmoid(x):
    return 1.0 / (1.0 + jnp.exp(-x))


def _log_sigmoid(x):
    return jnp.minimum(x, 0.0) - jnp.log1p(jnp.exp(-jnp.abs(x)))


def _softplus(x):
    return jnp.maximum(x, 0.0) + jnp.log1p(jnp.exp(-jnp.abs(x)))


def _row_of_col(col, n):
    r = lax.broadcasted_iota(jnp.int32, (n, n), 0)
    c = lax.broadcasted_iota(jnp.int32, (n, n), 1)
    return jnp.sum(jnp.where(r == c, col, 0.0), axis=0, keepdims=True)


def _params(sem):
    return pltpu.CompilerParams(dimension_semantics=sem, vmem_limit_bytes=VMEM_LIMIT)


def _const_spec(shape):
    nd = len(shape)
    return pl.BlockSpec(shape, lambda *_: (0,) * nd, pipeline_mode=pl.Buffered(1))


PROJ_COL_CHUNK = 512


def _proj_kernel(x_ref, g_ref, w_ref, z_ref):
    x = x_ref[...]
    ms = jnp.mean(x * x, axis=-1, keepdims=True)
    h = (x * lax.rsqrt(ms + NORM_EPS) * g_ref[...]).astype(BF16)
    for c0 in range(0, Z_WIDTH, PROJ_COL_CHUNK):
        c1 = min(c0 + PROJ_COL_CHUNK, Z_WIDTH)
        z_ref[:, c0:c1] = _dot_nt(h, w_ref[c0:c1, :])


def _proj(x2d, g, w_packed, tm):
    n, d = x2d.shape
    return pl.pallas_call(
        _proj_kernel,
        grid=(n // tm,),
        in_specs=[pl.BlockSpec((tm, d), lambda i: (i, 0)),
                  _const_spec((1, d)),
                  _const_spec((Z_WIDTH, d))],
        out_specs=pl.BlockSpec((tm, Z_WIDTH), lambda i: (i, 0)),
        out_shape=jax.ShapeDtypeStruct((n, Z_WIDTH), F32),
        compiler_params=_params(("arbitrary",)),
        name="norm_proj",
    )(x2d, g.reshape(1, d), w_packed)


FFN_COL_CHUNK = 256


def _out_ffn_kernel(x_ref, oa_ref, ob_ref, oc_ref, od_ref, wo_ref, g2_ref, wg_ref, wu_ref, wd_ref, y_ref):
    x1 = x_ref[...]
    for gi, o_ref in enumerate((oa_ref, ob_ref, oc_ref, od_ref)):
        x1 = x1 + _dot(o_ref[...].astype(BF16), wo_ref[gi * GROUP_WIDTH:(gi + 1) * GROUP_WIDTH, :])
    ms = jnp.mean(x1 * x1, axis=-1, keepdims=True)
    h2 = (x1 * lax.rsqrt(ms + NORM_EPS) * g2_ref[...]).astype(BF16)
    y_ref[...] = x1
    d_ff = wg_ref.shape[1]
    for c0 in range(0, d_ff, FFN_COL_CHUNK):
        c1 = c0 + FFN_COL_CHUNK
        gate = _dot(h2, wg_ref[:, c0:c1])
        up = _dot(h2, wu_ref[:, c0:c1])
        act = (gate * _sigmoid(gate) * up).astype(BF16)
        y_ref[...] += _dot(act, wd_ref[c0:c1, :])


def _out_ffn(x2d, mix, wo, g2, wg, wu, wd, tm):
    n, d = x2d.shape
    d_ff = wg.shape[1]
    assert d_ff % FFN_COL_CHUNK == 0
    row = lambda w: pl.BlockSpec((tm, w), lambda i: (i, 0))
    return pl.pallas_call(
        _out_ffn_kernel,
        grid=(n // tm,),
        in_specs=[row(d)] + [row(GROUP_WIDTH)] * 4 + [
            _const_spec((d, d)), _const_spec((1, d)), _const_spec((d, d_ff)),
            _const_spec((d, d_ff)), _const_spec((d_ff, d))],
        out_specs=row(d),
        out_shape=jax.ShapeDtypeStruct((n, d), F32),
        compiler_params=_params(("arbitrary",)),
        name="out_ffn",
    )(x2d, *mix, wo, g2.reshape(1, d), wg, wu, wd)


def _lane_iota(rows):
    return lax.broadcasted_iota(jnp.int32, (rows, LANES), 1)


def _head_rms(x, w, gsum):
    y = x * x
    hi = y.astype(BF16)
    lo = (y - hi.astype(F32)).astype(BF16)
    ss = _dot(hi, gsum) + _dot(lo, gsum)
    return x * lax.rsqrt(ss * (1.0 / HEAD_DIM) + NORM_EPS) * w


def _rotary(x, cos, sin_signed):
    half = HEAD_DIM // 2
    out = []
    for p in range(GROUP_WIDTH // LANES):
        sl = slice(p * LANES, (p + 1) * LANES)
        v = x[:, sl]
        lane = _lane_iota(v.shape[0])
        partner = jnp.where(lane % HEAD_DIM < half,
                            pltpu.roll(v, LANES - half, 1), pltpu.roll(v, half, 1))
        out.append(v * cos[:, sl] + partner * sin_signed[:, sl])
    return jnp.concatenate(out, axis=-1)


def _head_group(x, h, aug):
    p, e = divmod(h, 2)
    piece = x[:, p * LANES:(p + 1) * LANES]
    if e:
        piece = pltpu.roll(piece, HEAD_DIM, 1)
    return jnp.where(_lane_iota(x.shape[0]) < HEAD_DIM, piece, aug)


def _prep_prompt_kernel(zfq, zfk, zfv, zmq, zmk, zmv, zg, fqn, fkn, mqn, mkn, fb, cos, sin, gsum, tri,
                        fqa, fka, fva, fk32, flf, mqa, mka, mva, mk32, carry_ref, kmean_ref):
    j = pl.program_id(1)
    tm = zfq.shape[0]
    lane = _lane_iota(tm)
    g = gsum[...]

    @pl.when(j == 0)
    def _():
        carry_ref[...] = jnp.zeros_like(carry_ref)
        kmean_ref[...] = jnp.zeros_like(kmean_ref)

    fq = _head_rms(zfq[...], fqn[...], g) * QK_SCALE
    fk = _head_rms(zfk[...], fkn[...], g)
    fk32[...] = fk
    lf = _log_sigmoid(zg[...] + fb[...])
    flf[...] = lf
    c = _dot3_rhs(tri[...], lf) + carry_ref[...]
    carry_ref[...] = c[tm - 1:tm, :]
    fv = zfv[...]
    for h in range(N_HEADS):
        col = c[:, GL_FOX_F + h:GL_FOX_F + h + 1]
        hi = col.astype(BF16).astype(F32)
        r1 = col - hi
        mid = r1.astype(BF16).astype(F32)
        lo = (r1 - mid).astype(BF16).astype(F32)
        base = HEAD_DIM
        one_q = jnp.where((lane >= base + 3) & (lane < base + 6), 1.0, 0.0)
        aug_q = jnp.where(lane == base, hi, jnp.where(lane == base + 1, mid, jnp.where(lane == base + 2, lo, one_q)))
        one_k = jnp.where((lane >= base) & (lane < base + 3), 1.0, 0.0)
        aug_k = jnp.where(lane == base + 3, -hi, jnp.where(lane == base + 4, -mid, jnp.where(lane == base + 5, -lo, one_k)))
        aug_v = jnp.where(lane == base, 1.0, 0.0)
        sl = slice(h * LANES, (h + 1) * LANES)
        fqa[:, sl] = _head_group(fq, h, aug_q).astype(BF16)
        fka[:, sl] = _head_group(fk, h, aug_k).astype(BF16)
        fva[:, sl] = _head_group(fv, h, aug_v).astype(BF16)

    mq = _rotary(_head_rms(zmq[...], mqn[...], g), cos[...], sin[...])
    mk = _rotary(_head_rms(zmk[...], mkn[...], g), cos[...], sin[...])
    mk32[...] = mk
    mv = zmv[...]
    km = kmean_ref[...]
    lane_w = lax.broadcasted_iota(jnp.int32, km.shape, 1)
    kbd = jnp.concatenate([jnp.where(lane_w // HEAD_DIM == h, km, 0.0) for h in range(N_HEADS)], axis=0)
    qh = mq.astype(BF16)
    ql = (mq - qh.astype(F32)).astype(BF16)
    kh = kbd.astype(BF16)
    kl = (kbd - kh.astype(F32)).astype(BF16)
    gate = _dot_nt(qh, kh) + _dot_nt(qh, kl) + _dot_nt(ql, kh)
    blk = lane % MOBA_SEL_LANES
    gate = jnp.where(blk < j, gate, NEG)
    sel = jnp.zeros((tm, LANES), F32)
    ghs = [jnp.where(lane // MOBA_SEL_LANES == h, gate, -jnp.inf) for h in range(N_HEADS)]
    for _ in range(MOBA_TOPK):
        mxs = [jnp.max(gh, axis=-1, keepdims=True) for gh in ghs]
        firsts = [jnp.min(jnp.where(gh == mx, lane, 2 * LANES), axis=-1, keepdims=True) for gh, mx in zip(ghs, mxs)]
        for h in range(N_HEADS):
            pick = lane == firsts[h]
            sel = jnp.where(pick, 1.0, sel)
            ghs[h] = jnp.where(pick, -jnp.inf, ghs[h])
    pen = jnp.where((sel > 0.0) | (blk == j), 0.0, NEG)
    in_aug = (lane >= HEAD_DIM) & (lane < HEAD_DIM + MOBA_SEL_LANES)
    aug_k = jnp.where(in_aug & (lane - HEAD_DIM == j), 1.0, 0.0)
    aug_v = jnp.where(lane == HEAD_DIM, 1.0, 0.0)
    mqs = mq * QK_SCALE
    for h in range(N_HEADS):
        shift = (HEAD_DIM - MOBA_SEL_LANES * h) % LANES
        moved = pltpu.roll(pen, shift, 1) if shift else pen
        aug_q = jnp.where(in_aug, moved, 0.0)
        sl = slice(h * LANES, (h + 1) * LANES)
        mqa[:, sl] = _head_group(mqs, h, aug_q).astype(BF16)
        mka[:, sl] = _head_group(mk, h, aug_k).astype(BF16)
        mva[:, sl] = _head_group(mv, h, aug_v).astype(BF16)
    row = lax.broadcasted_iota(jnp.int32, km.shape, 0)
    kmean_ref[...] = jnp.where(row == j, jnp.mean(mk, axis=0, keepdims=True), km)


def _prep_prompt(z, nb, t, norms, fbias, cos, sin, gsum, tri):
    tm = MOBA_BLOCK
    nt = t // tm
    assert t % tm == 0 and nt <= MOBA_SEL_LANES
    n = nb * t
    zb = lambda blk: pl.BlockSpec((tm, GROUP_WIDTH), lambda b, j: (b * nt + j, blk))
    rowspec = lambda w: pl.BlockSpec((tm, w), lambda b, j: (b * nt + j, 0))
    wide = 4 * LANES
    outs = [(wide, BF16)] * 3 + [(GROUP_WIDTH, F32), (LANES, F32)] + [(wide, BF16)] * 3 + [(GROUP_WIDTH, F32)]
    return pl.pallas_call(
        _prep_prompt_kernel,
        grid=(nb, nt),
        in_specs=[zb(ZB_FOX_Q), zb(ZB_FOX_K), zb(ZB_FOX_V), zb(ZB_MOBA_Q), zb(ZB_MOBA_K), zb(ZB_MOBA_V),
                  pl.BlockSpec((tm, LANES), lambda b, j: (b * nt + j, ZB_GATES))]
                 + [_const_spec((1, GROUP_WIDTH))] * 4 + [_const_spec((1, LANES))]
                 + [pl.BlockSpec((tm, GROUP_WIDTH), lambda b, j: (j, 0))] * 2
                 + [_const_spec((GROUP_WIDTH, GROUP_WIDTH)), _const_spec((tm, tm))],
        out_specs=[rowspec(w) for w, _ in outs],
        out_shape=[jax.ShapeDtypeStruct((n, w), dt) for w, dt in outs],
        scratch_shapes=[pltpu.VMEM((1, LANES), F32), pltpu.VMEM((MOBA_SEL_LANES, GROUP_WIDTH), F32)],
        compiler_params=_params(("arbitrary", "arbitrary")),
        name="prep_prompt",
    )(z, z, z, z, z, z, z, *norms, fbias, cos, sin, gsum, tri)


def _prep_sample_kernel(zfq, zfk, zmq, zmk, zg, fqn, fkn, mqn, mkn, fb, cos, sin, gsum,
                        fq32, fk32, flf, mq32, mk32):
    g = gsum[...]
    fq32[...] = _head_rms(zfq[...], fqn[...], g) * QK_SCALE
    fk32[...] = _head_rms(zfk[...], fkn[...], g)
    flf[...] = _log_sigmoid(zg[...] + fb[...])
    mq32[...] = _rotary(_head_rms(zmq[...], mqn[...], g), cos[...], sin[...]) * QK_SCALE
    mk32[...] = _rotary(_head_rms(zmk[...], mkn[...], g), cos[...], sin[...])


def _prep_sample(z, norms, fbias, cos, sin, gsum):
    n = z.shape[0]
    zb = lambda blk: pl.BlockSpec((n, GROUP_WIDTH), lambda i: (0, blk))
    full = lambda w: pl.BlockSpec((n, w), lambda i: (0, 0))
    outs = [GROUP_WIDTH, GROUP_WIDTH, LANES, GROUP_WIDTH, GROUP_WIDTH]
    return pl.pallas_call(
        _prep_sample_kernel,
        grid=(1,),
        in_specs=[zb(ZB_FOX_Q), zb(ZB_FOX_K), zb(ZB_MOBA_Q), zb(ZB_MOBA_K),
                  pl.BlockSpec((n, LANES), lambda i: (0, ZB_GATES))]
                 + [full(GROUP_WIDTH)] * 0 + [_const_spec((1, GROUP_WIDTH))] * 4 + [_const_spec((1, LANES))]
                 + [full(GROUP_WIDTH)] * 2 + [_const_spec((GROUP_WIDTH, GROUP_WIDTH))],
        out_specs=[full(w) for w in outs],
        out_shape=[jax.ShapeDtypeStruct((n, w), F32) for w in outs],
        compiler_params=_params(("arbitrary",)),
        name="prep_sample",
    )(z, z, z, z, z, *norms, fbias, cos, sin, gsum)


FLASH_ROWS = 256


def _flash_kernel(qi_ref, kj_ref, q_ref, k_ref, v_ref, o_ref, m_ref, acc_ref):
    s_idx = pl.program_id(1)
    i = qi_ref[s_idx]
    j = kj_ref[s_idx]
    tq = q_ref.shape[0]

    @pl.when(j == 0)
    def _():
        m_ref[...] = jnp.full_like(m_ref, NEG)
        acc_ref[...] = jnp.zeros_like(acc_ref)

    def step(diagonal):
        tk = k_ref.shape[0]
        chains = [(h, r0) for h in range(N_HEADS) for r0 in range(0, tq, FLASH_ROWS)]
        state = [(m_ref[h, r0:r0 + FLASH_ROWS, :], acc_ref[h, r0:r0 + FLASH_ROWS, :]) for h, r0 in chains]
        nks = [min(tk, -(-(r0 + FLASH_ROWS) // LANES) * LANES) if diagonal else tk for _, r0 in chains]
        scores = [_dot_nt(q_ref[r0:r0 + FLASH_ROWS, h * LANES:(h + 1) * LANES], k_ref[0:nk, h * LANES:(h + 1) * LANES])
                  for (h, r0), nk in zip(chains, nks)]
        probs = []
        for (h, r0), nk, s, (m_prev, _) in zip(chains, nks, scores, state):
            if diagonal:
                r = lax.broadcasted_iota(jnp.int32, s.shape, 0) + r0
                c = lax.broadcasted_iota(jnp.int32, s.shape, 1)
                s = jnp.where(r >= c, s, NEG)
            m_new = jnp.maximum(m_prev, jnp.max(s, axis=-1, keepdims=True))
            p = jnp.concatenate([jnp.exp(s[:, c0:c0 + LANES] - m_new) for c0 in range(0, nk, LANES)], axis=-1)
            probs.append((m_new, p.astype(BF16)))
        results = []
        for (h, r0), nk, (m_new, p), (m_prev, acc_prev) in zip(chains, nks, probs, state):
            acc = acc_prev * jnp.exp(m_prev - m_new) + _dot(p, v_ref[0:nk, h * LANES:(h + 1) * LANES])
            results.append((m_new, acc))
        for (h, r0), (m_new, acc) in zip(chains, results):
            rows = slice(r0, r0 + FLASH_ROWS)
            acc_ref[h, rows, :] = acc
            m_ref[h, rows, :] = m_new
            if diagonal:
                o_ref[rows, h * HEAD_DIM:(h + 1) * HEAD_DIM] = acc[:, :HEAD_DIM] / acc[:, HEAD_DIM:HEAD_DIM + 1]

    @pl.when(j < i)
    def _():
        step(False)

    @pl.when(j == i)
    def _():
        step(True)


def _flash(qa, ka, va, nb, t, ta):
    nt = t // ta
    assert t % ta == 0
    pairs = [(i, j) for i in range(nt) for j in range(i + 1)]
    qi = jnp.asarray([p[0] for p in pairs], jnp.int32)
    kj = jnp.asarray([p[1] for p in pairs], jnp.int32)
    wide = 4 * LANES
    qspec = pl.BlockSpec((ta, wide), lambda b, s, qi, kj: (b * nt + qi[s], 0))
    kspec = pl.BlockSpec((ta, wide), lambda b, s, qi, kj: (b * nt + kj[s], 0))
    return pl.pallas_call(
        _flash_kernel,
        grid_spec=pltpu.PrefetchScalarGridSpec(
            num_scalar_prefetch=2,
            grid=(nb, len(pairs)),
            in_specs=[qspec, kspec, kspec],
            out_specs=pl.BlockSpec((ta, GROUP_WIDTH), lambda b, s, qi, kj: (b * nt + qi[s], 0)),
            scratch_shapes=[pltpu.VMEM((N_HEADS, ta, LANES), F32), pltpu.VMEM((N_HEADS, ta, LANES), F32)]),
        out_shape=jax.ShapeDtypeStruct((nb * t, GROUP_WIDTH), F32),
        compiler_params=_params(("arbitrary", "arbitrary")),
        name="flash_attn",
    )(qi, kj, qa, ka, va)


CHUNKS_PER_STEP = 4


def _chunk_masks():
    r = lax.broadcasted_iota(jnp.int32, (CHUNK, CHUNK), 0)
    c = lax.broadcasted_iota(jnp.int32, (CHUNK, CHUNK), 1)
    return r >= c, r > c, r == c


def _valid_rows(t_valid, t_padded, rows):
    if t_valid == t_padded:
        return None
    return pl.program_id(1) * rows + lax.broadcasted_iota(jnp.int32, (rows, 1), 0) < t_valid


def _chunks_per_step(n_chunks):
    return math.gcd(n_chunks, CHUNKS_PER_STEP)


def _rms_heads_out(x, w):
    return x * lax.rsqrt(jnp.mean(x * x, axis=-1, keepdims=True) + NORM_EPS) * w


def _gdn_kernel(x_ref, gate_ref, zg_ref, hist_ref, w_ref, arow_ref, dtrow_ref, ng_ref, s0_ref,
                o_ref, sout_ref, s_ref, xprev_ref, xp_ref, *, t_valid, t_padded, n_sub):
    n = pl.program_id(1)
    hd, gw = HEAD_DIM, GROUP_WIDTH

    @pl.when(n == 0)
    def _():
        s_ref[...] = s0_ref[...]
        xprev_ref[...] = hist_ref[...]

    rows = n_sub * CHUNK
    x = x_ref[...]
    xp_ref[0:SUBLANES, :] = xprev_ref[...]
    xp_ref[SUBLANES:, :] = x
    xprev_ref[...] = x[rows - SUBLANES:, :]
    first = SUBLANES - (GDN_CONV - 1)
    y = w_ref[0:1, :] * xp_ref[first:first + rows, :]
    for jj in range(1, GDN_CONV):
        y = y + w_ref[jj:jj + 1, :] * xp_ref[first + jj:first + jj + rows, :]
    qkv = y * _sigmoid(y)

    gates = zg_ref[...]
    g_all = -jnp.exp(arow_ref[...]) * _softplus(gates + dtrow_ref[...])
    beta_all = _sigmoid(gates)
    valid = _valid_rows(t_valid, t_padded, rows)
    if valid is not None:
        g_all = jnp.where(valid, g_all, 0.0)
    incl, strict, eye = _chunk_masks()
    tri = incl.astype(BF16)
    gate_act = gate_ref[...]
    gate_act = gate_act * _sigmoid(gate_act)

    chains = [(c, h) for c in range(n_sub) for h in range(N_HEADS)]
    gc_all = [_dot3_rhs(tri, g_all[c * CHUNK:(c + 1) * CHUNK]) for c in range(n_sub)]
    rows_of = lambda c: slice(c * CHUNK, (c + 1) * CHUNK)
    qs = [qkv[rows_of(c), h * hd:(h + 1) * hd] for c, h in chains]
    ks = [qkv[rows_of(c), gw + h * hd:gw + (h + 1) * hd] for c, h in chains]
    qq = [jnp.sum(q * q, axis=-1, keepdims=True) for q in qs]
    kq = [jnp.sum(k * k, axis=-1, keepdims=True) for k in ks]
    qs = [q * lax.rsqrt(s + NORM_EPS) * QK_SCALE for q, s in zip(qs, qq)]
    ks = [k * lax.rsqrt(s + NORM_EPS) for k, s in zip(ks, kq)]
    gcs = [gc_all[c][:, GL_GDN_A + h:GL_GDN_A + h + 1] for c, h in chains]
    gc_rows = [_row_of_col(gc, CHUNK) for gc in gcs]
    pre = []
    for (c, h), q, k, gc, gc_row in zip(chains, qs, ks, gcs, gc_rows):
        rs = rows_of(c)
        v = qkv[rs, 2 * gw + h * hd:2 * gw + (h + 1) * hd]
        beta = beta_all[rs, GL_GDN_B + h:GL_GDN_B + h + 1]
        if valid is not None:
            k = jnp.where(valid[rs], k, 0.0)
            v = jnp.where(valid[rs], v, 0.0)
            beta = jnp.where(valid[rs], beta, 0.0)
        decay = jnp.exp(jnp.where(incl, gc - gc_row, NEG))
        pre.append((q, k, v, beta, gc, decay, k * beta, jnp.exp(gc)))
    kk = [_bdot_nt(kb, k) for (_, k, _, _, _, _, kb, _) in pre]
    qk = [_bdot_nt(q, k) for (q, k, _, _, _, _, _, _) in pre]
    lms = [jnp.where(strict, a * p[5], 0.0) for a, p in zip(kk, pre)]
    invs = [jnp.where(eye, 1.0, 0.0) - lm for lm in lms]
    pws = lms
    for _ in range((CHUNK - 1).bit_length() - 1):
        pws = [_bdot(pw, pw) for pw in pws]
        invs = [inv + _bdot(inv, pw) for inv, pw in zip(invs, pws)]
    uws = [_bdot(inv, jnp.concatenate([p[2] * p[3], p[6] * p[7]], axis=-1)) for inv, p in zip(invs, pre)]
    outs = {}
    for c in range(n_sub):
        idx = [c * N_HEADS + h for h in range(N_HEADS)]
        states = [s_ref[h] for h in range(N_HEADS)]
        v_new = [uws[i][:, :hd] - _bdot(uws[i][:, hd:], s) for i, s in zip(idx, states)]
        o_s = [_bdot(pre[i][0] * pre[i][7], s) for i, s in zip(idx, states)]
        o_v = [_bdot(qk[i] * pre[i][5], vn) for i, vn in zip(idx, v_new)]
        g_last = [pre[i][4][CHUNK - 1:CHUNK, :] for i in idx]
        upd = [_bdot_tn(pre[i][1] * jnp.exp(gl - pre[i][4]), vn) for i, gl, vn in zip(idx, g_last, v_new)]
        for h in range(N_HEADS):
            s_ref[h] = states[h] * jnp.exp(g_last[h]) + upd[h]
            outs[(c, h)] = o_s[h] + o_v[h]
    o_raw = [outs[ch] for ch in chains]
    ms = [jnp.mean(o * o, axis=-1, keepdims=True) for o in o_raw]
    o_fin = [o * lax.rsqrt(v + NORM_EPS) * ng_ref[...] * gate_act[rows_of(c), h * hd:(h + 1) * hd]
             for o, v, (c, h) in zip(o_raw, ms, chains)]
    o_ref[...] = jnp.concatenate(
        [jnp.concatenate(o_fin[c * N_HEADS:(c + 1) * N_HEADS], axis=-1) for c in range(n_sub)], axis=0)

    @pl.when(n == pl.num_programs(1) - 1)
    def _():
        sout_ref[...] = s_ref[...]


def _gdn(z, nb, t_padded, t_valid, hist, conv_w8, arow, dtrow, ng, s0):
    assert t_padded % CHUNK == 0
    n_sub = _chunks_per_step(t_padded // CHUNK)
    rows = n_sub * CHUNK
    nc = t_padded // rows
    qkv_w = 3 * GROUP_WIDTH
    assert (ZB_GDN_QKV * GROUP_WIDTH) % qkv_w == 0
    state = pl.BlockSpec((None, N_HEADS, HEAD_DIM, HEAD_DIM), lambda b, n: (b, 0, 0, 0))
    return pl.pallas_call(
        functools.partial(_gdn_kernel, t_valid=t_valid, t_padded=t_padded, n_sub=n_sub),
        grid=(nb, nc),
        in_specs=[pl.BlockSpec((rows, qkv_w), lambda b, n: (b * nc + n, ZB_GDN_QKV * GROUP_WIDTH // qkv_w)),
                  pl.BlockSpec((rows, GROUP_WIDTH), lambda b, n: (b * nc + n, ZB_GDN_G)),
                  pl.BlockSpec((rows, LANES), lambda b, n: (b * nc + n, ZB_GATES)),
                  pl.BlockSpec((None, SUBLANES, qkv_w), lambda b, n: (b, 0, 0)),
                  _const_spec((SUBLANES, qkv_w)), _const_spec((1, LANES)), _const_spec((1, LANES)),
                  _const_spec((1, HEAD_DIM)), state],
        out_specs=[pl.BlockSpec((rows, GROUP_WIDTH), lambda b, n: (b * nc + n, 0)), state],
        out_shape=[jax.ShapeDtypeStruct((nb * t_padded, GROUP_WIDTH), F32),
                   jax.ShapeDtypeStruct((nb, N_HEADS, HEAD_DIM, HEAD_DIM), F32)],
        scratch_shapes=[pltpu.VMEM((N_HEADS, HEAD_DIM, HEAD_DIM), F32),
                        pltpu.VMEM((SUBLANES, qkv_w), F32),
                        pltpu.VMEM((rows + SUBLANES, qkv_w), F32)],
        compiler_params=_params(("arbitrary", "arbitrary")),
        name="gdn_chunks",
    )(z, z, z, hist, conv_w8, arow, dtrow, ng, s0)


def _mlstm_kernel(q_ref, k_ref, v_ref, og_ref, zg_ref, ibrow_ref, fbrow_ref, ng_ref, c0_ref, n0_ref, m0_ref,
                  o_ref, cout_ref, nout_ref, mout_ref, c_ref, n_ref, m_ref, *, t_valid, t_padded, n_sub):
    step = pl.program_id(1)
    hd = HEAD_DIM

    @pl.when(step == 0)
    def _():
        c_ref[...] = c0_ref[...]
        n_ref[...] = n0_ref[...]
        m_ref[...] = m0_ref[...]

    rows = n_sub * CHUNK
    gates = zg_ref[...]
    ig_all = gates + ibrow_ref[...]
    lf_all = _log_sigmoid(gates + fbrow_ref[...])
    valid = _valid_rows(t_valid, t_padded, rows)
    if valid is not None:
        ig_all = jnp.where(valid, ig_all, NEG)
        lf_all = jnp.where(valid, lf_all, 0.0)
    incl, _, _ = _chunk_masks()
    tri = incl.astype(BF16)
    og = _sigmoid(og_ref[...])

    chains = [(c, h) for c in range(n_sub) for h in range(N_HEADS)]
    b_all = [_dot3_rhs(tri, lf_all[c * CHUNK:(c + 1) * CHUNK]) for c in range(n_sub)]
    bs = [b_all[c][:, GL_ML_F + h:GL_ML_F + h + 1] for c, h in chains]
    igs = [ig_all[c * CHUNK:(c + 1) * CHUNK, GL_ML_I + h:GL_ML_I + h + 1] for c, h in chains]
    dms = [jnp.where(incl, b - _row_of_col(b, CHUNK) + _row_of_col(ig, CHUNK), NEG) for b, ig in zip(bs, igs)]
    m_intras = [jnp.max(dm, axis=-1, keepdims=True) for dm in dms]
    pre = []
    for (c, h), b, ig, dm, mi in zip(chains, bs, igs, dms, m_intras):
        rs = slice(c * CHUNK, (c + 1) * CHUNK)
        sl = slice(h * hd, (h + 1) * hd)
        b_last = b[CHUNK - 1:CHUNK, :]
        pre.append((q_ref[rs, sl], k_ref[rs, sl] * QK_SCALE, v_ref[rs, sl], b, dm, mi, b_last, b_last - b + ig))
    qk = [_dot_nt(p[0], p[1]) for p in pre]
    m_start, m_next = {}, {}
    for h in range(N_HEADS):
        m = m_ref[h][:, :1]
        for c in range(n_sub):
            _, _, _, _, _, _, b_last, a_end = pre[c * N_HEADS + h]
            m_start[(c, h)] = m
            m = jnp.maximum(b_last + m, jnp.max(a_end, axis=0, keepdims=True))
            m_next[(c, h)] = m
        m_ref[h] = jnp.broadcast_to(m, (1, LANES))
    m_t = [jnp.maximum(p[3] + m_start[ch], p[5]) for ch, p in zip(chains, pre)]
    inter = [jnp.exp(p[3] + m_start[ch] - mt) for ch, p, mt in zip(chains, pre, m_t)]
    wmat = [jnp.exp(p[4] - mt) * a for p, mt, a in zip(pre, m_t, qk)]
    wv = [_dot(w, p[2]) for w, p in zip(wmat, pre)]
    wk = [jnp.exp(p[7] - m_next[ch]) * p[1] for ch, p in zip(chains, pre)]
    sc = [jnp.exp(p[6] + m_start[ch] - m_next[ch]) for ch, p in zip(chains, pre)]
    upd = [_dot_tn(a, p[2]) for a, p in zip(wk, pre)]
    c_start, n_start = {}, {}
    for h in range(N_HEADS):
        cs = c_ref[h]
        ns = n_ref[h]
        for c in range(n_sub):
            i = c * N_HEADS + h
            c_start[i], n_start[i] = cs, ns
            cs = sc[i] * cs + upd[i]
            ns = sc[i] * ns + jnp.sum(wk[i], axis=0, keepdims=True)
        c_ref[h] = cs
        n_ref[h] = ns
    qc = [_dot(p[0], c_start[i]) for i, p in enumerate(pre)]
    qn = [jnp.sum(p[0] * n_start[i], axis=-1, keepdims=True) for i, p in enumerate(pre)]
    wsum = [jnp.sum(w, axis=-1, keepdims=True) for w in wmat]
    hh = [(inter[i] * qc[i] + wv[i]) / jnp.maximum(jnp.abs(inter[i] * qn[i] + wsum[i]), jnp.exp(-m_t[i]))
          for i in range(len(chains))]
    ms = [jnp.mean(x * x, axis=-1, keepdims=True) for x in hh]
    outs = [x * lax.rsqrt(v + NORM_EPS) * ng_ref[...] * og[c * CHUNK:(c + 1) * CHUNK, h * hd:(h + 1) * hd]
            for x, v, (c, h) in zip(hh, ms, chains)]
    o_ref[...] = jnp.concatenate(
        [jnp.concatenate(outs[c * N_HEADS:(c + 1) * N_HEADS], axis=-1) for c in range(n_sub)], axis=0)

    @pl.when(step == pl.num_programs(1) - 1)
    def _():
        cout_ref[...] = c_ref[...]
        nout_ref[...] = n_ref[...]
        mout_ref[...] = m_ref[...]


def _mlstm(z, nb, t_padded, t_valid, ibrow, fbrow, ng, c0, n0, m0):
    assert t_padded % CHUNK == 0
    n_sub = _chunks_per_step(t_padded // CHUNK)
    rows = n_sub * CHUNK
    nc = t_padded // rows
    zb = lambda blk: pl.BlockSpec((rows, GROUP_WIDTH), lambda b, n: (b * nc + n, blk))
    st = lambda r, w: pl.BlockSpec((None, N_HEADS, r, w), lambda b, n: (b, 0, 0, 0))
    shp = lambda r, w: jax.ShapeDtypeStruct((nb, N_HEADS, r, w), F32)
    return pl.pallas_call(
        functools.partial(_mlstm_kernel, t_valid=t_valid, t_padded=t_padded, n_sub=n_sub),
        grid=(nb, nc),
        in_specs=[zb(ZB_ML_Q), zb(ZB_ML_K), zb(ZB_ML_V), zb(ZB_ML_O),
                  pl.BlockSpec((rows, LANES), lambda b, n: (b * nc + n, ZB_GATES)),
                  _const_spec((1, LANES)), _const_spec((1, LANES)), _const_spec((1, HEAD_DIM)),
                  st(HEAD_DIM, HEAD_DIM), st(1, HEAD_DIM), st(1, LANES)],
        out_specs=[pl.BlockSpec((rows, GROUP_WIDTH), lambda b, n: (b * nc + n, 0)),
                   st(HEAD_DIM, HEAD_DIM), st(1, HEAD_DIM), st(1, LANES)],
        out_shape=[jax.ShapeDtypeStruct((nb * t_padded, GROUP_WIDTH), F32),
                   shp(HEAD_DIM, HEAD_DIM), shp(1, HEAD_DIM), shp(1, LANES)],
        scratch_shapes=[pltpu.VMEM((N_HEADS, HEAD_DIM, HEAD_DIM), F32),
                        pltpu.VMEM((N_HEADS, 1, HEAD_DIM), F32),
                        pltpu.VMEM((N_HEADS, 1, LANES), F32)],
        compiler_params=_params(("arbitrary", "arbitrary")),
        name="mlstm_chunks",
    )(z, z, z, z, z, ibrow, fbrow, ng, c0, n0, m0)


N_QROWS = 16
PAGES_PER_STEP = 16


def _dup_rows(x8):
    return jnp.concatenate([x8, x8], axis=0)


def _col_of_row(row, n):
    r = lax.broadcasted_iota(jnp.int32, (n, row.shape[1]), 0)
    c = lax.broadcasted_iota(jnp.int32, (n, row.shape[1]), 1)
    return jnp.sum(jnp.where(r == c, row, 0.0), axis=1, keepdims=True)


def _row_of_col_padded(col):
    r = lax.broadcasted_iota(jnp.int32, (N_QROWS, LANES), 0)
    c = lax.broadcasted_iota(jnp.int32, (N_QROWS, LANES), 1)
    return jnp.sum(jnp.where(r == c, col, 0.0), axis=0, keepdims=True)


def _decode_kernel(*refs, fox, layer, n_pages, n_pages_step, n_blocks):
    pt_ref = refs[0]
    q_ref, knew_ref, vnew_ref = refs[1:4]
    pos = 4
    lfnew_ref = qt_ref = sfx_ref = lfc_ref = lfbuf = None
    if fox:
        lfnew_ref, sfx_ref = refs[pos], refs[pos + 1]
        pos += 2
    else:
        qt_ref = refs[pos]
        pos += 1
    kc_ref, vc_ref = refs[pos], refs[pos + 1]
    pos += 2
    if fox:
        lfc_ref = refs[pos]
        pos += 1
    o_ref = refs[pos]
    if fox:
        kbuf, vbuf, lfbuf, sem = refs[-4:]
        scratch = refs[pos + 1:-4]
    else:
        kbuf, vbuf, sem = refs[-3:]
        scratch = refs[pos + 1:-3]
    b = pl.program_id(0)
    g = pl.program_id(1)
    steps = pl.num_programs(1)
    last = steps - 1

    def page_copies(bb, gg, slot):
        cps = []
        for i in range(n_pages_step):
            where = gg * n_pages_step + i
            if fox:
                where = n_pages - 1 - where
            page = pt_ref[bb * n_pages + where]
            cps.append(pltpu.make_async_copy(kc_ref.at[layer, page], kbuf.at[slot, i], sem.at[0, slot]))
            cps.append(pltpu.make_async_copy(vc_ref.at[layer, page], vbuf.at[slot, i], sem.at[1, slot]))
            if fox:
                cps.append(pltpu.make_async_copy(lfc_ref.at[layer, page], lfbuf.at[slot, i], sem.at[2, slot]))
        return cps

    step = b * steps + g
    slot = lax.rem(step, 2)

    @pl.when(step == 0)
    def _():
        for cp in page_copies(0, 0, 0):
            cp.start()

    @pl.when(step + 1 < pl.num_programs(0) * steps)
    def _():
        wrap = g == last
        for cp in page_copies(jnp.where(wrap, b + 1, b), jnp.where(wrap, 0, g + 1), 1 - slot):
            cp.start()

    for cp in page_copies(b, g, slot):
        cp.wait()
    k_page = lambda i: kbuf[slot, i]
    v_page = lambda i: vbuf[slot, i]

    row = lax.broadcasted_iota(jnp.int32, (N_QROWS, GROUP_WIDTH), 0)
    lane_w = lax.broadcasted_iota(jnp.int32, (N_QROWS, GROUP_WIDTH), 1)
    qbd = jnp.where(lane_w // HEAD_DIM == row % N_HEADS, q_ref[...], 0.0)
    qbd16 = qbd.astype(BF16)
    tok = lax.broadcasted_iota(jnp.int32, (N_QROWS, LANES), 0) // N_HEADS
    key = lax.broadcasted_iota(jnp.int32, (N_QROWS, LANES), 1)
    new_ok = key <= tok

    def scores(kt_page):
        return _dot(qbd16, kt_page.astype(BF16))

    def weighted_values(p, vt_page):
        return _dot_nt(p.astype(BF16), vt_page.astype(BF16))

    if fox:
        m_ref, l_ref, acc_ref, run_ref, sq_ref = scratch

        def suffix_sums(lf_rows):
            both = _dot3_lhs(lf_rows, sfx_ref[...])
            return both[:, :LANES], both[:, LANES:]

        def online(s_list, v_list):
            mx = s_list[0]
            for s in s_list[1:]:
                mx = jnp.maximum(mx, s)
            m_prev = m_ref[...]
            m_new = jnp.maximum(m_prev, jnp.max(mx, axis=-1, keepdims=True))
            alpha = jnp.exp(m_prev - m_new)
            p_list = [jnp.exp(s - m_new) for s in s_list]
            pv = [weighted_values(p, v_page) for p, v_page in zip(p_list, v_list)]
            acc = acc_ref[...] * jnp.concatenate([alpha, alpha], axis=-1)
            for o in pv:
                acc = acc + o
            acc_ref[...] = acc
            l_ref[...] = l_ref[...] * alpha + jnp.sum(sum(p_list), axis=-1, keepdims=True)
            m_ref[...] = m_new

        @pl.when(g == 0)
        def _():
            m_ref[...] = jnp.full_like(m_ref, NEG)
            l_ref[...] = jnp.zeros_like(l_ref)
            acc_ref[...] = jnp.zeros_like(acc_ref)
            excl, tot = suffix_sums(lfnew_ref[...])
            excl16 = _dup_rows(excl)
            sq = jnp.broadcast_to(jnp.sum(jnp.where(key == tok, excl16, 0.0), axis=-1, keepdims=True),
                                  (N_QROWS, LANES))
            sq_ref[...] = sq
            run_ref[...] = _dup_rows(tot)
            s = jnp.where(new_ok, scores(knew_ref[...]) + excl16 - sq, NEG)
            online([s], [vnew_ref[...]])

        excl_all, tot_all = suffix_sums(lfbuf[slot].reshape(n_pages_step * SUBLANES, LANES))
        raw = [scores(k_page(i)) for i in range(n_pages_step)]
        run = run_ref[...]
        sq = sq_ref[...]
        s_list = []
        for i in range(n_pages_step):
            rs = slice(i * SUBLANES, (i + 1) * SUBLANES)
            s_list.append(raw[i] + _dup_rows(excl_all[rs]) + (run - sq))
            run = run + _dup_rows(tot_all[rs])
        run_ref[...] = run
        half = n_pages_step // 2
        online(s_list[:half], [v_page(i) for i in range(half)])
        online(s_list[half:], [v_page(i) for i in range(half, n_pages_step)])

        @pl.when(g == last)
        def _():
            l = l_ref[...]
            o_ref[...] = acc_ref[...] / jnp.concatenate([l, l], axis=-1)
    else:
        opart_ref, mt_ref, lt_ref, gt_ref, mown_ref, lown_ref, oown_ref = scratch
        blk_row = lax.broadcasted_iota(jnp.int32, (n_blocks, LANES), 0)

        @pl.when(g == 0)
        def _():
            mt_ref[...] = jnp.zeros_like(mt_ref)
            lt_ref[...] = jnp.zeros_like(lt_ref)
            gt_ref[...] = jnp.zeros_like(gt_ref)
            s = jnp.where(new_ok, scores(knew_ref[...]), NEG)
            m = jnp.max(s, axis=-1, keepdims=True)
            p = jnp.exp(s - m)
            mown_ref[...] = jnp.broadcast_to(m, (N_QROWS, LANES))
            lown_ref[...] = jnp.broadcast_to(jnp.sum(p, axis=-1, keepdims=True), (N_QROWS, LANES))
            oown_ref[...] = weighted_values(p, vnew_ref[...])

        crow = lax.broadcasted_iota(jnp.int32, (GROUP_WIDTH, LANES), 0)
        qlane = lax.broadcasted_iota(jnp.int32, (GROUP_WIDTH, LANES), 1)
        qbd_t = jnp.where(crow // HEAD_DIM == qlane % N_HEADS, qt_ref[...], 0.0)
        pages_per_block = MOBA_BLOCK // LANES
        blocks_step = n_pages_step // pages_per_block
        raw = [scores(k_page(i)) for i in range(n_pages_step)]
        stats = []
        for bi in range(blocks_step):
            s_pages = raw[bi * pages_per_block:(bi + 1) * pages_per_block]
            mx = s_pages[0]
            for s in s_pages[1:]:
                mx = jnp.maximum(mx, s)
            m = jnp.max(mx, axis=-1, keepdims=True)
            p_pages = [jnp.exp(s - m) for s in s_pages]
            stats.append((m, jnp.sum(sum(p_pages), axis=-1, keepdims=True), p_pages))
        pv = [[weighted_values(p, v_page(bi * pages_per_block + e)) for e, p in enumerate(st[2])]
              for bi, st in enumerate(stats)]
        mt, lt, gt = mt_ref[...], lt_ref[...], gt_ref[...]
        for bi in range(blocks_step):
            n = g * blocks_step + bi
            m, l, _ = stats[bi]
            ksum = sum(k_page(bi * pages_per_block + e) for e in range(pages_per_block))
            kmean = jnp.sum(ksum, axis=-1, keepdims=True) * (1.0 / MOBA_BLOCK)
            gate = jnp.sum(qbd_t * kmean, axis=0, keepdims=True)
            opart_ref[n] = sum(pv[bi])
            here = blk_row == n
            mt = jnp.where(here, _row_of_col_padded(m), mt)
            lt = jnp.where(here, _row_of_col_padded(l), lt)
            gt = jnp.where(here, gate, gt)
        mt_ref[...] = mt
        lt_ref[...] = lt
        gt_ref[...] = gt

        @pl.when(g == last)
        def _():
            gt = gt_ref[...]
            sel = jnp.zeros(gt.shape, F32)
            for _ in range(MOBA_TOPK):
                mx = jnp.max(gt, axis=0, keepdims=True)
                first = jnp.min(jnp.where(gt == mx, blk_row, n_blocks), axis=0, keepdims=True)
                pick = blk_row == first
                sel = jnp.where(pick, 1.0, sel)
                gt = jnp.where(pick, -jnp.inf, gt)
            mt = jnp.where(sel > 0.0, mt_ref[...], NEG)
            m_own = _row_of_col_padded(mown_ref[:, :1])
            l_own = _row_of_col_padded(lown_ref[:, :1])
            m_all = jnp.maximum(jnp.max(mt, axis=0, keepdims=True), m_own)
            w = jnp.where(sel > 0.0, jnp.exp(mt - m_all), 0.0)
            w_own = jnp.exp(m_own - m_all)
            denom = jnp.sum(w * lt_ref[...], axis=0, keepdims=True) + w_own * l_own
            w = w / denom
            out = oown_ref[...] * _col_of_row(w_own / denom, N_QROWS)
            for n in range(n_blocks):
                out = out + opart_ref[n] * _col_of_row(w[n:n + 1, :], N_QROWS)
            o_ref[...] = out


def _decode(page_table, layer, q16, knew_t, vnew_t, extra, cache_kt, cache_vt, cache_lf8, fox):
    nb, n_pages = page_table.shape
    pps = PAGES_PER_STEP
    assert n_pages % pps == 0 and cache_kt.shape[2:] == (GROUP_WIDTH, LANES)
    n_blocks = n_pages * LANES // MOBA_BLOCK
    assert (n_pages * LANES) % MOBA_BLOCK == 0 and n_blocks >= MOBA_TOPK and n_blocks % SUBLANES == 0
    steps = n_pages // pps
    hbm = pl.BlockSpec(memory_space=pl.ANY)
    seq = lambda r, w: pl.BlockSpec((None, r, w), lambda b, g, pt: (b, 0, 0))
    in_specs = [seq(N_QROWS, GROUP_WIDTH), seq(GROUP_WIDTH, LANES), seq(GROUP_WIDTH, LANES),
                seq(SUBLANES if fox else GROUP_WIDTH, LANES)]
    args = [q16, knew_t, vnew_t, extra]
    if fox:
        pos_i = jnp.arange(LANES)
        later = (pos_i[:, None] > pos_i[None, :]).astype(BF16)
        in_specs.append(pl.BlockSpec((LANES, 2 * LANES), lambda b, g, pt: (0, 0)))
        args.append(jnp.concatenate([later, jnp.ones((LANES, LANES), BF16)], axis=1))
    in_specs += [hbm, hbm]
    args += [cache_kt, cache_vt]
    ring = [pltpu.VMEM((2, pps, GROUP_WIDTH, LANES), F32), pltpu.VMEM((2, pps, GROUP_WIDTH, LANES), F32)]
    if fox:
        in_specs.append(hbm)
        args.append(cache_lf8)
        scratch = [pltpu.VMEM((N_QROWS, LANES), F32), pltpu.VMEM((N_QROWS, LANES), F32),
                   pltpu.VMEM((N_QROWS, GROUP_WIDTH), F32), pltpu.VMEM((N_QROWS, LANES), F32),
                   pltpu.VMEM((N_QROWS, LANES), F32)]
        ring.append(pltpu.VMEM((2, pps, SUBLANES, LANES), F32))
    else:
        scratch = [pltpu.VMEM((n_blocks, N_QROWS, GROUP_WIDTH), F32)] + [pltpu.VMEM((n_blocks, LANES), F32)] * 3 + [
            pltpu.VMEM((N_QROWS, LANES), F32), pltpu.VMEM((N_QROWS, LANES), F32),
            pltpu.VMEM((N_QROWS, GROUP_WIDTH), F32)]
    scratch = scratch + ring + [pltpu.SemaphoreType.DMA((len(ring), 2))]
    return pl.pallas_call(
        functools.partial(_decode_kernel, fox=fox, layer=layer, n_pages=n_pages, n_pages_step=pps, n_blocks=n_blocks),
        grid_spec=pltpu.PrefetchScalarGridSpec(
            num_scalar_prefetch=1, grid=(nb, steps), in_specs=in_specs,
            out_specs=seq(N_QROWS, GROUP_WIDTH), scratch_shapes=scratch),
        out_shape=jax.ShapeDtypeStruct((nb, N_QROWS, GROUP_WIDTH), F32),
        compiler_params=_params(("arbitrary", "arbitrary")),
        name="fox_decode" if fox else "moba_decode",
    )(page_table.reshape(-1), *args)


def _pack_w_in(w):
    gw, nh = GROUP_WIDTH, N_HEADS
    widths = [('fox_q', gw), ('fox_k', gw), ('fox_v', gw), ('fox_f', nh), ('gdn_qkv', 3 * gw), ('gdn_a', nh),
              ('gdn_b', nh), ('gdn_g', gw), ('moba_q', gw), ('moba_k', gw), ('moba_v', gw), ('mlstm_q', gw),
              ('mlstm_k', gw), ('mlstm_v', gw), ('mlstm_i', nh), ('mlstm_f', nh), ('mlstm_o', gw)]
    wt = w.T
    cols, off = {}, 0
    for name, width in widths:
        cols[name] = wt[off:off + width]
        off += width
    assert off == wt.shape[0]
    wide = [cols[k] for k in ('fox_q', 'fox_k', 'fox_v', 'gdn_qkv', 'gdn_g', 'moba_q', 'moba_k', 'moba_v',
                              'mlstm_q', 'mlstm_k', 'mlstm_v', 'mlstm_o')]
    gates = [cols[k] for k in ('fox_f', 'gdn_a', 'gdn_b', 'mlstm_i', 'mlstm_f')]
    pad = jnp.zeros((LANES - 5 * nh, wt.shape[1]), w.dtype)
    return jnp.concatenate(wide + gates + [pad], axis=0).astype(BF16)


def _gate_row(vec, lane0):
    return jnp.zeros((1, LANES), F32).at[0, lane0:lane0 + N_HEADS].set(vec.astype(F32))


def _tile_heads(vec):
    return jnp.tile(vec.astype(F32), N_HEADS).reshape(1, GROUP_WIDTH)


def _rope_tables(pos):
    half = HEAD_DIM // 2
    inv_freq = jnp.power(ROPE_THETA, -jnp.arange(half, dtype=F32) / half)
    ang = pos.astype(F32)[:, None] * inv_freq[None, :]
    cos = jnp.cos(ang)
    sin = jnp.sin(ang)
    cos_h = jnp.concatenate([cos, cos], axis=-1)
    sin_h = jnp.concatenate([-sin, sin], axis=-1)
    return jnp.tile(cos_h, (1, N_HEADS)), jnp.tile(sin_h, (1, N_HEADS))


def _head_sum_matrix():
    lane = jnp.arange(GROUP_WIDTH)
    return (lane[:, None] // HEAD_DIM == lane[None, :] // HEAD_DIM).astype(BF16)


def _lower_tri(n):
    r = jnp.arange(n)
    return (r[:, None] >= r[None, :]).astype(BF16)


def _pad_rows(a, rows, front=0):
    return jnp.pad(a, ((0, 0), (front, rows - front - a.shape[1]), (0, 0)))


ROW_TILE = 512
ATTN_TILE = 512


def _layer_weights(w_in, w_out, w_gate, w_up, w_down):
    return (_pack_w_in(w_in), w_out.astype(BF16), w_gate.astype(BF16), w_up.astype(BF16), w_down.astype(BF16))


def _zcols(z, blk, width=GROUP_WIDTH):
    return z[:, blk * GROUP_WIDTH:blk * GROUP_WIDTH + width]


def _prompt_layer(x, lw, lp, consts):
    nb, t, d = x.shape
    wp, wo, wg, wu, wd = lw
    x2 = x.reshape(nb * t, d)
    z = _proj(x2, lp['ln1'], wp, ROW_TILE)
    fqa, fka, fva, fk32, flf, mqa, mka, mva, mk32 = _prep_prompt(
        z, nb, t, lp['attn_norms'], lp['fox_fb_row'], consts['cos_p'], consts['sin_p'], consts['gsum'], consts['tri'])
    o_a = _flash(fqa, fka, fva, nb, t, ATTN_TILE)
    o_c = _flash(mqa, mka, mva, nb, t, ATTN_TILE)
    zeros_state = jnp.zeros((nb, N_HEADS, HEAD_DIM, HEAD_DIM), F32)
    o_b, gdn_s = _gdn(z, nb, t, t, jnp.zeros((nb, SUBLANES, 3 * GROUP_WIDTH), F32), lp['conv_w8'],
                      lp['gdn_a_row'], lp['gdn_dt_row'], lp['gdn_ng'], zeros_state)
    o_d, ml_c, ml_n, ml_m = _mlstm(z, nb, t, t, lp['ml_ib_row'], lp['ml_fb_row'], lp['ml_ng'], zeros_state,
                                   jnp.zeros((nb, N_HEADS, 1, HEAD_DIM), F32), jnp.zeros((nb, N_HEADS, 1, LANES), F32))
    y = _out_ffn(x2, (o_a, o_b, o_c, o_d), wo, lp['ln2'], wg, wu, wd, ROW_TILE).reshape(nb, t, d)
    heads = lambda a: a.reshape(nb, t, N_HEADS, HEAD_DIM)
    z3 = z.reshape(nb, t, Z_WIDTH)
    qkv0 = ZB_GDN_QKV * GROUP_WIDTH
    states = (heads(fk32), heads(_zcols(z, ZB_FOX_V)), flf[:, :N_HEADS].reshape(nb, t, N_HEADS),
              heads(mk32), heads(_zcols(z, ZB_MOBA_V)),
              z3[:, t - (GDN_CONV - 1):, qkv0:qkv0 + 3 * GROUP_WIDTH], gdn_s,
              ml_c, ml_n[:, :, 0, :], ml_m[:, :, 0, 0])
    return y, states


def _sample_layer(x, layer, lw, lp, consts, caches, page_table, st):
    nb, t, d = x.shape
    assert t <= N_HEADS and t >= GDN_CONV - 1 and nb * t % SUBLANES == 0
    wp, wo, wg, wu, wd = lw
    x2 = x.reshape(nb * t, d)
    z = _proj(x2, lp['ln1'], wp, nb * t)
    fq32, fk32, flf, mq32, mk32 = _prep_sample(z, lp['attn_norms'], lp['fox_fb_row'], consts['cos_s'], consts['sin_s'],
                                               consts['gsum'])
    seq = lambda a: a.reshape(nb, t, a.shape[-1])
    tok_rows = lambda a: _pad_rows(jnp.repeat(seq(a), N_HEADS, axis=1), N_QROWS)
    new_page = lambda a: jnp.pad(jnp.swapaxes(seq(a), 1, 2), ((0, 0), (0, 0), (0, LANES - t)))
    rows_t = lambda a: jnp.pad(jnp.swapaxes(tok_rows(a), 1, 2), ((0, 0), (0, 0), (0, LANES - N_QROWS)))
    lf_new = jnp.swapaxes(seq(flf[:, :N_HEADS]), 1, 2)
    lf_new = jnp.pad(lf_new, ((0, 0), (0, 0), (0, LANES - t)))
    lf_new8 = jnp.concatenate([lf_new, lf_new], axis=1)
    fox_k, fox_v, fox_lf8, moba_k, moba_v = caches

    def own_head(o16):
        o5 = o16.reshape(nb, N_HEADS, N_HEADS, N_HEADS, HEAD_DIM)[:, :t]
        hh = jnp.arange(N_HEADS)
        return o5[:, :, hh, hh, :].reshape(nb * t, GROUP_WIDTH)

    o_a = own_head(_decode(page_table, layer, tok_rows(fq32), new_page(fk32), new_page(_zcols(z, ZB_FOX_V)),
                           lf_new8, fox_k, fox_v, fox_lf8, True))
    o_c = own_head(_decode(page_table, layer, tok_rows(mq32), new_page(mk32), new_page(_zcols(z, ZB_MOBA_V)),
                           rows_t(mq32), moba_k, moba_v, None, False))
    zp = _pad_rows(seq(z), CHUNK).reshape(nb * CHUNK, Z_WIDTH)
    conv_buf, gdn_s0, ml_c0, ml_n0, ml_m0 = st
    real_rows = lambda a: a.reshape(nb, CHUNK, GROUP_WIDTH)[:, :t].reshape(nb * t, GROUP_WIDTH)
    o_b, gdn_s = _gdn(zp, nb, CHUNK, t, _pad_rows(conv_buf, SUBLANES, front=SUBLANES - (GDN_CONV - 1)), lp['conv_w8'],
                      lp['gdn_a_row'], lp['gdn_dt_row'], lp['gdn_ng'], gdn_s0)
    o_d, ml_c, ml_n, ml_m = _mlstm(zp, nb, CHUNK, t, lp['ml_ib_row'], lp['ml_fb_row'], lp['ml_ng'], ml_c0,
                                   ml_n0[:, :, None, :],
                                   jnp.broadcast_to(ml_m0[:, :, None, None], (nb, N_HEADS, 1, LANES)))
    y = _out_ffn(x2, (o_a, real_rows(o_b), o_c, real_rows(o_d)), wo, lp['ln2'], wg, wu, wd, nb * t).reshape(nb, t, d)
    heads = lambda a: a.reshape(nb, t, N_HEADS, HEAD_DIM)
    qkv0 = ZB_GDN_QKV * GROUP_WIDTH
    conv_rows = jnp.concatenate([conv_buf, seq(z)[:, :, qkv0:qkv0 + 3 * GROUP_WIDTH]], axis=1)
    states = (heads(fk32), heads(_zcols(z, ZB_FOX_V)), seq(flf[:, :N_HEADS]),
              heads(mk32), heads(_zcols(z, ZB_MOBA_V)),
              conv_rows[:, conv_rows.shape[1] - (GDN_CONV - 1):], gdn_s,
              ml_c, ml_n[:, :, 0, :], ml_m[:, :, 0, 0])
    return y, states


def kernel(x_prompt, x_sample, cache_fox_k, cache_fox_v, cache_fox_logf, cache_moba_k, cache_moba_v,
           state_gdn_conv, state_gdn_s, state_mlstm_c, state_mlstm_n, state_mlstm_m, page_table,
           ln1, w_in, fox_q_norm, fox_k_norm, fox_f_bias, gdn_conv_w, gdn_a_log, gdn_dt_bias, gdn_norm,
           moba_q_norm, moba_k_norm, mlstm_i_bias, mlstm_f_bias, mlstm_norm, w_out, ln2,
           w_gate, w_up, w_down):
    depth = w_in.shape[0]
    t_prompt = x_prompt.shape[1]
    nb_s, t_sample = x_sample.shape[:2]
    n_pool, page = cache_fox_k.shape[1:3]
    past_len = page_table.shape[1] * page
    assert page == LANES and past_len % MOBA_BLOCK == 0
    cos_p, sin_p = _rope_tables(jnp.arange(t_prompt, dtype=jnp.int32))
    cos_s, sin_s = _rope_tables(past_len + jnp.arange(t_sample, dtype=jnp.int32))
    consts = {'cos_p': cos_p, 'sin_p': sin_p, 'cos_s': jnp.tile(cos_s, (nb_s, 1)), 'sin_s': jnp.tile(sin_s, (nb_s, 1)),
              'gsum': _head_sum_matrix(), 'tri': _lower_tri(MOBA_BLOCK)}
    pages = lambda c: jnp.transpose(c, (0, 1, 3, 4, 2)).reshape(depth, n_pool, GROUP_WIDTH, page)
    lf_t = jnp.swapaxes(cache_fox_logf, 2, 3)
    caches = (pages(cache_fox_k), pages(cache_fox_v), jnp.concatenate([lf_t, lf_t], axis=2),
              pages(cache_moba_k), pages(cache_moba_v))
    y_p, y_s = x_prompt, x_sample
    p_states, s_states = [], []
    for i in range(depth):
        lw = _layer_weights(w_in[i], w_out[i], w_gate[i], w_up[i], w_down[i])
        lp = {
            'ln1': ln1[i], 'ln2': ln2[i],
            'attn_norms': [_tile_heads(v[i]) for v in (fox_q_norm, fox_k_norm, moba_q_norm, moba_k_norm)],
            'fox_fb_row': _gate_row(fox_f_bias[i], GL_FOX_F),
            'conv_w8': jnp.pad(gdn_conv_w[i].astype(F32), ((0, SUBLANES - GDN_CONV), (0, 0))),
            'gdn_a_row': _gate_row(gdn_a_log[i], GL_GDN_A), 'gdn_dt_row': _gate_row(gdn_dt_bias[i], GL_GDN_A),
            'gdn_ng': gdn_norm[i].reshape(1, HEAD_DIM).astype(F32),
            'ml_ib_row': _gate_row(mlstm_i_bias[i], GL_ML_I), 'ml_fb_row': _gate_row(mlstm_f_bias[i], GL_ML_F),
            'ml_ng': mlstm_norm[i].reshape(1, HEAD_DIM).astype(F32),
        }
        y_p, st_p = _prompt_layer(y_p, lw, lp, consts)
        st = (state_gdn_conv[i], state_gdn_s[i], state_mlstm_c[i], state_mlstm_n[i], state_mlstm_m[i])
        y_s, st_s = _sample_layer(y_s, i, lw, lp, consts, caches, page_table, st)
        p_states.append(st_p)
        s_states.append(st_s)
    stack = lambda states: [jnp.stack(a) for a in zip(*states)]
    return (y_p, y_s, *stack(p_states), *stack(s_states))
```

```python
import functools
import math

import jax
import jax.numpy as jnp
from jax import lax
from jax.experimental import pallas as pl
from jax.experimental.pallas import tpu as pltpu

F32 = jnp.float32
BF16 = jnp.bfloat16

HEAD_DIM = 64
N_HEADS = 4
GROUP_WIDTH = N_HEADS * HEAD_DIM
LANES = 128
SUBLANES = 8
MOBA_BLOCK = 256
MOBA_TOPK = 3
MOBA_SEL_LANES = 32
GDN_CONV = 4
CHUNK = 64
NORM_EPS = 1e-6
ROPE_THETA = 10000.0
NEG = -1e30
QK_SCALE = HEAD_DIM ** -0.5
VMEM_LIMIT = 48 * 1024 * 1024

ZB_FOX_Q, ZB_FOX_K, ZB_FOX_V = 0, 1, 2
ZB_GDN_QKV = 3
ZB_GDN_G = 6
ZB_MOBA_Q, ZB_MOBA_K, ZB_MOBA_V = 7, 8, 9
ZB_ML_Q, ZB_ML_K, ZB_ML_V, ZB_ML_O = 10, 11, 12, 13
Z_WIDE = 14 * GROUP_WIDTH
Z_WIDTH = Z_WIDE + LANES
ZB_GATES = Z_WIDE // LANES
GL_FOX_F, GL_GDN_A, GL_GDN_B, GL_ML_I, GL_ML_F = 0, 4, 8, 12, 16


def _dot(a, b):
    return jnp.dot(a, b, preferred_element_type=F32)


def _dot_nt(a, b):
    return lax.dot_general(a, b, (((1,), (1,)), ((), ())), preferred_element_type=F32)


def _dot_tn(a, b):
    return lax.dot_general(a, b, (((0,), (0,)), ((), ())), preferred_element_type=F32)


def _bdot(a, b):
    return _dot(a.astype(BF16), b.astype(BF16))


def _bdot_nt(a, b):
    return _dot_nt(a.astype(BF16), b.astype(BF16))


def _bdot_tn(a, b):
    return _dot_tn(a.astype(BF16), b.astype(BF16))


def _split3(x):
    hi = x.astype(BF16)
    r1 = x - hi.astype(F32)
    mid = r1.astype(BF16)
    lo = (r1 - mid.astype(F32)).astype(BF16)
    return hi, mid, lo


def _dot3_rhs(a_bf16, x):
    hi, mid, lo = _split3(x)
    return _dot(a_bf16, hi) + _dot(a_bf16, mid) + _dot(a_bf16, lo)


def _dot3_lhs(x, b_bf16):
    hi, mid, lo = _split3(x)
    return _dot(hi, b_bf16) + _dot(mid, b_bf16) + _dot(lo, b_bf16)


def _sigmoid(x):
    return 1.0 / (1.0 + jnp.exp(-x))


def _log_sigmoid(x):
    return jnp.minimum(x, 0.0) - jnp.log1p(jnp.exp(-jnp.abs(x)))


def _softplus(x):
    return jnp.maximum(x, 0.0) + jnp.log1p(jnp.exp(-jnp.abs(x)))


def _row_of_col(col, n):
    r = lax.broadcasted_iota(jnp.int32, (n, n), 0)
    c = lax.broadcasted_iota(jnp.int32, (n, n), 1)
    return jnp.sum(jnp.where(r == c, col, 0.0), axis=0, keepdims=True)


def _params(sem):
    return pltpu.CompilerParams(dimension_semantics=sem, vmem_limit_bytes=VMEM_LIMIT)


def _const_spec(shape):
    nd = len(shape)
    return pl.BlockSpec(shape, lambda *_: (0,) * nd, pipeline_mode=pl.Buffered(1))


PROJ_COL_CHUNK = 512


def _proj_kernel(x_ref, g_ref, w_ref, z_ref):
    x = x_ref[...]
    ms = jnp.mean(x * x, axis=-1, keepdims=True)
    h = (x * lax.rsqrt(ms + NORM_EPS) * g_ref[...]).astype(BF16)
    for c0 in range(0, Z_WIDTH, PROJ_COL_CHUNK):
        c1 = min(c0 + PROJ_COL_CHUNK, Z_WIDTH)
        z_ref[:, c0:c1] = _dot_nt(h, w_ref[c0:c1, :])


def _proj(x2d, g, w_packed, tm):
    n, d = x2d.shape
    return pl.pallas_call(
        _proj_kernel,
        grid=(n // tm,),
        in_specs=[pl.BlockSpec((tm, d), lambda i: (i, 0)),
                  _const_spec((1, d)),
                  _const_spec((Z_WIDTH, d))],
        out_specs=pl.BlockSpec((tm, Z_WIDTH), lambda i: (i, 0)),
        out_shape=jax.ShapeDtypeStruct((n, Z_WIDTH), F32),
        compiler_params=_params(("arbitrary",)),
        name="norm_proj",
    )(x2d, g.reshape(1, d), w_packed)


FFN_COL_CHUNK = 256


def _out_ffn_kernel(x_ref, oa_ref, ob_ref, oc_ref, od_ref, wo_ref, g2_ref, wg_ref, wu_ref, wd_ref, y_ref):
    x1 = x_ref[...]
    for gi, o_ref in enumerate((oa_ref, ob_ref, oc_ref, od_ref)):
        x1 = x1 + _dot(o_ref[...].astype(BF16), wo_ref[gi * GROUP_WIDTH:(gi + 1) * GROUP_WIDTH, :])
    ms = jnp.mean(x1 * x1, axis=-1, keepdims=True)
    h2 = (x1 * lax.rsqrt(ms + NORM_EPS) * g2_ref[...]).astype(BF16)
    y_ref[...] = x1
    d_ff = wg_ref.shape[1]
    for c0 in range(0, d_ff, FFN_COL_CHUNK):
        c1 = c0 + FFN_COL_CHUNK
        gate = _dot(h2, wg_ref[:, c0:c1])
        up = _dot(h2, wu_ref[:, c0:c1])
        act = (gate * _sigmoid(gate) * up).astype(BF16)
        y_ref[...] += _dot(act, wd_ref[c0:c1, :])


def _out_ffn(x2d, mix, wo, g2, wg, wu, wd, tm):
    n, d = x2d.shape
    d_ff = wg.shape[1]
    assert d_ff % FFN_COL_CHUNK == 0
    row = lambda w: pl.BlockSpec((tm, w), lambda i: (i, 0))
    return pl.pallas_call(
        _out_ffn_kernel,
        grid=(n // tm,),
        in_specs=[row(d)] + [row(GROUP_WIDTH)] * 4 + [
            _const_spec((d, d)), _const_spec((1, d)), _const_spec((d, d_ff)),
            _const_spec((d, d_ff)), _const_spec((d_ff, d))],
        out_specs=row(d),
        out_shape=jax.ShapeDtypeStruct((n, d), F32),
        compiler_params=_params(("arbitrary",)),
        name="out_ffn",
    )(x2d, *mix, wo, g2.reshape(1, d), wg, wu, wd)


def _lane_iota(rows):
    return lax.broadcasted_iota(jnp.int32, (rows, LANES), 1)


def _head_rms(x, w, gsum):
    y = x * x
    hi = y.astype(BF16)
    lo = (y - hi.astype(F32)).astype(BF16)
    ss = _dot(hi, gsum) + _dot(lo, gsum)
    return x * lax.rsqrt(ss * (1.0 / HEAD_DIM) + NORM_EPS) * w


def _rotary(x, cos, sin_signed):
    half = HEAD_DIM // 2
    out = []
    for p in range(GROUP_WIDTH // LANES):
        sl = slice(p * LANES, (p + 1) * LANES)
        v = x[:, sl]
        lane = _lane_iota(v.shape[0])
        partner = jnp.where(lane % HEAD_DIM < half,
                            pltpu.roll(v, LANES - half, 1), pltpu.roll(v, half, 1))
        out.append(v * cos[:, sl] + partner * sin_signed[:, sl])
    return jnp.concatenate(out, axis=-1)


def _head_group(x, h, aug):
    p, e = divmod(h, 2)
    piece = x[:, p * LANES:(p + 1) * LANES]
    if e:
        piece = pltpu.roll(piece, HEAD_DIM, 1)
    return jnp.where(_lane_iota(x.shape[0]) < HEAD_DIM, piece, aug)


def _prep_prompt_kernel(zfq, zfk, zfv, zmq, zmk, zmv, zg, fqn, fkn, mqn, mkn, fb, cos, sin, gsum, tri,
                        fqa, fka, fva, fk32, fv32, flf, mqa, mka, mva, mk32, mv32, carry_ref, kmean_ref):
    j = pl.program_id(1)
    tm = zfq.shape[0]
    lane = _lane_iota(tm)
    g = gsum[...]

    @pl.when(j == 0)
    def _():
        carry_ref[...] = jnp.zeros_like(carry_ref)
        kmean_ref[...] = jnp.zeros_like(kmean_ref)

    fq = _head_rms(zfq[...], fqn[...], g) * QK_SCALE
    fk = _head_rms(zfk[...], fkn[...], g)
    fv = zfv[...]
    fk32[...] = fk.T
    fv32[...] = fv.T
    lf = _log_sigmoid(zg[...] + fb[...])
    flf[...] = lf.T[:SUBLANES]
    c = _dot3_rhs(tri[...], lf) + carry_ref[...]
    carry_ref[...] = c[tm - 1:tm, :]
    for h in range(N_HEADS):
        col = c[:, GL_FOX_F + h:GL_FOX_F + h + 1]
        hi = col.astype(BF16).astype(F32)
        r1 = col - hi
        mid = r1.astype(BF16).astype(F32)
        lo = (r1 - mid).astype(BF16).astype(F32)
        base = HEAD_DIM
        one_q = jnp.where((lane >= base + 3) & (lane < base + 6), 1.0, 0.0)
        aug_q = jnp.where(lane == base, hi, jnp.where(lane == base + 1, mid, jnp.where(lane == base + 2, lo, one_q)))
        one_k = jnp.where((lane >= base) & (lane < base + 3), 1.0, 0.0)
        aug_k = jnp.where(lane == base + 3, -hi, jnp.where(lane == base + 4, -mid, jnp.where(lane == base + 5, -lo, one_k)))
        aug_v = jnp.where(lane == base, 1.0, 0.0)
        sl = slice(h * LANES, (h + 1) * LANES)
        fqa[:, sl] = _head_group(fq, h, aug_q).astype(BF16)
        fka[:, sl] = _head_group(fk, h, aug_k).astype(BF16)
        fva[:, sl] = _head_group(fv, h, aug_v).astype(BF16)

    mq = _rotary(_head_rms(zmq[...], mqn[...], g), cos[...], sin[...])
    mk = _rotary(_head_rms(zmk[...], mkn[...], g), cos[...], sin[...])
    mv = zmv[...]
    mk32[...] = mk.T
    mv32[...] = mv.T
    km = kmean_ref[...]
    lane_w = lax.broadcasted_iota(jnp.int32, km.shape, 1)
    kbd = jnp.concatenate([jnp.where(lane_w // HEAD_DIM == h, km, 0.0) for h in range(N_HEADS)], axis=0)
    qh = mq.astype(BF16)
    ql = (mq - qh.astype(F32)).astype(BF16)
    kh = kbd.astype(BF16)
    kl = (kbd - kh.astype(F32)).astype(BF16)
    gate = _dot_nt(qh, kh) + _dot_nt(qh, kl) + _dot_nt(ql, kh)
    blk = lane % MOBA_SEL_LANES
    gate = jnp.where(blk < j, gate, NEG)
    sel = jnp.zeros((tm, LANES), F32)
    ghs = [jnp.where(lane // MOBA_SEL_LANES == h, gate, -jnp.inf) for h in range(N_HEADS)]
    for _ in range(MOBA_TOPK):
        mxs = [jnp.max(gh, axis=-1, keepdims=True) for gh in ghs]
        firsts = [jnp.min(jnp.where(gh == mx, lane, 2 * LANES), axis=-1, keepdims=True) for gh, mx in zip(ghs, mxs)]
        for h in range(N_HEADS):
            pick = lane == firsts[h]
            sel = jnp.where(pick, 1.0, sel)
            ghs[h] = jnp.where(pick, -jnp.inf, ghs[h])
    pen = jnp.where((sel > 0.0) | (blk == j), 0.0, NEG)
    in_aug = (lane >= HEAD_DIM) & (lane < HEAD_DIM + MOBA_SEL_LANES)
    aug_k = jnp.where(in_aug & (lane - HEAD_DIM == j), 1.0, 0.0)
    aug_v = jnp.where(lane == HEAD_DIM, 1.0, 0.0)
    mqs = mq * QK_SCALE
    for h in range(N_HEADS):
        shift = (HEAD_DIM - MOBA_SEL_LANES * h) % LANES
        moved = pltpu.roll(pen, shift, 1) if shift else pen
        aug_q = jnp.where(in_aug, moved, 0.0)
        sl = slice(h * LANES, (h + 1) * LANES)
        mqa[:, sl] = _head_group(mqs, h, aug_q).astype(BF16)
        mka[:, sl] = _head_group(mk, h, aug_k).astype(BF16)
        mva[:, sl] = _head_group(mv, h, aug_v).astype(BF16)
    row = lax.broadcasted_iota(jnp.int32, km.shape, 0)
    kmean_ref[...] = jnp.where(row == j, jnp.mean(mk, axis=0, keepdims=True), km)


def _prep_prompt(z, nb, t, norms, fbias, cos, sin, gsum, tri):
    tm = MOBA_BLOCK
    nt = t // tm
    assert t % tm == 0 and nt <= MOBA_SEL_LANES
    n = nb * t
    zb = lambda blk: pl.BlockSpec((tm, GROUP_WIDTH), lambda b, j: (b * nt + j, blk))
    wide = 4 * LANES
    aug = (pl.BlockSpec((tm, wide), lambda b, j: (b * nt + j, 0)), jax.ShapeDtypeStruct((n, wide), BF16))
    ct = lambda r: (pl.BlockSpec((None, r, tm), lambda b, j: (b, 0, j)), jax.ShapeDtypeStruct((nb, r, t), F32))
    outs = [aug] * 3 + [ct(GROUP_WIDTH), ct(GROUP_WIDTH), ct(SUBLANES)] + [aug] * 3 + [ct(GROUP_WIDTH), ct(GROUP_WIDTH)]
    return pl.pallas_call(
        _prep_prompt_kernel,
        grid=(nb, nt),
        in_specs=[zb(ZB_FOX_Q), zb(ZB_FOX_K), zb(ZB_FOX_V), zb(ZB_MOBA_Q), zb(ZB_MOBA_K), zb(ZB_MOBA_V),
                  pl.BlockSpec((tm, LANES), lambda b, j: (b * nt + j, ZB_GATES))]
                 + [_const_spec((1, GROUP_WIDTH))] * 4 + [_const_spec((1, LANES))]
                 + [pl.BlockSpec((tm, GROUP_WIDTH), lambda b, j: (j, 0))] * 2
                 + [_const_spec((GROUP_WIDTH, GROUP_WIDTH)), _const_spec((tm, tm))],
        out_specs=[spec for spec, _ in outs],
        out_shape=[shape for _, shape in outs],
        scratch_shapes=[pltpu.VMEM((1, LANES), F32), pltpu.VMEM((MOBA_SEL_LANES, GROUP_WIDTH), F32)],
        compiler_params=_params(("arbitrary", "arbitrary")),
        name="prep_prompt",
    )(z, z, z, z, z, z, z, *norms, fbias, cos, sin, gsum, tri)


def _prep_sample_kernel(zfq, zfk, zmq, zmk, zg, fqn, fkn, mqn, mkn, fb, cos, sin, gsum,
                        fq32, fk32, flf, mq32, mk32):
    g = gsum[...]
    fq32[...] = _head_rms(zfq[...], fqn[...], g) * QK_SCALE
    fk32[...] = _head_rms(zfk[...], fkn[...], g)
    flf[...] = _log_sigmoid(zg[...] + fb[...])
    mq32[...] = _rotary(_head_rms(zmq[...], mqn[...], g), cos[...], sin[...]) * QK_SCALE
    mk32[...] = _rotary(_head_rms(zmk[...], mkn[...], g), cos[...], sin[...])


def _prep_sample(z, norms, fbias, cos, sin, gsum):
    n = z.shape[0]
    zb = lambda blk: pl.BlockSpec((n, GROUP_WIDTH), lambda i: (0, blk))
    full = lambda w: pl.BlockSpec((n, w), lambda i: (0, 0))
    outs = [GROUP_WIDTH, GROUP_WIDTH, LANES, GROUP_WIDTH, GROUP_WIDTH]
    return pl.pallas_call(
        _prep_sample_kernel,
        grid=(1,),
        in_specs=[zb(ZB_FOX_Q), zb(ZB_FOX_K), zb(ZB_MOBA_Q), zb(ZB_MOBA_K),
                  pl.BlockSpec((n, LANES), lambda i: (0, ZB_GATES))]
                 + [full(GROUP_WIDTH)] * 0 + [_const_spec((1, GROUP_WIDTH))] * 4 + [_const_spec((1, LANES))]
                 + [full(GROUP_WIDTH)] * 2 + [_const_spec((GROUP_WIDTH, GROUP_WIDTH))],
        out_specs=[full(w) for w in outs],
        out_shape=[jax.ShapeDtypeStruct((n, w), F32) for w in outs],
        compiler_params=_params(("arbitrary",)),
        name="prep_sample",
    )(z, z, z, z, z, *norms, fbias, cos, sin, gsum)


FLASH_ROWS = 256


def _flash_kernel(qi_ref, kj_ref, q_ref, k_ref, v_ref, o_ref, m_ref, acc_ref):
    s_idx = pl.program_id(1)
    i = qi_ref[s_idx]
    j = kj_ref[s_idx]
    tq = q_ref.shape[0]

    @pl.when(j == 0)
    def _():
        m_ref[...] = jnp.full_like(m_ref, NEG)
        acc_ref[...] = jnp.zeros_like(acc_ref)

    def step(diagonal):
        tk = k_ref.shape[0]
        chains = [(h, r0) for h in range(N_HEADS) for r0 in range(0, tq, FLASH_ROWS)]
        state = [(m_ref[h, r0:r0 + FLASH_ROWS, :], acc_ref[h, r0:r0 + FLASH_ROWS, :]) for h, r0 in chains]
        nks = [min(tk, -(-(r0 + FLASH_ROWS) // LANES) * LANES) if diagonal else tk for _, r0 in chains]
        scores = [_dot_nt(q_ref[r0:r0 + FLASH_ROWS, h * LANES:(h + 1) * LANES], k_ref[0:nk, h * LANES:(h + 1) * LANES])
                  for (h, r0), nk in zip(chains, nks)]
        probs = []
        for (h, r0), nk, s, (m_prev, _) in zip(chains, nks, scores, state):
            if diagonal:
                r = lax.broadcasted_iota(jnp.int32, s.shape, 0) + r0
                c = lax.broadcasted_iota(jnp.int32, s.shape, 1)
                s = jnp.where(r >= c, s, NEG)
            m_new = jnp.maximum(m_prev, jnp.max(s, axis=-1, keepdims=True))
            p = jnp.concatenate([jnp.exp(s[:, c0:c0 + LANES] - m_new) for c0 in range(0, nk, LANES)], axis=-1)
            probs.append((m_new, p.astype(BF16)))
        results = []
        for (h, r0), nk, (m_new, p), (m_prev, acc_prev) in zip(chains, nks, probs, state):
            acc = acc_prev * jnp.exp(m_prev - m_new) + _dot(p, v_ref[0:nk, h * LANES:(h + 1) * LANES])
            results.append((m_new, acc))
        for (h, r0), (m_new, acc) in zip(chains, results):
            rows = slice(r0, r0 + FLASH_ROWS)
            acc_ref[h, rows, :] = acc
            m_ref[h, rows, :] = m_new
            if diagonal:
                o_ref[rows, h * HEAD_DIM:(h + 1) * HEAD_DIM] = acc[:, :HEAD_DIM] / acc[:, HEAD_DIM:HEAD_DIM + 1]

    @pl.when(j < i)
    def _():
        step(False)

    @pl.when(j == i)
    def _():
        step(True)


def _flash(qa, ka, va, nb, t, ta):
    nt = t // ta
    assert t % ta == 0
    pairs = [(i, j) for i in range(nt) for j in range(i + 1)]
    qi = jnp.asarray([p[0] for p in pairs], jnp.int32)
    kj = jnp.asarray([p[1] for p in pairs], jnp.int32)
    wide = 4 * LANES
    qspec = pl.BlockSpec((ta, wide), lambda b, s, qi, kj: (b * nt + qi[s], 0))
    kspec = pl.BlockSpec((ta, wide), lambda b, s, qi, kj: (b * nt + kj[s], 0))
    return pl.pallas_call(
        _flash_kernel,
        grid_spec=pltpu.PrefetchScalarGridSpec(
            num_scalar_prefetch=2,
            grid=(nb, len(pairs)),
            in_specs=[qspec, kspec, kspec],
            out_specs=pl.BlockSpec((ta, GROUP_WIDTH), lambda b, s, qi, kj: (b * nt + qi[s], 0)),
            scratch_shapes=[pltpu.VMEM((N_HEADS, ta, LANES), F32), pltpu.VMEM((N_HEADS, ta, LANES), F32)]),
        out_shape=jax.ShapeDtypeStruct((nb * t, GROUP_WIDTH), F32),
        compiler_params=_params(("arbitrary", "arbitrary")),
        name="flash_attn",
    )(qi, kj, qa, ka, va)


CHUNKS_PER_STEP = 4


def _chunk_masks():
    r = lax.broadcasted_iota(jnp.int32, (CHUNK, CHUNK), 0)
    c = lax.broadcasted_iota(jnp.int32, (CHUNK, CHUNK), 1)
    return r >= c, r > c, r == c


def _valid_rows(t_valid, t_padded, rows):
    if t_valid == t_padded:
        return None
    return pl.program_id(1) * rows + lax.broadcasted_iota(jnp.int32, (rows, 1), 0) < t_valid


def _chunks_per_step(n_chunks):
    return math.gcd(n_chunks, CHUNKS_PER_STEP)


def _rms_heads_out(x, w):
    return x * lax.rsqrt(jnp.mean(x * x, axis=-1, keepdims=True) + NORM_EPS) * w


def _gdn_kernel(x_ref, gate_ref, zg_ref, hist_ref, w_ref, arow_ref, dtrow_ref, ng_ref, s0_ref,
                o_ref, sout_ref, s_ref, xprev_ref, xp_ref, *, t_valid, t_padded, n_sub):
    n = pl.program_id(1)
    hd, gw = HEAD_DIM, GROUP_WIDTH

    @pl.when(n == 0)
    def _():
        s_ref[...] = s0_ref[...]
        xprev_ref[...] = hist_ref[...]

    rows = n_sub * CHUNK
    x = x_ref[...]
    xp_ref[0:SUBLANES, :] = xprev_ref[...]
    xp_ref[SUBLANES:, :] = x
    xprev_ref[...] = x[rows - SUBLANES:, :]
    first = SUBLANES - (GDN_CONV - 1)
    y = w_ref[0:1, :] * xp_ref[first:first + rows, :]
    for jj in range(1, GDN_CONV):
        y = y + w_ref[jj:jj + 1, :] * xp_ref[first + jj:first + jj + rows, :]
    qkv = y * _sigmoid(y)

    gates = zg_ref[...]
    g_all = -jnp.exp(arow_ref[...]) * _softplus(gates + dtrow_ref[...])
    beta_all = _sigmoid(gates)
    valid = _valid_rows(t_valid, t_padded, rows)
    if valid is not None:
        g_all = jnp.where(valid, g_all, 0.0)
    incl, strict, eye = _chunk_masks()
    tri = incl.astype(BF16)
    gate_act = gate_ref[...]
    gate_act = gate_act * _sigmoid(gate_act)

    chains = [(c, h) for c in range(n_sub) for h in range(N_HEADS)]
    gc_all = [_dot3_rhs(tri, g_all[c * CHUNK:(c + 1) * CHUNK]) for c in range(n_sub)]
    rows_of = lambda c: slice(c * CHUNK, (c + 1) * CHUNK)
    qs = [qkv[rows_of(c), h * hd:(h + 1) * hd] for c, h in chains]
    ks = [qkv[rows_of(c), gw + h * hd:gw + (h + 1) * hd] for c, h in chains]
    qq = [jnp.sum(q * q, axis=-1, keepdims=True) for q in qs]
    kq = [jnp.sum(k * k, axis=-1, keepdims=True) for k in ks]
    qs = [q * lax.rsqrt(s + NORM_EPS) * QK_SCALE for q, s in zip(qs, qq)]
    ks = [k * lax.rsqrt(s + NORM_EPS) for k, s in zip(ks, kq)]
    gcs = [gc_all[c][:, GL_GDN_A + h:GL_GDN_A + h + 1] for c, h in chains]
    gc_rows = [_row_of_col(gc, CHUNK) for gc in gcs]
    pre = []
    for (c, h), q, k, gc, gc_row in zip(chains, qs, ks, gcs, gc_rows):
        rs = rows_of(c)
        v = qkv[rs, 2 * gw + h * hd:2 * gw + (h + 1) * hd]
        beta = beta_all[rs, GL_GDN_B + h:GL_GDN_B + h + 1]
        if valid is not None:
            k = jnp.where(valid[rs], k, 0.0)
            v = jnp.where(valid[rs], v, 0.0)
            beta = jnp.where(valid[rs], beta, 0.0)
        decay = jnp.exp(jnp.where(incl, gc - gc_row, NEG))
        pre.append((q, k, v, beta, gc, decay, k * beta, jnp.exp(gc)))
    kk = [_bdot_nt(kb, k) for (_, k, _, _, _, _, kb, _) in pre]
    qk = [_bdot_nt(q, k) for (q, k, _, _, _, _, _, _) in pre]
    lms = [jnp.where(strict, a * p[5], 0.0) for a, p in zip(kk, pre)]
    invs = [jnp.where(eye, 1.0, 0.0) - lm for lm in lms]
    pws = lms
    for _ in range((CHUNK - 1).bit_length() - 1):
        pws = [_bdot(pw, pw) for pw in pws]
        invs = [inv + _bdot(inv, pw) for inv, pw in zip(invs, pws)]
    uws = [_bdot(inv, jnp.concatenate([p[2] * p[3], p[6] * p[7]], axis=-1)) for inv, p in zip(invs, pre)]
    outs = {}
    for c in range(n_sub):
        idx = [c * N_HEADS + h for h in range(N_HEADS)]
        states = [s_ref[h] for h in range(N_HEADS)]
        v_new = [uws[i][:, :hd] - _bdot(uws[i][:, hd:], s) for i, s in zip(idx, states)]
        o_s = [_bdot(pre[i][0] * pre[i][7], s) for i, s in zip(idx, states)]
        o_v = [_bdot(qk[i] * pre[i][5], vn) for i, vn in zip(idx, v_new)]
        g_last = [pre[i][4][CHUNK - 1:CHUNK, :] for i in idx]
        upd = [_bdot_tn(pre[i][1] * jnp.exp(gl - pre[i][4]), vn) for i, gl, vn in zip(idx, g_last, v_new)]
        for h in range(N_HEADS):
            s_ref[h] = states[h] * jnp.exp(g_last[h]) + upd[h]
            outs[(c, h)] = o_s[h] + o_v[h]
    o_raw = [outs[ch] for ch in chains]
    ms = [jnp.mean(o * o, axis=-1, keepdims=True) for o in o_raw]
    o_fin = [o * lax.rsqrt(v + NORM_EPS) * ng_ref[...] * gate_act[rows_of(c), h * hd:(h + 1) * hd]
             for o, v, (c, h) in zip(o_raw, ms, chains)]
    o_ref[...] = jnp.concatenate(
        [jnp.concatenate(o_fin[c * N_HEADS:(c + 1) * N_HEADS], axis=-1) for c in range(n_sub)], axis=0)

    @pl.when(n == pl.num_programs(1) - 1)
    def _():
        sout_ref[...] = s_ref[...]


def _gdn(z, nb, t_padded, t_valid, hist, conv_w8, arow, dtrow, ng, s0):
    assert t_padded % CHUNK == 0
    n_sub = _chunks_per_step(t_padded // CHUNK)
    rows = n_sub * CHUNK
    nc = t_padded // rows
    qkv_w = 3 * GROUP_WIDTH
    assert (ZB_GDN_QKV * GROUP_WIDTH) % qkv_w == 0
    state = pl.BlockSpec((None, N_HEADS, HEAD_DIM, HEAD_DIM), lambda b, n: (b, 0, 0, 0))
    return pl.pallas_call(
        functools.partial(_gdn_kernel, t_valid=t_valid, t_padded=t_padded, n_sub=n_sub),
        grid=(nb, nc),
        in_specs=[pl.BlockSpec((rows, qkv_w), lambda b, n: (b * nc + n, ZB_GDN_QKV * GROUP_WIDTH // qkv_w)),
                  pl.BlockSpec((rows, GROUP_WIDTH), lambda b, n: (b * nc + n, ZB_GDN_G)),
                  pl.BlockSpec((rows, LANES), lambda b, n: (b * nc + n, ZB_GATES)),
                  pl.BlockSpec((None, SUBLANES, qkv_w), lambda b, n: (b, 0, 0)),
                  _const_spec((SUBLANES, qkv_w)), _const_spec((1, LANES)), _const_spec((1, LANES)),
                  _const_spec((1, HEAD_DIM)), state],
        out_specs=[pl.BlockSpec((rows, GROUP_WIDTH), lambda b, n: (b * nc + n, 0)), state],
        out_shape=[jax.ShapeDtypeStruct((nb * t_padded, GROUP_WIDTH), F32),
                   jax.ShapeDtypeStruct((nb, N_HEADS, HEAD_DIM, HEAD_DIM), F32)],
        scratch_shapes=[pltpu.VMEM((N_HEADS, HEAD_DIM, HEAD_DIM), F32),
                        pltpu.VMEM((SUBLANES, qkv_w), F32),
                        pltpu.VMEM((rows + SUBLANES, qkv_w), F32)],
        compiler_params=_params(("arbitrary", "arbitrary")),
        name="gdn_chunks",
    )(z, z, z, hist, conv_w8, arow, dtrow, ng, s0)


def _mlstm_kernel(q_ref, k_ref, v_ref, og_ref, zg_ref, ibrow_ref, fbrow_ref, ng_ref, c0_ref, n0_ref, m0_ref,
                  o_ref, cout_ref, nout_ref, mout_ref, c_ref, n_ref, m_ref, *, t_valid, t_padded, n_sub):
    step = pl.program_id(1)
    hd = HEAD_DIM

    @pl.when(step == 0)
    def _():
        c_ref[...] = c0_ref[...]
        n_ref[...] = n0_ref[...]
        m_ref[...] = m0_ref[...]

    rows = n_sub * CHUNK
    gates = zg_ref[...]
    ig_all = gates + ibrow_ref[...]
    lf_all = _log_sigmoid(gates + fbrow_ref[...])
    valid = _valid_rows(t_valid, t_padded, rows)
    if valid is not None:
        ig_all = jnp.where(valid, ig_all, NEG)
        lf_all = jnp.where(valid, lf_all, 0.0)
    incl, _, _ = _chunk_masks()
    tri = incl.astype(BF16)
    og = _sigmoid(og_ref[...])

    chains = [(c, h) for c in range(n_sub) for h in range(N_HEADS)]
    b_all = [_dot3_rhs(tri, lf_all[c * CHUNK:(c + 1) * CHUNK]) for c in range(n_sub)]
    bs = [b_all[c][:, GL_ML_F + h:GL_ML_F + h + 1] for c, h in chains]
    igs = [ig_all[c * CHUNK:(c + 1) * CHUNK, GL_ML_I + h:GL_ML_I + h + 1] for c, h in chains]
    dms = [jnp.where(incl, b - _row_of_col(b, CHUNK) + _row_of_col(ig, CHUNK), NEG) for b, ig in zip(bs, igs)]
    m_intras = [jnp.max(dm, axis=-1, keepdims=True) for dm in dms]
    pre = []
    for (c, h), b, ig, dm, mi in zip(chains, bs, igs, dms, m_intras):
        rs = slice(c * CHUNK, (c + 1) * CHUNK)
        sl = slice(h * hd, (h + 1) * hd)
        b_last = b[CHUNK - 1:CHUNK, :]
        pre.append((q_ref[rs, sl], k_ref[rs, sl] * QK_SCALE, v_ref[rs, sl], b, dm, mi, b_last, b_last - b + ig))
    qk = [_dot_nt(p[0], p[1]) for p in pre]
    m_start, m_next = {}, {}
    for h in range(N_HEADS):
        m = m_ref[h][:, :1]
        for c in range(n_sub):
            _, _, _, _, _, _, b_last, a_end = pre[c * N_HEADS + h]
            m_start[(c, h)] = m
            m = jnp.maximum(b_last + m, jnp.max(a_end, axis=0, keepdims=True))
            m_next[(c, h)] = m
        m_ref[h] = jnp.broadcast_to(m, (1, LANES))
    m_t = [jnp.maximum(p[3] + m_start[ch], p[5]) for ch, p in zip(chains, pre)]
    inter = [jnp.exp(p[3] + m_start[ch] - mt) for ch, p, mt in zip(chains, pre, m_t)]
    wmat = [jnp.exp(p[4] - mt) * a for p, mt, a in zip(pre, m_t, qk)]
    wv = [_dot(w, p[2]) for w, p in zip(wmat, pre)]
    wk = [jnp.exp(p[7] - m_next[ch]) * p[1] for ch, p in zip(chains, pre)]
    sc = [jnp.exp(p[6] + m_start[ch] - m_next[ch]) for ch, p in zip(chains, pre)]
    upd = [_dot_tn(a, p[2]) for a, p in zip(wk, pre)]
    c_start, n_start = {}, {}
    for h in range(N_HEADS):
        cs = c_ref[h]
        ns = n_ref[h]
        for c in range(n_sub):
            i = c * N_HEADS + h
            c_start[i], n_start[i] = cs, ns
            cs = sc[i] * cs + upd[i]
            ns = sc[i] * ns + jnp.sum(wk[i], axis=0, keepdims=True)
        c_ref[h] = cs
        n_ref[h] = ns
    qc = [_dot(p[0], c_start[i]) for i, p in enumerate(pre)]
    qn = [jnp.sum(p[0] * n_start[i], axis=-1, keepdims=True) for i, p in enumerate(pre)]
    wsum = [jnp.sum(w, axis=-1, keepdims=True) for w in wmat]
    hh = [(inter[i] * qc[i] + wv[i]) / jnp.maximum(jnp.abs(inter[i] * qn[i] + wsum[i]), jnp.exp(-m_t[i]))
          for i in range(len(chains))]
    ms = [jnp.mean(x * x, axis=-1, keepdims=True) for x in hh]
    outs = [x * lax.rsqrt(v + NORM_EPS) * ng_ref[...] * og[c * CHUNK:(c + 1) * CHUNK, h * hd:(h + 1) * hd]
            for x, v, (c, h) in zip(hh, ms, chains)]
    o_ref[...] = jnp.concatenate(
        [jnp.concatenate(outs[c * N_HEADS:(c + 1) * N_HEADS], axis=-1) for c in range(n_sub)], axis=0)

    @pl.when(step == pl.num_programs(1) - 1)
    def _():
        cout_ref[...] = c_ref[...]
        nout_ref[...] = n_ref[...]
        mout_ref[...] = m_ref[...]


def _mlstm(z, nb, t_padded, t_valid, ibrow, fbrow, ng, c0, n0, m0):
    assert t_padded % CHUNK == 0
    n_sub = _chunks_per_step(t_padded // CHUNK)
    rows = n_sub * CHUNK
    nc = t_padded // rows
    zb = lambda blk: pl.BlockSpec((rows, GROUP_WIDTH), lambda b, n: (b * nc + n, blk))
    st = lambda r, w: pl.BlockSpec((None, N_HEADS, r, w), lambda b, n: (b, 0, 0, 0))
    shp = lambda r, w: jax.ShapeDtypeStruct((nb, N_HEADS, r, w), F32)
    return pl.pallas_call(
        functools.partial(_mlstm_kernel, t_valid=t_valid, t_padded=t_padded, n_sub=n_sub),
        grid=(nb, nc),
        in_specs=[zb(ZB_ML_Q), zb(ZB_ML_K), zb(ZB_ML_V), zb(ZB_ML_O),
                  pl.BlockSpec((rows, LANES), lambda b, n: (b * nc + n, ZB_GATES)),
                  _const_spec((1, LANES)), _const_spec((1, LANES)), _const_spec((1, HEAD_DIM)),
                  st(HEAD_DIM, HEAD_DIM), st(1, HEAD_DIM), st(1, LANES)],
        out_specs=[pl.BlockSpec((rows, GROUP_WIDTH), lambda b, n: (b * nc + n, 0)),
                   st(HEAD_DIM, HEAD_DIM), st(1, HEAD_DIM), st(1, LANES)],
        out_shape=[jax.ShapeDtypeStruct((nb * t_padded, GROUP_WIDTH), F32),
                   shp(HEAD_DIM, HEAD_DIM), shp(1, HEAD_DIM), shp(1, LANES)],
        scratch_shapes=[pltpu.VMEM((N_HEADS, HEAD_DIM, HEAD_DIM), F32),
                        pltpu.VMEM((N_HEADS, 1, HEAD_DIM), F32),
                        pltpu.VMEM((N_HEADS, 1, LANES), F32)],
        compiler_params=_params(("arbitrary", "arbitrary")),
        name="mlstm_chunks",
    )(z, z, z, z, z, ibrow, fbrow, ng, c0, n0, m0)


N_QROWS = 16
PAGES_PER_STEP = 32
ONLINE_GROUP = 8


def _dup_rows(x8):
    return jnp.concatenate([x8, x8], axis=0)


def _col_of_row(row, n):
    r = lax.broadcasted_iota(jnp.int32, (n, row.shape[1]), 0)
    c = lax.broadcasted_iota(jnp.int32, (n, row.shape[1]), 1)
    return jnp.sum(jnp.where(r == c, row, 0.0), axis=1, keepdims=True)


def _row_of_col_padded(col):
    r = lax.broadcasted_iota(jnp.int32, (N_QROWS, LANES), 0)
    c = lax.broadcasted_iota(jnp.int32, (N_QROWS, LANES), 1)
    return jnp.sum(jnp.where(r == c, col, 0.0), axis=0, keepdims=True)


def _decode_kernel(*refs, fox, layer, n_pages, n_pages_step, n_blocks):
    pt_ref = refs[0]
    q_ref, knew_ref, vnew_ref = refs[1:4]
    pos = 4
    lfnew_ref = qt_ref = sfx_ref = lfc_ref = lfbuf = None
    if fox:
        lfnew_ref, sfx_ref = refs[pos], refs[pos + 1]
        pos += 2
    else:
        qt_ref = refs[pos]
        pos += 1
    kc_ref, vc_ref = refs[pos], refs[pos + 1]
    pos += 2
    if fox:
        lfc_ref = refs[pos]
        pos += 1
    o_ref = refs[pos]
    if fox:
        kbuf, vbuf, lfbuf, sem = refs[-4:]
        scratch = refs[pos + 1:-4]
    else:
        kbuf, vbuf, sem = refs[-3:]
        scratch = refs[pos + 1:-3]
    b = pl.program_id(0)
    g = pl.program_id(1)
    steps = pl.num_programs(1)
    last = steps - 1

    def page_copies(bb, gg, slot):
        cps = []
        for i in range(n_pages_step):
            where = gg * n_pages_step + i
            if fox:
                where = n_pages - 1 - where
            page = pt_ref[bb * n_pages + where]
            cps.append(pltpu.make_async_copy(kc_ref.at[layer, page], kbuf.at[slot, i], sem.at[0, slot]))
            cps.append(pltpu.make_async_copy(vc_ref.at[layer, page], vbuf.at[slot, i], sem.at[1, slot]))
            if fox:
                cps.append(pltpu.make_async_copy(lfc_ref.at[layer, page], lfbuf.at[slot, i], sem.at[2, slot]))
        return cps

    step = b * steps + g
    slot = lax.rem(step, 2)

    @pl.when(step == 0)
    def _():
        for cp in page_copies(0, 0, 0):
            cp.start()

    @pl.when(step + 1 < pl.num_programs(0) * steps)
    def _():
        wrap = g == last
        for cp in page_copies(jnp.where(wrap, b + 1, b), jnp.where(wrap, 0, g + 1), 1 - slot):
            cp.start()

    for cp in page_copies(b, g, slot):
        cp.wait()
    k_page = lambda i: kbuf[slot, i]
    v_page = lambda i: vbuf[slot, i]

    row = lax.broadcasted_iota(jnp.int32, (N_QROWS, GROUP_WIDTH), 0)
    lane_w = lax.broadcasted_iota(jnp.int32, (N_QROWS, GROUP_WIDTH), 1)
    qbd = jnp.where(lane_w // HEAD_DIM == row % N_HEADS, q_ref[...], 0.0)
    qbd16 = qbd.astype(BF16)
    tok = lax.broadcasted_iota(jnp.int32, (N_QROWS, LANES), 0) // N_HEADS
    key = lax.broadcasted_iota(jnp.int32, (N_QROWS, LANES), 1)
    new_ok = key <= tok

    def scores(kt_page):
        return _dot(qbd16, kt_page.astype(BF16))

    def weighted_values(p, vt_page):
        return _dot_nt(p.astype(BF16), vt_page.astype(BF16))

    if fox:
        m_ref, l_ref, acc_ref, run_ref, sq_ref = scratch

        def suffix_sums(lf_rows):
            both = _dot3_lhs(lf_rows, sfx_ref[...])
            return both[:, :LANES], both[:, LANES:]

        def online(s_list, v_list):
            mx = s_list[0]
            for s in s_list[1:]:
                mx = jnp.maximum(mx, s)
            m_prev = m_ref[...]
            m_new = jnp.maximum(m_prev, jnp.max(mx, axis=-1, keepdims=True))
            alpha = jnp.exp(m_prev - m_new)
            p_list = [jnp.exp(s - m_new) for s in s_list]
            pv = [weighted_values(p, v_page) for p, v_page in zip(p_list, v_list)]
            acc = acc_ref[...] * jnp.concatenate([alpha, alpha], axis=-1)
            for o in pv:
                acc = acc + o
            acc_ref[...] = acc
            l_ref[...] = l_ref[...] * alpha + jnp.sum(sum(p_list), axis=-1, keepdims=True)
            m_ref[...] = m_new

        @pl.when(g == 0)
        def _():
            m_ref[...] = jnp.full_like(m_ref, NEG)
            l_ref[...] = jnp.zeros_like(l_ref)
            acc_ref[...] = jnp.zeros_like(acc_ref)
            excl, tot = suffix_sums(lfnew_ref[...])
            excl16 = _dup_rows(excl)
            sq = jnp.broadcast_to(jnp.sum(jnp.where(key == tok, excl16, 0.0), axis=-1, keepdims=True),
                                  (N_QROWS, LANES))
            sq_ref[...] = sq
            run_ref[...] = _dup_rows(tot)
            s = jnp.where(new_ok, scores(knew_ref[...]) + excl16 - sq, NEG)
            online([s], [vnew_ref[...]])

        excl_all, tot_all = suffix_sums(lfbuf[slot].reshape(n_pages_step * SUBLANES, LANES))
        raw = [scores(k_page(i)) for i in range(n_pages_step)]
        run = run_ref[...]
        sq = sq_ref[...]
        s_list = []
        for i in range(n_pages_step):
            rs = slice(i * SUBLANES, (i + 1) * SUBLANES)
            s_list.append(raw[i] + _dup_rows(excl_all[rs]) + (run - sq))
            run = run + _dup_rows(tot_all[rs])
        run_ref[...] = run
        for i0 in range(0, n_pages_step, ONLINE_GROUP):
            i1 = min(i0 + ONLINE_GROUP, n_pages_step)
            online(s_list[i0:i1], [v_page(i) for i in range(i0, i1)])

        @pl.when(g == last)
        def _():
            l = l_ref[...]
            o_ref[...] = acc_ref[...] / jnp.concatenate([l, l], axis=-1)
    else:
        opart_ref, mt_ref, lt_ref, gt_ref, mown_ref, lown_ref, oown_ref = scratch
        blk_row = lax.broadcasted_iota(jnp.int32, (n_blocks, LANES), 0)

        @pl.when(g == 0)
        def _():
            mt_ref[...] = jnp.zeros_like(mt_ref)
            lt_ref[...] = jnp.zeros_like(lt_ref)
            gt_ref[...] = jnp.zeros_like(gt_ref)
            s = jnp.where(new_ok, scores(knew_ref[...]), NEG)
            m = jnp.max(s, axis=-1, keepdims=True)
            p = jnp.exp(s - m)
            mown_ref[...] = jnp.broadcast_to(m, (N_QROWS, LANES))
            lown_ref[...] = jnp.broadcast_to(jnp.sum(p, axis=-1, keepdims=True), (N_QROWS, LANES))
            oown_ref[...] = weighted_values(p, vnew_ref[...])

        crow = lax.broadcasted_iota(jnp.int32, (GROUP_WIDTH, LANES), 0)
        qlane = lax.broadcasted_iota(jnp.int32, (GROUP_WIDTH, LANES), 1)
        qbd_t = jnp.where(crow // HEAD_DIM == qlane % N_HEADS, qt_ref[...], 0.0)
        pages_per_block = MOBA_BLOCK // LANES
        blocks_step = n_pages_step // pages_per_block
        raw = [scores(k_page(i)) for i in range(n_pages_step)]
        stats = []
        for bi in range(blocks_step):
            s_pages = raw[bi * pages_per_block:(bi + 1) * pages_per_block]
            mx = s_pages[0]
            for s in s_pages[1:]:
                mx = jnp.maximum(mx, s)
            m = jnp.max(mx, axis=-1, keepdims=True)
            p_pages = [jnp.exp(s - m) for s in s_pages]
            stats.append((m, jnp.sum(sum(p_pages), axis=-1, keepdims=True), p_pages))
        pv = [[weighted_values(p, v_page(bi * pages_per_block + e)) for e, p in enumerate(st[2])]
              for bi, st in enumerate(stats)]
        mt, lt, gt = mt_ref[...], lt_ref[...], gt_ref[...]
        for bi in range(blocks_step):
            n = g * blocks_step + bi
            m, l, _ = stats[bi]
            ksum = sum(k_page(bi * pages_per_block + e) for e in range(pages_per_block))
            kmean = jnp.sum(ksum, axis=-1, keepdims=True) * (1.0 / MOBA_BLOCK)
            gate = jnp.sum(qbd_t * kmean, axis=0, keepdims=True)
            opart_ref[n] = sum(pv[bi])
            here = blk_row == n
            mt = jnp.where(here, _row_of_col_padded(m), mt)
            lt = jnp.where(here, _row_of_col_padded(l), lt)
            gt = jnp.where(here, gate, gt)
        mt_ref[...] = mt
        lt_ref[...] = lt
        gt_ref[...] = gt

        @pl.when(g == last)
        def _():
            gt = gt_ref[...]
            sel = jnp.zeros(gt.shape, F32)
            for _ in range(MOBA_TOPK):
                mx = jnp.max(gt, axis=0, keepdims=True)
                first = jnp.min(jnp.where(gt == mx, blk_row, n_blocks), axis=0, keepdims=True)
                pick = blk_row == first
                sel = jnp.where(pick, 1.0, sel)
                gt = jnp.where(pick, -jnp.inf, gt)
            mt = jnp.where(sel > 0.0, mt_ref[...], NEG)
            m_own = _row_of_col_padded(mown_ref[:, :1])
            l_own = _row_of_col_padded(lown_ref[:, :1])
            m_all = jnp.maximum(jnp.max(mt, axis=0, keepdims=True), m_own)
            w = jnp.where(sel > 0.0, jnp.exp(mt - m_all), 0.0)
            w_own = jnp.exp(m_own - m_all)
            denom = jnp.sum(w * lt_ref[...], axis=0, keepdims=True) + w_own * l_own
            w = w / denom
            out = oown_ref[...] * _col_of_row(w_own / denom, N_QROWS)
            for n in range(n_blocks):
                out = out + opart_ref[n] * _col_of_row(w[n:n + 1, :], N_QROWS)
            o_ref[...] = out


def _decode(page_table, layer, q16, knew_t, vnew_t, extra, cache_kt, cache_vt, cache_lf8, fox):
    nb, n_pages = page_table.shape
    pps = PAGES_PER_STEP
    assert n_pages % pps == 0 and cache_kt.shape[2:] == (GROUP_WIDTH, LANES)
    n_blocks = n_pages * LANES // MOBA_BLOCK
    assert (n_pages * LANES) % MOBA_BLOCK == 0 and n_blocks >= MOBA_TOPK and n_blocks % SUBLANES == 0
    steps = n_pages // pps
    hbm = pl.BlockSpec(memory_space=pl.ANY)
    seq = lambda r, w: pl.BlockSpec((None, r, w), lambda b, g, pt: (b, 0, 0))
    in_specs = [seq(N_QROWS, GROUP_WIDTH), seq(GROUP_WIDTH, LANES), seq(GROUP_WIDTH, LANES),
                seq(SUBLANES if fox else GROUP_WIDTH, LANES)]
    args = [q16, knew_t, vnew_t, extra]
    if fox:
        pos_i = jnp.arange(LANES)
        later = (pos_i[:, None] > pos_i[None, :]).astype(BF16)
        in_specs.append(pl.BlockSpec((LANES, 2 * LANES), lambda b, g, pt: (0, 0)))
        args.append(jnp.concatenate([later, jnp.ones((LANES, LANES), BF16)], axis=1))
    in_specs += [hbm, hbm]
    args += [cache_kt, cache_vt]
    ring = [pltpu.VMEM((2, pps, GROUP_WIDTH, LANES), F32), pltpu.VMEM((2, pps, GROUP_WIDTH, LANES), F32)]
    if fox:
        in_specs.append(hbm)
        args.append(cache_lf8)
        scratch = [pltpu.VMEM((N_QROWS, LANES), F32), pltpu.VMEM((N_QROWS, LANES), F32),
                   pltpu.VMEM((N_QROWS, GROUP_WIDTH), F32), pltpu.VMEM((N_QROWS, LANES), F32),
                   pltpu.VMEM((N_QROWS, LANES), F32)]
        ring.append(pltpu.VMEM((2, pps, SUBLANES, LANES), F32))
    else:
        scratch = [pltpu.VMEM((n_blocks, N_QROWS, GROUP_WIDTH), F32)] + [pltpu.VMEM((n_blocks, LANES), F32)] * 3 + [
            pltpu.VMEM((N_QROWS, LANES), F32), pltpu.VMEM((N_QROWS, LANES), F32),
            pltpu.VMEM((N_QROWS, GROUP_WIDTH), F32)]
    scratch = scratch + ring + [pltpu.SemaphoreType.DMA((len(ring), 2))]
    return pl.pallas_call(
        functools.partial(_decode_kernel, fox=fox, layer=layer, n_pages=n_pages, n_pages_step=pps, n_blocks=n_blocks),
        grid_spec=pltpu.PrefetchScalarGridSpec(
            num_scalar_prefetch=1, grid=(nb, steps), in_specs=in_specs,
            out_specs=seq(N_QROWS, GROUP_WIDTH), scratch_shapes=scratch),
        out_shape=jax.ShapeDtypeStruct((nb, N_QROWS, GROUP_WIDTH), F32),
        compiler_params=_params(("arbitrary", "arbitrary")),
        name="fox_decode" if fox else "moba_decode",
    )(page_table.reshape(-1), *args)


def _pack_w_in(w):
    gw, nh = GROUP_WIDTH, N_HEADS
    widths = [('fox_q', gw), ('fox_k', gw), ('fox_v', gw), ('fox_f', nh), ('gdn_qkv', 3 * gw), ('gdn_a', nh),
              ('gdn_b', nh), ('gdn_g', gw), ('moba_q', gw), ('moba_k', gw), ('moba_v', gw), ('mlstm_q', gw),
              ('mlstm_k', gw), ('mlstm_v', gw), ('mlstm_i', nh), ('mlstm_f', nh), ('mlstm_o', gw)]
    wt = w.T
    cols, off = {}, 0
    for name, width in widths:
        cols[name] = wt[off:off + width]
        off += width
    assert off == wt.shape[0]
    wide = [cols[k] for k in ('fox_q', 'fox_k', 'fox_v', 'gdn_qkv', 'gdn_g', 'moba_q', 'moba_k', 'moba_v',
                              'mlstm_q', 'mlstm_k', 'mlstm_v', 'mlstm_o')]
    gates = [cols[k] for k in ('fox_f', 'gdn_a', 'gdn_b', 'mlstm_i', 'mlstm_f')]
    pad = jnp.zeros((LANES - 5 * nh, wt.shape[1]), w.dtype)
    return jnp.concatenate(wide + gates + [pad], axis=0).astype(BF16)


def _gate_row(vec, lane0):
    return jnp.zeros((1, LANES), F32).at[0, lane0:lane0 + N_HEADS].set(vec.astype(F32))


def _tile_heads(vec):
    return jnp.tile(vec.astype(F32), N_HEADS).reshape(1, GROUP_WIDTH)


def _rope_tables(pos):
    half = HEAD_DIM // 2
    inv_freq = jnp.power(ROPE_THETA, -jnp.arange(half, dtype=F32) / half)
    ang = pos.astype(F32)[:, None] * inv_freq[None, :]
    cos = jnp.cos(ang)
    sin = jnp.sin(ang)
    cos_h = jnp.concatenate([cos, cos], axis=-1)
    sin_h = jnp.concatenate([-sin, sin], axis=-1)
    return jnp.tile(cos_h, (1, N_HEADS)), jnp.tile(sin_h, (1, N_HEADS))


def _head_sum_matrix():
    lane = jnp.arange(GROUP_WIDTH)
    return (lane[:, None] // HEAD_DIM == lane[None, :] // HEAD_DIM).astype(BF16)


def _lower_tri(n):
    r = jnp.arange(n)
    return (r[:, None] >= r[None, :]).astype(BF16)


def _pad_rows(a, rows, front=0):
    return jnp.pad(a, ((0, 0), (front, rows - front - a.shape[1]), (0, 0)))


ROW_TILE = 512
ATTN_TILE = 512


def _layer_weights(w_in, w_out, w_gate, w_up, w_down):
    return (_pack_w_in(w_in), w_out.astype(BF16), w_gate.astype(BF16), w_up.astype(BF16), w_down.astype(BF16))


def _zcols(z, blk, width=GROUP_WIDTH):
    return z[:, blk * GROUP_WIDTH:blk * GROUP_WIDTH + width]


def _prompt_layer(x, lw, lp, consts):
    nb, t, d = x.shape
    wp, wo, wg, wu, wd = lw
    x2 = x.reshape(nb * t, d)
    z = _proj(x2, lp['ln1'], wp, ROW_TILE)
    fqa, fka, fva, fk_ct, fv_ct, flf_ct, mqa, mka, mva, mk_ct, mv_ct = _prep_prompt(
        z, nb, t, lp['attn_norms'], lp['fox_fb_row'], consts['cos_p'], consts['sin_p'], consts['gsum'], consts['tri'])
    o_a = _flash(fqa, fka, fva, nb, t, ATTN_TILE)
    o_c = _flash(mqa, mka, mva, nb, t, ATTN_TILE)
    zeros_state = jnp.zeros((nb, N_HEADS, HEAD_DIM, HEAD_DIM), F32)
    o_b, gdn_s = _gdn(z, nb, t, t, jnp.zeros((nb, SUBLANES, 3 * GROUP_WIDTH), F32), lp['conv_w8'],
                      lp['gdn_a_row'], lp['gdn_dt_row'], lp['gdn_ng'], zeros_state)
    o_d, ml_c, ml_n, ml_m = _mlstm(z, nb, t, t, lp['ml_ib_row'], lp['ml_fb_row'], lp['ml_ng'], zeros_state,
                                   jnp.zeros((nb, N_HEADS, 1, HEAD_DIM), F32), jnp.zeros((nb, N_HEADS, 1, LANES), F32))
    y = _out_ffn(x2, (o_a, o_b, o_c, o_d), wo, lp['ln2'], wg, wu, wd, ROW_TILE).reshape(nb, t, d)
    heads = lambda a: jnp.transpose(a.reshape(nb, N_HEADS, HEAD_DIM, t), (0, 3, 1, 2))
    z3 = z.reshape(nb, t, Z_WIDTH)
    qkv0 = ZB_GDN_QKV * GROUP_WIDTH
    states = (heads(fk_ct), heads(fv_ct), jnp.swapaxes(flf_ct[:, :N_HEADS], 1, 2),
              heads(mk_ct), heads(mv_ct),
              z3[:, t - (GDN_CONV - 1):, qkv0:qkv0 + 3 * GROUP_WIDTH], gdn_s,
              ml_c, ml_n[:, :, 0, :], ml_m[:, :, 0, 0])
    return y, states


def _sample_layer(x, layer, lw, lp, consts, caches, page_table, st):
    nb, t, d = x.shape
    assert t <= N_HEADS and t >= GDN_CONV - 1 and nb * t % SUBLANES == 0
    wp, wo, wg, wu, wd = lw
    x2 = x.reshape(nb * t, d)
    z = _proj(x2, lp['ln1'], wp, nb * t)
    fq32, fk32, flf, mq32, mk32 = _prep_sample(z, lp['attn_norms'], lp['fox_fb_row'], consts['cos_s'], consts['sin_s'],
                                               consts['gsum'])
    seq = lambda a: a.reshape(nb, t, a.shape[-1])
    tok_rows = lambda a: _pad_rows(jnp.repeat(seq(a), N_HEADS, axis=1), N_QROWS)
    new_page = lambda a: jnp.pad(jnp.swapaxes(seq(a), 1, 2), ((0, 0), (0, 0), (0, LANES - t)))
    rows_t = lambda a: jnp.pad(jnp.swapaxes(tok_rows(a), 1, 2), ((0, 0), (0, 0), (0, LANES - N_QROWS)))
    lf_new = jnp.swapaxes(seq(flf[:, :N_HEADS]), 1, 2)
    lf_new = jnp.pad(lf_new, ((0, 0), (0, 0), (0, LANES - t)))
    lf_new8 = jnp.concatenate([lf_new, lf_new], axis=1)
    fox_k, fox_v, fox_lf8, moba_k, moba_v = caches

    def own_head(o16):
        o5 = o16.reshape(nb, N_HEADS, N_HEADS, N_HEADS, HEAD_DIM)[:, :t]
        hh = jnp.arange(N_HEADS)
        return o5[:, :, hh, hh, :].reshape(nb * t, GROUP_WIDTH)

    o_a = own_head(_decode(page_table, layer, tok_rows(fq32), new_page(fk32), new_page(_zcols(z, ZB_FOX_V)),
                           lf_new8, fox_k, fox_v, fox_lf8, True))
    o_c = own_head(_decode(page_table, layer, tok_rows(mq32), new_page(mk32), new_page(_zcols(z, ZB_MOBA_V)),
                           rows_t(mq32), moba_k, moba_v, None, False))
    zp = _pad_rows(seq(z), CHUNK).reshape(nb * CHUNK, Z_WIDTH)
    conv_buf, gdn_s0, ml_c0, ml_n0, ml_m0 = st
    real_rows = lambda a: a.reshape(nb, CHUNK, GROUP_WIDTH)[:, :t].reshape(nb * t, GROUP_WIDTH)
    o_b, gdn_s = _gdn(zp, nb, CHUNK, t, _pad_rows(conv_buf, SUBLANES, front=SUBLANES - (GDN_CONV - 1)), lp['conv_w8'],
                      lp['gdn_a_row'], lp['gdn_dt_row'], lp['gdn_ng'], gdn_s0)
    o_d, ml_c, ml_n, ml_m = _mlstm(zp, nb, CHUNK, t, lp['ml_ib_row'], lp['ml_fb_row'], lp['ml_ng'], ml_c0,
                                   ml_n0[:, :, None, :],
                                   jnp.broadcast_to(ml_m0[:, :, None, None], (nb, N_HEADS, 1, LANES)))
    y = _out_ffn(x2, (o_a, real_rows(o_b), o_c, real_rows(o_d)), wo, lp['ln2'], wg, wu, wd, nb * t).reshape(nb, t, d)
    heads = lambda a: a.reshape(nb, t, N_HEADS, HEAD_DIM)
    qkv0 = ZB_GDN_QKV * GROUP_WIDTH
    conv_rows = jnp.concatenate([conv_buf, seq(z)[:, :, qkv0:qkv0 + 3 * GROUP_WIDTH]], axis=1)
    states = (heads(fk32), heads(_zcols(z, ZB_FOX_V)), seq(flf[:, :N_HEADS]),
              heads(mk32), heads(_zcols(z, ZB_MOBA_V)),
              conv_rows[:, conv_rows.shape[1] - (GDN_CONV - 1):], gdn_s,
              ml_c, ml_n[:, :, 0, :], ml_m[:, :, 0, 0])
    return y, states


def kernel(x_prompt, x_sample, cache_fox_k, cache_fox_v, cache_fox_logf, cache_moba_k, cache_moba_v,
           state_gdn_conv, state_gdn_s, state_mlstm_c, state_mlstm_n, state_mlstm_m, page_table,
           ln1, w_in, fox_q_norm, fox_k_norm, fox_f_bias, gdn_conv_w, gdn_a_log, gdn_dt_bias, gdn_norm,
           moba_q_norm, moba_k_norm, mlstm_i_bias, mlstm_f_bias, mlstm_norm, w_out, ln2,
           w_gate, w_up, w_down):
    depth = w_in.shape[0]
    t_prompt = x_prompt.shape[1]
    nb_s, t_sample = x_sample.shape[:2]
    n_pool, page = cache_fox_k.shape[1:3]
    past_len = page_table.shape[1] * page
    assert page == LANES and past_len % MOBA_BLOCK == 0
    cos_p, sin_p = _rope_tables(jnp.arange(t_prompt, dtype=jnp.int32))
    cos_s, sin_s = _rope_tables(past_len + jnp.arange(t_sample, dtype=jnp.int32))
    consts = {'cos_p': cos_p, 'sin_p': sin_p, 'cos_s': jnp.tile(cos_s, (nb_s, 1)), 'sin_s': jnp.tile(sin_s, (nb_s, 1)),
              'gsum': _head_sum_matrix(), 'tri': _lower_tri(MOBA_BLOCK)}
    pages = lambda c: jnp.transpose(c, (0, 1, 3, 4, 2)).reshape(depth, n_pool, GROUP_WIDTH, page)
    lf_t = jnp.swapaxes(cache_fox_logf, 2, 3)
    caches = (pages(cache_fox_k), pages(cache_fox_v), jnp.concatenate([lf_t, lf_t], axis=2),
              pages(cache_moba_k), pages(cache_moba_v))
    y_p, y_s = x_prompt, x_sample
    p_states, s_states = [], []
    for i in range(depth):
        lw = _layer_weights(w_in[i], w_out[i], w_gate[i], w_up[i], w_down[i])
        lp = {
            'ln1': ln1[i], 'ln2': ln2[i],
            'attn_norms': [_tile_heads(v[i]) for v in (fox_q_norm, fox_k_norm, moba_q_norm, moba_k_norm)],
            'fox_fb_row': _gate_row(fox_f_bias[i], GL_FOX_F),
            'conv_w8': jnp.pad(gdn_conv_w[i].astype(F32), ((0, SUBLANES - GDN_CONV), (0, 0))),
            'gdn_a_row': _gate_row(gdn_a_log[i], GL_GDN_A), 'gdn_dt_row': _gate_row(gdn_dt_bias[i], GL_GDN_A),
            'gdn_ng': gdn_norm[i].reshape(1, HEAD_DIM).astype(F32),
            'ml_ib_row': _gate_row(mlstm_i_bias[i], GL_ML_I), 'ml_fb_row': _gate_row(mlstm_f_bias[i], GL_ML_F),
            'ml_ng': mlstm_norm[i].reshape(1, HEAD_DIM).astype(F32),
        }
        y_p, st_p = _prompt_layer(y_p, lw, lp, consts)
        st = (state_gdn_conv[i], state_gdn_s[i], state_mlstm_c[i], state_mlstm_n[i], state_mlstm_m[i])
        y_s, st_s = _sample_layer(y_s, i, lw, lp, consts, caches, page_table, st)
        p_states.append(st_p)
        s_states.append(st_s)
    stack = lambda states: [jnp.stack(a) for a in zip(*states)]
    return (y_p, y_s, *stack(p_states), *stack(s_states))
```

```python
import functools
import math

import jax
import jax.numpy as jnp
from jax import lax
from jax.experimental import pallas as pl
from jax.experimental.pallas import tpu as pltpu

F32 = jnp.float32
BF16 = jnp.bfloat16

HEAD_DIM = 64
N_HEADS = 4
GROUP_WIDTH = N_HEADS * HEAD_DIM
LANES = 128
SUBLANES = 8
MOBA_BLOCK = 256
MOBA_TOPK = 3
MOBA_SEL_LANES = 32
GDN_CONV = 4
CHUNK = 64
NORM_EPS = 1e-6
ROPE_THETA = 10000.0
NEG = -1e30
QK_SCALE = HEAD_DIM ** -0.5
VMEM_LIMIT = 48 * 1024 * 1024

ZB_FOX_Q, ZB_FOX_K, ZB_FOX_V = 0, 1, 2
ZB_GDN_QKV = 3
ZB_GDN_G = 6
ZB_MOBA_Q, ZB_MOBA_K, ZB_MOBA_V = 7, 8, 9
ZB_ML_Q, ZB_ML_K, ZB_ML_V, ZB_ML_O = 10, 11, 12, 13
Z_WIDE = 14 * GROUP_WIDTH
Z_WIDTH = Z_WIDE + LANES
ZB_GATES = Z_WIDE // LANES
GL_FOX_F, GL_GDN_A, GL_GDN_B, GL_ML_I, GL_ML_F = 0, 4, 8, 12, 16


def _dot(a, b):
    return jnp.dot(a, b, preferred_element_type=F32)


def _dot_nt(a, b):
    return lax.dot_general(a, b, (((1,), (1,)), ((), ())), preferred_element_type=F32)


def _dot_tn(a, b):
    return lax.dot_general(a, b, (((0,), (0,)), ((), ())), preferred_element_type=F32)


def _bdot(a, b):
    return _dot(a.astype(BF16), b.astype(BF16))


def _bdot_nt(a, b):
    return _dot_nt(a.astype(BF16), b.astype(BF16))


def _bdot_tn(a, b):
    return _dot_tn(a.astype(BF16), b.astype(BF16))


def _split3(x):
    hi = x.astype(BF16)
    r1 = x - hi.astype(F32)
    mid = r1.astype(BF16)
    lo = (r1 - mid.astype(F32)).astype(BF16)
    return hi, mid, lo


def _dot3_rhs(a_bf16, x):
    hi, mid, lo = _split3(x)
    return _dot(a_bf16, hi) + _dot(a_bf16, mid) + _dot(a_bf16, lo)


def _dot3_lhs(x, b_bf16):
    hi, mid, lo = _split3(x)
    return _dot(hi, b_bf16) + _dot(mid, b_bf16) + _dot(lo, b_bf16)


def _sigmoid(x):
    return 1.0 / (1.0 + jnp.exp(-x))


def _log_sigmoid(x):
    return jnp.minimum(x, 0.0) - jnp.log1p(jnp.exp(-jnp.abs(x)))


def _softplus(x):
    return jnp.maximum(x, 0.0) + jnp.log1p(jnp.exp(-jnp.abs(x)))


def _row_of_col(col, n):
    r = lax.broadcasted_iota(jnp.int32, (n, n), 0)
    c = lax.broadcasted_iota(jnp.int32, (n, n), 1)
    return jnp.sum(jnp.where(r == c, col, 0.0), axis=0, keepdims=True)


def _params(sem):
    return pltpu.CompilerParams(dimension_semantics=sem, vmem_limit_bytes=VMEM_LIMIT)


def _const_spec(shape):
    nd = len(shape)
    return pl.BlockSpec(shape, lambda *_: (0,) * nd, pipeline_mode=pl.Buffered(1))


PROJ_COL_CHUNK = 512


def _proj_kernel(x_ref, g_ref, w_ref, z_ref):
    x = x_ref[...]
    ms = jnp.mean(x * x, axis=-1, keepdims=True)
    h = (x * lax.rsqrt(ms + NORM_EPS) * g_ref[...]).astype(BF16)
    for c0 in range(0, Z_WIDTH, PROJ_COL_CHUNK):
        c1 = min(c0 + PROJ_COL_CHUNK, Z_WIDTH)
        z_ref[:, c0:c1] = _dot_nt(h, w_ref[c0:c1, :])


def _proj(x2d, g, w_packed, tm):
    n, d = x2d.shape
    return pl.pallas_call(
        _proj_kernel,
        grid=(n // tm,),
        in_specs=[pl.BlockSpec((tm, d), lambda i: (i, 0)),
                  _const_spec((1, d)),
                  _const_spec((Z_WIDTH, d))],
        out_specs=pl.BlockSpec((tm, Z_WIDTH), lambda i: (i, 0)),
        out_shape=jax.ShapeDtypeStruct((n, Z_WIDTH), F32),
        compiler_params=_params(("arbitrary",)),
        name="norm_proj",
    )(x2d, g.reshape(1, d), w_packed)


FFN_COL_CHUNK = 256


def _out_ffn_kernel(x_ref, oa_ref, ob_ref, oc_ref, od_ref, wo_ref, g2_ref, wg_ref, wu_ref, wd_ref, y_ref):
    x1 = x_ref[...]
    for gi, o_ref in enumerate((oa_ref, ob_ref, oc_ref, od_ref)):
        x1 = x1 + _dot(o_ref[...].astype(BF16), wo_ref[gi * GROUP_WIDTH:(gi + 1) * GROUP_WIDTH, :])
    ms = jnp.mean(x1 * x1, axis=-1, keepdims=True)
    h2 = (x1 * lax.rsqrt(ms + NORM_EPS) * g2_ref[...]).astype(BF16)
    y_ref[...] = x1
    d_ff = wg_ref.shape[1]
    for c0 in range(0, d_ff, FFN_COL_CHUNK):
        c1 = c0 + FFN_COL_CHUNK
        gate = _dot(h2, wg_ref[:, c0:c1])
        up = _dot(h2, wu_ref[:, c0:c1])
        act = (gate * _sigmoid(gate) * up).astype(BF16)
        y_ref[...] += _dot(act, wd_ref[c0:c1, :])


def _out_ffn(x2d, mix, wo, g2, wg, wu, wd, tm):
    n, d = x2d.shape
    d_ff = wg.shape[1]
    assert d_ff % FFN_COL_CHUNK == 0
    row = lambda w: pl.BlockSpec((tm, w), lambda i: (i, 0))
    return pl.pallas_call(
        _out_ffn_kernel,
        grid=(n // tm,),
        in_specs=[row(d)] + [row(GROUP_WIDTH)] * 4 + [
            _const_spec((d, d)), _const_spec((1, d)), _const_spec((d, d_ff)),
            _const_spec((d, d_ff)), _const_spec((d_ff, d))],
        out_specs=row(d),
        out_shape=jax.ShapeDtypeStruct((n, d), F32),
        compiler_params=_params(("arbitrary",)),
        name="out_ffn",
    )(x2d, *mix, wo, g2.reshape(1, d), wg, wu, wd)


def _lane_iota(rows):
    return lax.broadcasted_iota(jnp.int32, (rows, LANES), 1)


def _head_rms(x, w, gsum):
    y = x * x
    hi = y.astype(BF16)
    lo = (y - hi.astype(F32)).astype(BF16)
    ss = _dot(hi, gsum) + _dot(lo, gsum)
    return x * lax.rsqrt(ss * (1.0 / HEAD_DIM) + NORM_EPS) * w


def _rotary(x, cos, sin_signed):
    half = HEAD_DIM // 2
    out = []
    for p in range(GROUP_WIDTH // LANES):
        sl = slice(p * LANES, (p + 1) * LANES)
        v = x[:, sl]
        lane = _lane_iota(v.shape[0])
        partner = jnp.where(lane % HEAD_DIM < half,
                            pltpu.roll(v, LANES - half, 1), pltpu.roll(v, half, 1))
        out.append(v * cos[:, sl] + partner * sin_signed[:, sl])
    return jnp.concatenate(out, axis=-1)


def _head_group(x, h, aug):
    p, e = divmod(h, 2)
    piece = x[:, p * LANES:(p + 1) * LANES]
    if e:
        piece = pltpu.roll(piece, HEAD_DIM, 1)
    return jnp.where(_lane_iota(x.shape[0]) < HEAD_DIM, piece, aug)


def _prep_prompt_kernel(zfq, zfk, zfv, zmq, zmk, zmv, zg, fqn, fkn, mqn, mkn, fb, cos, sin, gsum, tri,
                        fqa, fka, fva, fk32, fv32, flf, mqa, mka, mva, mk32, mv32, carry_ref, kmean_ref):
    j = pl.program_id(1)
    tm = zfq.shape[0]
    lane = _lane_iota(tm)
    g = gsum[...]

    @pl.when(j == 0)
    def _():
        carry_ref[...] = jnp.zeros_like(carry_ref)
        kmean_ref[...] = jnp.zeros_like(kmean_ref)

    fq = _head_rms(zfq[...], fqn[...], g) * QK_SCALE
    fk = _head_rms(zfk[...], fkn[...], g)
    fv = zfv[...]
    fk32[...] = fk.T
    fv32[...] = fv.T
    lf = _log_sigmoid(zg[...] + fb[...])
    flf[...] = lf.T[:SUBLANES]
    c = _dot3_rhs(tri[...], lf) + carry_ref[...]
    carry_ref[...] = c[tm - 1:tm, :]
    for h in range(N_HEADS):
        col = c[:, GL_FOX_F + h:GL_FOX_F + h + 1]
        hi = col.astype(BF16).astype(F32)
        r1 = col - hi
        mid = r1.astype(BF16).astype(F32)
        lo = (r1 - mid).astype(BF16).astype(F32)
        base = HEAD_DIM
        one_q = jnp.where((lane >= base + 3) & (lane < base + 6), 1.0, 0.0)
        aug_q = jnp.where(lane == base, hi, jnp.where(lane == base + 1, mid, jnp.where(lane == base + 2, lo, one_q)))
        one_k = jnp.where((lane >= base) & (lane < base + 3), 1.0, 0.0)
        aug_k = jnp.where(lane == base + 3, -hi, jnp.where(lane == base + 4, -mid, jnp.where(lane == base + 5, -lo, one_k)))
        aug_v = jnp.where(lane == base, 1.0, 0.0)
        sl = slice(h * LANES, (h + 1) * LANES)
        fqa[:, sl] = _head_group(fq, h, aug_q).astype(BF16)
        fka[:, sl] = _head_group(fk, h, aug_k).astype(BF16)
        fva[:, sl] = _head_group(fv, h, aug_v).astype(BF16)

    mq = _rotary(_head_rms(zmq[...], mqn[...], g), cos[...], sin[...])
    mk = _rotary(_head_rms(zmk[...], mkn[...], g), cos[...], sin[...])
    mv = zmv[...]
    mk32[...] = mk.T
    mv32[...] = mv.T
    km = kmean_ref[...]
    lane_w = lax.broadcasted_iota(jnp.int32, km.shape, 1)
    kbd = jnp.concatenate([jnp.where(lane_w // HEAD_DIM == h, km, 0.0) for h in range(N_HEADS)], axis=0)
    qh, qm, ql = _split3(mq)
    kh, km_, kl = _split3(kbd)
    gate = (_dot_nt(qh, kh) + _dot_nt(qh, km_) + _dot_nt(qm, kh)
            + _dot_nt(qh, kl) + _dot_nt(ql, kh) + _dot_nt(qm, km_))
    blk = lane % MOBA_SEL_LANES
    gate = jnp.where(blk < j, gate, NEG)
    sel = jnp.zeros((tm, LANES), F32)
    ghs = [jnp.where(lane // MOBA_SEL_LANES == h, gate, -jnp.inf) for h in range(N_HEADS)]
    for _ in range(MOBA_TOPK):
        mxs = [jnp.max(gh, axis=-1, keepdims=True) for gh in ghs]
        firsts = [jnp.min(jnp.where(gh == mx, lane, 2 * LANES), axis=-1, keepdims=True) for gh, mx in zip(ghs, mxs)]
        for h in range(N_HEADS):
            pick = lane == firsts[h]
            sel = jnp.where(pick, 1.0, sel)
            ghs[h] = jnp.where(pick, -jnp.inf, ghs[h])
    pen = jnp.where((sel > 0.0) | (blk == j), 0.0, NEG)
    in_aug = (lane >= HEAD_DIM) & (lane < HEAD_DIM + MOBA_SEL_LANES)
    aug_k = jnp.where(in_aug & (lane - HEAD_DIM == j), 1.0, 0.0)
    aug_v = jnp.where(lane == HEAD_DIM, 1.0, 0.0)
    mqs = mq * QK_SCALE
    for h in range(N_HEADS):
        shift = (HEAD_DIM - MOBA_SEL_LANES * h) % LANES
        moved = pltpu.roll(pen, shift, 1) if shift else pen
        aug_q = jnp.where(in_aug, moved, 0.0)
        sl = slice(h * LANES, (h + 1) * LANES)
        mqa[:, sl] = _head_group(mqs, h, aug_q).astype(BF16)
        mka[:, sl] = _head_group(mk, h, aug_k).astype(BF16)
        mva[:, sl] = _head_group(mv, h, aug_v).astype(BF16)
    row = lax.broadcasted_iota(jnp.int32, km.shape, 0)
    kmean_ref[...] = jnp.where(row == j, jnp.mean(mk, axis=0, keepdims=True), km)


def _prep_prompt(z, nb, t, norms, fbias, cos, sin, gsum, tri):
    tm = MOBA_BLOCK
    nt = t // tm
    assert t % tm == 0 and nt <= MOBA_SEL_LANES
    n = nb * t
    zb = lambda blk: pl.BlockSpec((tm, GROUP_WIDTH), lambda b, j: (b * nt + j, blk))
    wide = 4 * LANES
    aug = (pl.BlockSpec((tm, wide), lambda b, j: (b * nt + j, 0)), jax.ShapeDtypeStruct((n, wide), BF16))
    ct = lambda r: (pl.BlockSpec((None, r, tm), lambda b, j: (b, 0, j)), jax.ShapeDtypeStruct((nb, r, t), F32))
    outs = [aug] * 3 + [ct(GROUP_WIDTH), ct(GROUP_WIDTH), ct(SUBLANES)] + [aug] * 3 + [ct(GROUP_WIDTH), ct(GROUP_WIDTH)]
    return pl.pallas_call(
        _prep_prompt_kernel,
        grid=(nb, nt),
        in_specs=[zb(ZB_FOX_Q), zb(ZB_FOX_K), zb(ZB_FOX_V), zb(ZB_MOBA_Q), zb(ZB_MOBA_K), zb(ZB_MOBA_V),
                  pl.BlockSpec((tm, LANES), lambda b, j: (b * nt + j, ZB_GATES))]
                 + [_const_spec((1, GROUP_WIDTH))] * 4 + [_const_spec((1, LANES))]
                 + [pl.BlockSpec((tm, GROUP_WIDTH), lambda b, j: (j, 0))] * 2
                 + [_const_spec((GROUP_WIDTH, GROUP_WIDTH)), _const_spec((tm, tm))],
        out_specs=[spec for spec, _ in outs],
        out_shape=[shape for _, shape in outs],
        scratch_shapes=[pltpu.VMEM((1, LANES), F32), pltpu.VMEM((MOBA_SEL_LANES, GROUP_WIDTH), F32)],
        compiler_params=_params(("arbitrary", "arbitrary")),
        name="prep_prompt",
    )(z, z, z, z, z, z, z, *norms, fbias, cos, sin, gsum, tri)


def _prep_sample_kernel(zfq, zfk, zmq, zmk, zg, fqn, fkn, mqn, mkn, fb, cos, sin, gsum,
                        fq32, fk32, flf, mq32, mk32):
    g = gsum[...]
    fq32[...] = _head_rms(zfq[...], fqn[...], g) * QK_SCALE
    fk32[...] = _head_rms(zfk[...], fkn[...], g)
    flf[...] = _log_sigmoid(zg[...] + fb[...])
    mq32[...] = _rotary(_head_rms(zmq[...], mqn[...], g), cos[...], sin[...]) * QK_SCALE
    mk32[...] = _rotary(_head_rms(zmk[...], mkn[...], g), cos[...], sin[...])


def _prep_sample(z, norms, fbias, cos, sin, gsum):
    n = z.shape[0]
    zb = lambda blk: pl.BlockSpec((n, GROUP_WIDTH), lambda i: (0, blk))
    full = lambda w: pl.BlockSpec((n, w), lambda i: (0, 0))
    outs = [GROUP_WIDTH, GROUP_WIDTH, LANES, GROUP_WIDTH, GROUP_WIDTH]
    return pl.pallas_call(
        _prep_sample_kernel,
        grid=(1,),
        in_specs=[zb(ZB_FOX_Q), zb(ZB_FOX_K), zb(ZB_MOBA_Q), zb(ZB_MOBA_K),
                  pl.BlockSpec((n, LANES), lambda i: (0, ZB_GATES))]
                 + [full(GROUP_WIDTH)] * 0 + [_const_spec((1, GROUP_WIDTH))] * 4 + [_const_spec((1, LANES))]
                 + [full(GROUP_WIDTH)] * 2 + [_const_spec((GROUP_WIDTH, GROUP_WIDTH))],
        out_specs=[full(w) for w in outs],
        out_shape=[jax.ShapeDtypeStruct((n, w), F32) for w in outs],
        compiler_params=_params(("arbitrary",)),
        name="prep_sample",
    )(z, z, z, z, z, *norms, fbias, cos, sin, gsum)


FLASH_ROWS = 256


def _flash_kernel(qi_ref, kj_ref, q_ref, k_ref, v_ref, o_ref, m_ref, acc_ref):
    s_idx = pl.program_id(1)
    i = qi_ref[s_idx]
    j = kj_ref[s_idx]
    tq = q_ref.shape[0]

    @pl.when(j == 0)
    def _():
        m_ref[...] = jnp.full_like(m_ref, NEG)
        acc_ref[...] = jnp.zeros_like(acc_ref)

    tk = k_ref.shape[0]
    row0 = i * tq
    col0 = j * tk
    chains = [(h, r0) for h in range(N_HEADS) for r0 in range(0, tq, FLASH_ROWS)]

    def step(masked):
        state = [(m_ref[h, r0:r0 + FLASH_ROWS, :], acc_ref[h, r0:r0 + FLASH_ROWS, :]) for h, r0 in chains]
        nks = [min(tk, -(-(r0 + FLASH_ROWS) // LANES) * LANES) if masked and tq == tk else tk for _, r0 in chains]
        scores = [_dot_nt(q_ref[r0:r0 + FLASH_ROWS, h * LANES:(h + 1) * LANES], k_ref[0:nk, h * LANES:(h + 1) * LANES])
                  for (h, r0), nk in zip(chains, nks)]
        probs = []
        for (h, r0), nk, s, (m_prev, _) in zip(chains, nks, scores, state):
            if masked:
                diag = tq == tk
                r = lax.broadcasted_iota(jnp.int32, s.shape, 0) + (r0 if diag else row0 + r0)
                c = lax.broadcasted_iota(jnp.int32, s.shape, 1) + (0 if diag else col0)
                s = jnp.where(r >= c, s, NEG)
            m_new = jnp.maximum(m_prev, jnp.max(s, axis=-1, keepdims=True))
            p = jnp.concatenate([jnp.exp(s[:, c0:c0 + LANES] - m_new) for c0 in range(0, nk, LANES)], axis=-1)
            probs.append((m_new, p.astype(BF16)))
        results = []
        for (h, r0), nk, (m_new, p), (m_prev, acc_prev) in zip(chains, nks, probs, state):
            acc = acc_prev * jnp.exp(m_prev - m_new) + _dot(p, v_ref[0:nk, h * LANES:(h + 1) * LANES])
            results.append((m_new, acc))
        for (h, r0), (m_new, acc) in zip(chains, results):
            rows = slice(r0, r0 + FLASH_ROWS)
            acc_ref[h, rows, :] = acc
            m_ref[h, rows, :] = m_new
            if masked and tq == tk:
                o_ref[rows, h * HEAD_DIM:(h + 1) * HEAD_DIM] = acc[:, :HEAD_DIM] / acc[:, HEAD_DIM:HEAD_DIM + 1]

    crosses = col0 + tk - 1 > row0

    @pl.when(jnp.logical_not(crosses))
    def _():
        step(False)

    @pl.when(crosses)
    def _():
        step(True)

    if tq != tk:
        @pl.when(col0 + tk >= row0 + tq)
        def _():
            for h, r0 in chains:
                acc = acc_ref[h, r0:r0 + FLASH_ROWS, :]
                o_ref[r0:r0 + FLASH_ROWS, h * HEAD_DIM:(h + 1) * HEAD_DIM] = acc[:, :HEAD_DIM] / acc[:, HEAD_DIM:HEAD_DIM + 1]


def _flash(qa, ka, va, nb, t, tq, tk):
    assert t % tq == 0 and t % tk == 0 and tq % FLASH_ROWS == 0
    nq, nk = t // tq, t // tk
    pairs = [(i, j) for i in range(nq) for j in range(((i + 1) * tq - 1) // tk + 1)]
    qi = jnp.asarray([p[0] for p in pairs], jnp.int32)
    kj = jnp.asarray([p[1] for p in pairs], jnp.int32)
    wide = 4 * LANES
    qspec = pl.BlockSpec((tq, wide), lambda b, s, qi, kj: (b * nq + qi[s], 0))
    kspec = pl.BlockSpec((tk, wide), lambda b, s, qi, kj: (b * nk + kj[s], 0))
    return pl.pallas_call(
        _flash_kernel,
        grid_spec=pltpu.PrefetchScalarGridSpec(
            num_scalar_prefetch=2,
            grid=(nb, len(pairs)),
            in_specs=[qspec, kspec, kspec],
            out_specs=pl.BlockSpec((tq, GROUP_WIDTH), lambda b, s, qi, kj: (b * nq + qi[s], 0)),
            scratch_shapes=[pltpu.VMEM((N_HEADS, tq, LANES), F32), pltpu.VMEM((N_HEADS, tq, LANES), F32)]),
        out_shape=jax.ShapeDtypeStruct((nb * t, GROUP_WIDTH), F32),
        compiler_params=_params(("arbitrary", "arbitrary")),
        name="flash_attn",
    )(qi, kj, qa, ka, va)


CHUNKS_PER_STEP = 4


def _chunk_masks():
    r = lax.broadcasted_iota(jnp.int32, (CHUNK, CHUNK), 0)
    c = lax.broadcasted_iota(jnp.int32, (CHUNK, CHUNK), 1)
    return r >= c, r > c, r == c


def _valid_rows(t_valid, t_padded, seq_rows, n_seq):
    if t_valid == t_padded:
        return None
    local = lax.rem(lax.broadcasted_iota(jnp.int32, (n_seq * seq_rows, 1), 0), seq_rows)
    return pl.program_id(1) * seq_rows + local < t_valid


def _step_shape(nb, n_chunks):
    n_sub = math.gcd(n_chunks, CHUNKS_PER_STEP)
    n_seq = math.gcd(nb, CHUNKS_PER_STEP // n_sub) if n_sub == n_chunks else 1
    return n_seq, n_sub


def _gdn_kernel(x_ref, gate_ref, zg_ref, hist_ref, w_ref, arow_ref, dtrow_ref, ng_ref, s0_ref,
                o_ref, sout_ref, s_ref, xprev_ref, xp_ref, *, t_valid, t_padded, n_sub, n_seq):
    n = pl.program_id(1)
    hd, gw = HEAD_DIM, GROUP_WIDTH

    @pl.when(n == 0)
    def _():
        s_ref[...] = s0_ref[...]
        xprev_ref[...] = hist_ref[...]

    seq_rows = n_sub * CHUNK
    x = x_ref[...]
    first = SUBLANES - (GDN_CONV - 1)
    ys = []
    for g in range(n_seq):
        xg = x[g * seq_rows:(g + 1) * seq_rows]
        xp_ref[g, 0:SUBLANES, :] = xprev_ref[g]
        xp_ref[g, SUBLANES:, :] = xg
        xprev_ref[g] = xg[seq_rows - SUBLANES:, :]
        yg = w_ref[0:1, :] * xp_ref[g, first:first + seq_rows, :]
        for jj in range(1, GDN_CONV):
            yg = yg + w_ref[jj:jj + 1, :] * xp_ref[g, first + jj:first + jj + seq_rows, :]
        ys.append(yg)
    y = jnp.concatenate(ys, axis=0) if n_seq > 1 else ys[0]
    qkv = y * _sigmoid(y)

    gates = zg_ref[...]
    g_all = -jnp.exp(arow_ref[...]) * _softplus(gates + dtrow_ref[...])
    beta_all = _sigmoid(gates)
    valid = _valid_rows(t_valid, t_padded, seq_rows, n_seq)
    if valid is not None:
        g_all = jnp.where(valid, g_all, 0.0)
    incl, strict, eye = _chunk_masks()
    tri = incl.astype(BF16)
    gate_act = gate_ref[...]
    gate_act = gate_act * _sigmoid(gate_act)

    n_chunks = n_seq * n_sub
    chains = [(c, h) for c in range(n_chunks) for h in range(N_HEADS)]
    gc_all = [_dot3_rhs(tri, g_all[c * CHUNK:(c + 1) * CHUNK]) for c in range(n_chunks)]
    rows_of = lambda c: slice(c * CHUNK, (c + 1) * CHUNK)
    qs = [qkv[rows_of(c), h * hd:(h + 1) * hd] for c, h in chains]
    ks = [qkv[rows_of(c), gw + h * hd:gw + (h + 1) * hd] for c, h in chains]
    qq = [jnp.sum(q * q, axis=-1, keepdims=True) for q in qs]
    kq = [jnp.sum(k * k, axis=-1, keepdims=True) for k in ks]
    qs = [q * lax.rsqrt(s + NORM_EPS) * QK_SCALE for q, s in zip(qs, qq)]
    ks = [k * lax.rsqrt(s + NORM_EPS) for k, s in zip(ks, kq)]
    gcs = [gc_all[c][:, GL_GDN_A + h:GL_GDN_A + h + 1] for c, h in chains]
    gc_rows = [_row_of_col(gc, CHUNK) for gc in gcs]
    pre = []
    for (c, h), q, k, gc, gc_row in zip(chains, qs, ks, gcs, gc_rows):
        rs = rows_of(c)
        v = qkv[rs, 2 * gw + h * hd:2 * gw + (h + 1) * hd]
        beta = beta_all[rs, GL_GDN_B + h:GL_GDN_B + h + 1]
        if valid is not None:
            k = jnp.where(valid[rs], k, 0.0)
            v = jnp.where(valid[rs], v, 0.0)
            beta = jnp.where(valid[rs], beta, 0.0)
        decay = jnp.exp(jnp.where(incl, gc - gc_row, NEG))
        pre.append((q, k, v, beta, gc, decay, k * beta, jnp.exp(gc)))
    kk = [_bdot_nt(kb, k) for (_, k, _, _, _, _, kb, _) in pre]
    qk = [_bdot_nt(q, k) for (q, k, _, _, _, _, _, _) in pre]
    lms = [jnp.where(strict, a * p[5], 0.0) for a, p in zip(kk, pre)]
    invs = [jnp.where(eye, 1.0, 0.0) - lm for lm in lms]
    pws = lms
    for _ in range((CHUNK - 1).bit_length() - 1):
        pws = [_bdot(pw, pw) for pw in pws]
        invs = [inv + _bdot(inv, pw) for inv, pw in zip(invs, pws)]
    uws = [_bdot(inv, jnp.concatenate([p[2] * p[3], p[6] * p[7]], axis=-1)) for inv, p in zip(invs, pre)]
    outs = {}
    lanes_of = [(g, h) for g in range(n_seq) for h in range(N_HEADS)]
    for c in range(n_sub):
        idx = [(g * n_sub + c) * N_HEADS + h for g, h in lanes_of]
        states = [s_ref[g, h] for g, h in lanes_of]
        v_new = [uws[i][:, :hd] - _bdot(uws[i][:, hd:], s) for i, s in zip(idx, states)]
        o_s = [_bdot(pre[i][0] * pre[i][7], s) for i, s in zip(idx, states)]
        o_v = [_bdot(qk[i] * pre[i][5], vn) for i, vn in zip(idx, v_new)]
        g_last = [pre[i][4][CHUNK - 1:CHUNK, :] for i in idx]
        upd = [_bdot_tn(pre[i][1] * jnp.exp(gl - pre[i][4]), vn) for i, gl, vn in zip(idx, g_last, v_new)]
        for k_, (g, h) in enumerate(lanes_of):
            s_ref[g, h] = states[k_] * jnp.exp(g_last[k_]) + upd[k_]
            outs[chains[idx[k_]]] = o_s[k_] + o_v[k_]
    o_raw = [outs[ch] for ch in chains]
    ms = [jnp.mean(o * o, axis=-1, keepdims=True) for o in o_raw]
    o_fin = [o * lax.rsqrt(v + NORM_EPS) * ng_ref[...] * gate_act[rows_of(c), h * hd:(h + 1) * hd]
             for o, v, (c, h) in zip(o_raw, ms, chains)]
    o_ref[...] = jnp.concatenate(
        [jnp.concatenate(o_fin[c * N_HEADS:(c + 1) * N_HEADS], axis=-1) for c in range(n_chunks)], axis=0)

    @pl.when(n == pl.num_programs(1) - 1)
    def _():
        sout_ref[...] = s_ref[...]


def _gdn(z, nb, t_padded, t_valid, hist, conv_w8, arow, dtrow, ng, s0):
    assert t_padded % CHUNK == 0
    n_seq, n_sub = _step_shape(nb, t_padded // CHUNK)
    seq_rows = n_sub * CHUNK
    rows = n_seq * seq_rows
    nc = t_padded // seq_rows
    assert n_seq == 1 or nc == 1
    qkv_w = 3 * GROUP_WIDTH
    assert (ZB_GDN_QKV * GROUP_WIDTH) % qkv_w == 0
    state = pl.BlockSpec((n_seq, N_HEADS, HEAD_DIM, HEAD_DIM), lambda b, n: (b, 0, 0, 0))
    return pl.pallas_call(
        functools.partial(_gdn_kernel, t_valid=t_valid, t_padded=t_padded, n_sub=n_sub, n_seq=n_seq),
        grid=(nb // n_seq, nc),
        in_specs=[pl.BlockSpec((rows, qkv_w), lambda b, n: (b * nc + n, ZB_GDN_QKV * GROUP_WIDTH // qkv_w)),
                  pl.BlockSpec((rows, GROUP_WIDTH), lambda b, n: (b * nc + n, ZB_GDN_G)),
                  pl.BlockSpec((rows, LANES), lambda b, n: (b * nc + n, ZB_GATES)),
                  pl.BlockSpec((n_seq, SUBLANES, qkv_w), lambda b, n: (b, 0, 0)),
                  _const_spec((SUBLANES, qkv_w)), _const_spec((1, LANES)), _const_spec((1, LANES)),
                  _const_spec((1, HEAD_DIM)), state],
        out_specs=[pl.BlockSpec((rows, GROUP_WIDTH), lambda b, n: (b * nc + n, 0)), state],
        out_shape=[jax.ShapeDtypeStruct((nb * t_padded, GROUP_WIDTH), F32),
                   jax.ShapeDtypeStruct((nb, N_HEADS, HEAD_DIM, HEAD_DIM), F32)],
        scratch_shapes=[pltpu.VMEM((n_seq, N_HEADS, HEAD_DIM, HEAD_DIM), F32),
                        pltpu.VMEM((n_seq, SUBLANES, qkv_w), F32),
                        pltpu.VMEM((n_seq, seq_rows + SUBLANES, qkv_w), F32)],
        compiler_params=_params(("arbitrary", "arbitrary")),
        name="gdn_chunks",
    )(z, z, z, hist, conv_w8, arow, dtrow, ng, s0)


def _mlstm_kernel(q_ref, k_ref, v_ref, og_ref, zg_ref, ibrow_ref, fbrow_ref, ng_ref, c0_ref, n0_ref, m0_ref,
                  o_ref, cout_ref, nout_ref, mout_ref, c_ref, n_ref, m_ref, *, t_valid, t_padded, n_sub, n_seq):
    step = pl.program_id(1)
    hd = HEAD_DIM

    @pl.when(step == 0)
    def _():
        c_ref[...] = c0_ref[...]
        n_ref[...] = n0_ref[...]
        m_ref[...] = m0_ref[...]

    gates = zg_ref[...]
    ig_all = gates + ibrow_ref[...]
    lf_all = _log_sigmoid(gates + fbrow_ref[...])
    valid = _valid_rows(t_valid, t_padded, n_sub * CHUNK, n_seq)
    if valid is not None:
        ig_all = jnp.where(valid, ig_all, NEG)
        lf_all = jnp.where(valid, lf_all, 0.0)
    incl, _, _ = _chunk_masks()
    tri = incl.astype(BF16)
    og = _sigmoid(og_ref[...])

    n_chunks = n_seq * n_sub
    chains = [(c, h) for c in range(n_chunks) for h in range(N_HEADS)]
    b_all = [_dot3_rhs(tri, lf_all[c * CHUNK:(c + 1) * CHUNK]) for c in range(n_chunks)]
    bs = [b_all[c][:, GL_ML_F + h:GL_ML_F + h + 1] for c, h in chains]
    igs = [ig_all[c * CHUNK:(c + 1) * CHUNK, GL_ML_I + h:GL_ML_I + h + 1] for c, h in chains]
    dms = [jnp.where(incl, b - _row_of_col(b, CHUNK) + _row_of_col(ig, CHUNK), NEG) for b, ig in zip(bs, igs)]
    m_intras = [jnp.max(dm, axis=-1, keepdims=True) for dm in dms]
    pre = []
    for (c, h), b, ig, dm, mi in zip(chains, bs, igs, dms, m_intras):
        rs = slice(c * CHUNK, (c + 1) * CHUNK)
        sl = slice(h * hd, (h + 1) * hd)
        b_last = b[CHUNK - 1:CHUNK, :]
        pre.append((q_ref[rs, sl], k_ref[rs, sl] * QK_SCALE, v_ref[rs, sl], b, dm, mi, b_last, b_last - b + ig))
    qk = [_dot_nt(p[0], p[1]) for p in pre]
    m_start, m_next = {}, {}
    lanes_of = [(g, h) for g in range(n_seq) for h in range(N_HEADS)]
    for g, h in lanes_of:
        m = m_ref[g, h][:, :1]
        for c in range(g * n_sub, (g + 1) * n_sub):
            _, _, _, _, _, _, b_last, a_end = pre[c * N_HEADS + h]
            m_start[(c, h)] = m
            m = jnp.maximum(b_last + m, jnp.max(a_end, axis=0, keepdims=True))
            m_next[(c, h)] = m
        m_ref[g, h] = jnp.broadcast_to(m, (1, LANES))
    m_t = [jnp.maximum(p[3] + m_start[ch], p[5]) for ch, p in zip(chains, pre)]
    inter = [jnp.exp(p[3] + m_start[ch] - mt) for ch, p, mt in zip(chains, pre, m_t)]
    wmat = [jnp.exp(p[4] - mt) * a for p, mt, a in zip(pre, m_t, qk)]
    wv = [_dot(w, p[2]) for w, p in zip(wmat, pre)]
    wk = [jnp.exp(p[7] - m_next[ch]) * p[1] for ch, p in zip(chains, pre)]
    sc = [jnp.exp(p[6] + m_start[ch] - m_next[ch]) for ch, p in zip(chains, pre)]
    upd = [_dot_tn(a, p[2]) for a, p in zip(wk, pre)]
    c_start, n_start = {}, {}
    for g, h in lanes_of:
        cs = c_ref[g, h]
        ns = n_ref[g, h]
        for c in range(g * n_sub, (g + 1) * n_sub):
            i = c * N_HEADS + h
            c_start[i], n_start[i] = cs, ns
            cs = sc[i] * cs + upd[i]
            ns = sc[i] * ns + jnp.sum(wk[i], axis=0, keepdims=True)
        c_ref[g, h] = cs
        n_ref[g, h] = ns
    qc = [_dot(p[0], c_start[i]) for i, p in enumerate(pre)]
    qn = [jnp.sum(p[0] * n_start[i], axis=-1, keepdims=True) for i, p in enumerate(pre)]
    wsum = [jnp.sum(w, axis=-1, keepdims=True) for w in wmat]
    hh = [(inter[i] * qc[i] + wv[i]) / jnp.maximum(jnp.abs(inter[i] * qn[i] + wsum[i]), jnp.exp(-m_t[i]))
          for i in range(len(chains))]
    ms = [jnp.mean(x * x, axis=-1, keepdims=True) for x in hh]
    outs = [x * lax.rsqrt(v + NORM_EPS) * ng_ref[...] * og[c * CHUNK:(c + 1) * CHUNK, h * hd:(h + 1) * hd]
            for x, v, (c, h) in zip(hh, ms, chains)]
    o_ref[...] = jnp.concatenate(
        [jnp.concatenate(outs[c * N_HEADS:(c + 1) * N_HEADS], axis=-1) for c in range(n_chunks)], axis=0)

    @pl.when(step == pl.num_programs(1) - 1)
    def _():
        cout_ref[...] = c_ref[...]
        nout_ref[...] = n_ref[...]
        mout_ref[...] = m_ref[...]


def _mlstm(z, nb, t_padded, t_valid, ibrow, fbrow, ng, c0, n0, m0):
    assert t_padded % CHUNK == 0
    n_seq, n_sub = _step_shape(nb, t_padded // CHUNK)
    rows = n_seq * n_sub * CHUNK
    nc = t_padded // (n_sub * CHUNK)
    assert n_seq == 1 or nc == 1
    zb = lambda blk: pl.BlockSpec((rows, GROUP_WIDTH), lambda b, n: (b * nc + n, blk))
    st = lambda r, w: pl.BlockSpec((n_seq, N_HEADS, r, w), lambda b, n: (b, 0, 0, 0))
    shp = lambda r, w: jax.ShapeDtypeStruct((nb, N_HEADS, r, w), F32)
    return pl.pallas_call(
        functools.partial(_mlstm_kernel, t_valid=t_valid, t_padded=t_padded, n_sub=n_sub, n_seq=n_seq),
        grid=(nb // n_seq, nc),
        in_specs=[zb(ZB_ML_Q), zb(ZB_ML_K), zb(ZB_ML_V), zb(ZB_ML_O),
                  pl.BlockSpec((rows, LANES), lambda b, n: (b * nc + n, ZB_GATES)),
                  _const_spec((1, LANES)), _const_spec((1, LANES)), _const_spec((1, HEAD_DIM)),
                  st(HEAD_DIM, HEAD_DIM), st(1, HEAD_DIM), st(1, LANES)],
        out_specs=[pl.BlockSpec((rows, GROUP_WIDTH), lambda b, n: (b * nc + n, 0)),
                   st(HEAD_DIM, HEAD_DIM), st(1, HEAD_DIM), st(1, LANES)],
        out_shape=[jax.ShapeDtypeStruct((nb * t_padded, GROUP_WIDTH), F32),
                   shp(HEAD_DIM, HEAD_DIM), shp(1, HEAD_DIM), shp(1, LANES)],
        scratch_shapes=[pltpu.VMEM((n_seq, N_HEADS, HEAD_DIM, HEAD_DIM), F32),
                        pltpu.VMEM((n_seq, N_HEADS, 1, HEAD_DIM), F32),
                        pltpu.VMEM((n_seq, N_HEADS, 1, LANES), F32)],
        compiler_params=_params(("arbitrary", "arbitrary")),
        name="mlstm_chunks",
    )(z, z, z, z, z, ibrow, fbrow, ng, c0, n0, m0)


N_QROWS = 16
PAGES_PER_STEP = 32
ONLINE_GROUP = 8


def _dup_rows(x8):
    return jnp.concatenate([x8, x8], axis=0)


def _col_of_row(row, n):
    r = lax.broadcasted_iota(jnp.int32, (n, row.shape[1]), 0)
    c = lax.broadcasted_iota(jnp.int32, (n, row.shape[1]), 1)
    return jnp.sum(jnp.where(r == c, row, 0.0), axis=1, keepdims=True)


def _row_of_col_padded(col):
    r = lax.broadcasted_iota(jnp.int32, (N_QROWS, LANES), 0)
    c = lax.broadcasted_iota(jnp.int32, (N_QROWS, LANES), 1)
    return jnp.sum(jnp.where(r == c, col, 0.0), axis=0, keepdims=True)


def _decode_kernel(*refs, fox, layer, n_pages, n_pages_step, n_blocks):
    pt_ref = refs[0]
    q_ref, knew_ref, vnew_ref = refs[1:4]
    pos = 4
    lfnew_ref = qt_ref = sfx_ref = lfc_ref = lfbuf = None
    if fox:
        lfnew_ref, sfx_ref = refs[pos], refs[pos + 1]
        pos += 2
    else:
        qt_ref = refs[pos]
        pos += 1
    kc_ref, vc_ref = refs[pos], refs[pos + 1]
    pos += 2
    if fox:
        lfc_ref = refs[pos]
        pos += 1
    o_ref = refs[pos]
    if fox:
        kbuf, vbuf, lfbuf, sem = refs[-4:]
        scratch = refs[pos + 1:-4]
    else:
        kbuf, vbuf, sem = refs[-3:]
        scratch = refs[pos + 1:-3]
    b = pl.program_id(0)
    g = pl.program_id(1)
    steps = pl.num_programs(1)
    last = steps - 1

    def page_copies(bb, gg, slot):
        cps = []
        for i in range(n_pages_step):
            where = gg * n_pages_step + i
            if fox:
                where = n_pages - 1 - where
            page = pt_ref[bb * n_pages + where]
            cps.append(pltpu.make_async_copy(kc_ref.at[layer, page], kbuf.at[slot, i], sem.at[0, slot]))
            cps.append(pltpu.make_async_copy(vc_ref.at[layer, page], vbuf.at[slot, i], sem.at[1, slot]))
            if fox:
                cps.append(pltpu.make_async_copy(lfc_ref.at[layer, page], lfbuf.at[slot, i], sem.at[2, slot]))
        return cps

    step = b * steps + g
    slot = lax.rem(step, 2)

    @pl.when(step == 0)
    def _():
        for cp in page_copies(0, 0, 0):
            cp.start()

    @pl.when(step + 1 < pl.num_programs(0) * steps)
    def _():
        wrap = g == last
        for cp in page_copies(jnp.where(wrap, b + 1, b), jnp.where(wrap, 0, g + 1), 1 - slot):
            cp.start()

    for cp in page_copies(b, g, slot):
        cp.wait()
    k_page = lambda i: kbuf[slot, i]
    v_page = lambda i: vbuf[slot, i]

    row = lax.broadcasted_iota(jnp.int32, (N_QROWS, GROUP_WIDTH), 0)
    lane_w = lax.broadcasted_iota(jnp.int32, (N_QROWS, GROUP_WIDTH), 1)
    qbd = jnp.where(lane_w // HEAD_DIM == row % N_HEADS, q_ref[...], 0.0)
    qbd16 = qbd.astype(BF16)
    tok = lax.broadcasted_iota(jnp.int32, (N_QROWS, LANES), 0) // N_HEADS
    key = lax.broadcasted_iota(jnp.int32, (N_QROWS, LANES), 1)
    new_ok = key <= tok

    def scores(kt_page):
        return _dot(qbd16, kt_page.astype(BF16))

    def weighted_values(p, vt_page):
        return _dot_nt(p.astype(BF16), vt_page.astype(BF16))

    if fox:
        m_ref, l_ref, acc_ref, run_ref, sq_ref = scratch

        def suffix_sums(lf_rows):
            both = _dot3_lhs(lf_rows, sfx_ref[...])
            return both[:, :LANES], both[:, LANES:]

        def online(s_list, v_list):
            mx = s_list[0]
            for s in s_list[1:]:
                mx = jnp.maximum(mx, s)
            m_prev = m_ref[...]
            m_new = jnp.maximum(m_prev, jnp.max(mx, axis=-1, keepdims=True))
            alpha = jnp.exp(m_prev - m_new)
            p_list = [jnp.exp(s - m_new) for s in s_list]
            pv = [weighted_values(p, v_page) for p, v_page in zip(p_list, v_list)]
            acc = acc_ref[...] * jnp.concatenate([alpha, alpha], axis=-1)
            for o in pv:
                acc = acc + o
            acc_ref[...] = acc
            l_ref[...] = l_ref[...] * alpha + jnp.sum(sum(p_list), axis=-1, keepdims=True)
            m_ref[...] = m_new

        @pl.when(g == 0)
        def _():
            m_ref[...] = jnp.full_like(m_ref, NEG)
            l_ref[...] = jnp.zeros_like(l_ref)
            acc_ref[...] = jnp.zeros_like(acc_ref)
            excl, tot = suffix_sums(lfnew_ref[...])
            excl16 = _dup_rows(excl)
            sq = jnp.broadcast_to(jnp.sum(jnp.where(key == tok, excl16, 0.0), axis=-1, keepdims=True),
                                  (N_QROWS, LANES))
            sq_ref[...] = sq
            run_ref[...] = _dup_rows(tot)
            s = jnp.where(new_ok, scores(knew_ref[...]) + excl16 - sq, NEG)
            online([s], [vnew_ref[...]])

        excl_all, tot_all = suffix_sums(lfbuf[slot].reshape(n_pages_step * SUBLANES, LANES))
        raw = [scores(k_page(i)) for i in range(n_pages_step)]
        run = run_ref[...]
        sq = sq_ref[...]
        s_list = []
        for i in range(n_pages_step):
            rs = slice(i * SUBLANES, (i + 1) * SUBLANES)
            s_list.append(raw[i] + _dup_rows(excl_all[rs]) + (run - sq))
            run = run + _dup_rows(tot_all[rs])
        run_ref[...] = run
        for i0 in range(0, n_pages_step, ONLINE_GROUP):
            i1 = min(i0 + ONLINE_GROUP, n_pages_step)
            online(s_list[i0:i1], [v_page(i) for i in range(i0, i1)])

        @pl.when(g == last)
        def _():
            l = l_ref[...]
            o_ref[...] = acc_ref[...] / jnp.concatenate([l, l], axis=-1)
    else:
        opart_ref, mt_ref, lt_ref, gt_ref, mown_ref, lown_ref, oown_ref = scratch
        blk_row = lax.broadcasted_iota(jnp.int32, (n_blocks, LANES), 0)

        @pl.when(g == 0)
        def _():
            mt_ref[...] = jnp.zeros_like(mt_ref)
            lt_ref[...] = jnp.zeros_like(lt_ref)
            gt_ref[...] = jnp.zeros_like(gt_ref)
            s = jnp.where(new_ok, scores(knew_ref[...]), NEG)
            m = jnp.max(s, axis=-1, keepdims=True)
            p = jnp.exp(s - m)
            mown_ref[...] = jnp.broadcast_to(m, (N_QROWS, LANES))
            lown_ref[...] = jnp.broadcast_to(jnp.sum(p, axis=-1, keepdims=True), (N_QROWS, LANES))
            oown_ref[...] = weighted_values(p, vnew_ref[...])

        crow = lax.broadcasted_iota(jnp.int32, (GROUP_WIDTH, LANES), 0)
        qlane = lax.broadcasted_iota(jnp.int32, (GROUP_WIDTH, LANES), 1)
        qbd_t = jnp.where(crow // HEAD_DIM == qlane % N_HEADS, qt_ref[...], 0.0)
        pages_per_block = MOBA_BLOCK // LANES
        blocks_step = n_pages_step // pages_per_block
        raw = [scores(k_page(i)) for i in range(n_pages_step)]
        stats = []
        for bi in range(blocks_step):
            s_pages = raw[bi * pages_per_block:(bi + 1) * pages_per_block]
            mx = s_pages[0]
            for s in s_pages[1:]:
                mx = jnp.maximum(mx, s)
            m = jnp.max(mx, axis=-1, keepdims=True)
            p_pages = [jnp.exp(s - m) for s in s_pages]
            stats.append((m, jnp.sum(sum(p_pages), axis=-1, keepdims=True), p_pages))
        pv = [[weighted_values(p, v_page(bi * pages_per_block + e)) for e, p in enumerate(st[2])]
              for bi, st in enumerate(stats)]
        mt, lt, gt = mt_ref[...], lt_ref[...], gt_ref[...]
        for bi in range(blocks_step):
            n = g * blocks_step + bi
            m, l, _ = stats[bi]
            ksum = sum(k_page(bi * pages_per_block + e) for e in range(pages_per_block))
            kmean = jnp.sum(ksum, axis=-1, keepdims=True) * (1.0 / MOBA_BLOCK)
            gate = jnp.sum(qbd_t * kmean, axis=0, keepdims=True)
            opart_ref[n] = sum(pv[bi])
            here = blk_row == n
            mt = jnp.where(here, _row_of_col_padded(m), mt)
            lt = jnp.where(here, _row_of_col_padded(l), lt)
            gt = jnp.where(here, gate, gt)
        mt_ref[...] = mt
        lt_ref[...] = lt
        gt_ref[...] = gt

        @pl.when(g == last)
        def _():
            gt = gt_ref[...]
            sel = jnp.zeros(gt.shape, F32)
            for _ in range(MOBA_TOPK):
                mx = jnp.max(gt, axis=0, keepdims=True)
                first = jnp.min(jnp.where(gt == mx, blk_row, n_blocks), axis=0, keepdims=True)
                pick = blk_row == first
                sel = jnp.where(pick, 1.0, sel)
                gt = jnp.where(pick, -jnp.inf, gt)
            mt = jnp.where(sel > 0.0, mt_ref[...], NEG)
            m_own = _row_of_col_padded(mown_ref[:, :1])
            l_own = _row_of_col_padded(lown_ref[:, :1])
            m_all = jnp.maximum(jnp.max(mt, axis=0, keepdims=True), m_own)
            w = jnp.where(sel > 0.0, jnp.exp(mt - m_all), 0.0)
            w_own = jnp.exp(m_own - m_all)
            denom = jnp.sum(w * lt_ref[...], axis=0, keepdims=True) + w_own * l_own
            w = w / denom
            out = oown_ref[...] * _col_of_row(w_own / denom, N_QROWS)
            for n in range(n_blocks):
                out = out + opart_ref[n] * _col_of_row(w[n:n + 1, :], N_QROWS)
            o_ref[...] = out


def _decode(page_table, layer, q16, knew_t, vnew_t, extra, cache_kt, cache_vt, cache_lf8, fox):
    nb, n_pages = page_table.shape
    pps = PAGES_PER_STEP
    assert n_pages % pps == 0 and cache_kt.shape[2:] == (GROUP_WIDTH, LANES)
    n_blocks = n_pages * LANES // MOBA_BLOCK
    assert (n_pages * LANES) % MOBA_BLOCK == 0 and n_blocks >= MOBA_TOPK and n_blocks % SUBLANES == 0
    steps = n_pages // pps
    hbm = pl.BlockSpec(memory_space=pl.ANY)
    seq = lambda r, w: pl.BlockSpec((None, r, w), lambda b, g, pt: (b, 0, 0))
    in_specs = [seq(N_QROWS, GROUP_WIDTH), seq(GROUP_WIDTH, LANES), seq(GROUP_WIDTH, LANES),
                seq(SUBLANES if fox else GROUP_WIDTH, LANES)]
    args = [q16, knew_t, vnew_t, extra]
    if fox:
        pos_i = jnp.arange(LANES)
        later = (pos_i[:, None] > pos_i[None, :]).astype(BF16)
        in_specs.append(pl.BlockSpec((LANES, 2 * LANES), lambda b, g, pt: (0, 0)))
        args.append(jnp.concatenate([later, jnp.ones((LANES, LANES), BF16)], axis=1))
    in_specs += [hbm, hbm]
    args += [cache_kt, cache_vt]
    ring = [pltpu.VMEM((2, pps, GROUP_WIDTH, LANES), F32), pltpu.VMEM((2, pps, GROUP_WIDTH, LANES), F32)]
    if fox:
        in_specs.append(hbm)
        args.append(cache_lf8)
        scratch = [pltpu.VMEM((N_QROWS, LANES), F32), pltpu.VMEM((N_QROWS, LANES), F32),
                   pltpu.VMEM((N_QROWS, GROUP_WIDTH), F32), pltpu.VMEM((N_QROWS, LANES), F32),
                   pltpu.VMEM((N_QROWS, LANES), F32)]
        ring.append(pltpu.VMEM((2, pps, SUBLANES, LANES), F32))
    else:
        scratch = [pltpu.VMEM((n_blocks, N_QROWS, GROUP_WIDTH), F32)] + [pltpu.VMEM((n_blocks, LANES), F32)] * 3 + [
            pltpu.VMEM((N_QROWS, LANES), F32), pltpu.VMEM((N_QROWS, LANES), F32),
            pltpu.VMEM((N_QROWS, GROUP_WIDTH), F32)]
    scratch = scratch + ring + [pltpu.SemaphoreType.DMA((len(ring), 2))]
    return pl.pallas_call(
        functools.partial(_decode_kernel, fox=fox, layer=layer, n_pages=n_pages, n_pages_step=pps, n_blocks=n_blocks),
        grid_spec=pltpu.PrefetchScalarGridSpec(
            num_scalar_prefetch=1, grid=(nb, steps), in_specs=in_specs,
            out_specs=seq(N_QROWS, GROUP_WIDTH), scratch_shapes=scratch),
        out_shape=jax.ShapeDtypeStruct((nb, N_QROWS, GROUP_WIDTH), F32),
        compiler_params=_params(("arbitrary", "arbitrary")),
        name="fox_decode" if fox else "moba_decode",
    )(page_table.reshape(-1), *args)


def _pack_w_in(w):
    gw, nh = GROUP_WIDTH, N_HEADS
    widths = [('fox_q', gw), ('fox_k', gw), ('fox_v', gw), ('fox_f', nh), ('gdn_qkv', 3 * gw), ('gdn_a', nh),
              ('gdn_b', nh), ('gdn_g', gw), ('moba_q', gw), ('moba_k', gw), ('moba_v', gw), ('mlstm_q', gw),
              ('mlstm_k', gw), ('mlstm_v', gw), ('mlstm_i', nh), ('mlstm_f', nh), ('mlstm_o', gw)]
    wt = w.T
    cols, off = {}, 0
    for name, width in widths:
        cols[name] = wt[off:off + width]
        off += width
    assert off == wt.shape[0]
    wide = [cols[k] for k in ('fox_q', 'fox_k', 'fox_v', 'gdn_qkv', 'gdn_g', 'moba_q', 'moba_k', 'moba_v',
                              'mlstm_q', 'mlstm_k', 'mlstm_v', 'mlstm_o')]
    gates = [cols[k] for k in ('fox_f', 'gdn_a', 'gdn_b', 'mlstm_i', 'mlstm_f')]
    pad = jnp.zeros((LANES - 5 * nh, wt.shape[1]), w.dtype)
    return jnp.concatenate(wide + gates + [pad], axis=0).astype(BF16)


def _gate_row(vec, lane0):
    return jnp.zeros((1, LANES), F32).at[0, lane0:lane0 + N_HEADS].set(vec.astype(F32))


def _tile_heads(vec):
    return jnp.tile(vec.astype(F32), N_HEADS).reshape(1, GROUP_WIDTH)


def _rope_tables(pos):
    half = HEAD_DIM // 2
    inv_freq = jnp.power(ROPE_THETA, -jnp.arange(half, dtype=F32) / half)
    ang = pos.astype(F32)[:, None] * inv_freq[None, :]
    cos = jnp.cos(ang)
    sin = jnp.sin(ang)
    cos_h = jnp.concatenate([cos, cos], axis=-1)
    sin_h = jnp.concatenate([-sin, sin], axis=-1)
    return jnp.tile(cos_h, (1, N_HEADS)), jnp.tile(sin_h, (1, N_HEADS))


def _head_sum_matrix():
    lane = jnp.arange(GROUP_WIDTH)
    return (lane[:, None] // HEAD_DIM == lane[None, :] // HEAD_DIM).astype(BF16)


def _lower_tri(n):
    r = jnp.arange(n)
    return (r[:, None] >= r[None, :]).astype(BF16)


def _pad_rows(a, rows, front=0):
    return jnp.pad(a, ((0, 0), (front, rows - front - a.shape[1]), (0, 0)))


ROW_TILE = 512
ATTN_Q_TILE = 512
ATTN_K_TILE = 512


def _layer_weights(w_in, w_out, w_gate, w_up, w_down):
    return (_pack_w_in(w_in), w_out.astype(BF16), w_gate.astype(BF16), w_up.astype(BF16), w_down.astype(BF16))


def _zcols(z, blk, width=GROUP_WIDTH):
    return z[:, blk * GROUP_WIDTH:blk * GROUP_WIDTH + width]


def _prompt_layer(x, lw, lp, consts):
    nb, t, d = x.shape
    wp, wo, wg, wu, wd = lw
    x2 = x.reshape(nb * t, d)
    z = _proj(x2, lp['ln1'], wp, ROW_TILE)
    fqa, fka, fva, fk_ct, fv_ct, flf_ct, mqa, mka, mva, mk_ct, mv_ct = _prep_prompt(
        z, nb, t, lp['attn_norms'], lp['fox_fb_row'], consts['cos_p'], consts['sin_p'], consts['gsum'], consts['tri'])
    tq, tk = min(ATTN_Q_TILE, t), min(ATTN_K_TILE, t)
    o_a = _flash(fqa, fka, fva, nb, t, tq, tk)
    o_c = _flash(mqa, mka, mva, nb, t, tq, tk)
    zeros_state = jnp.zeros((nb, N_HEADS, HEAD_DIM, HEAD_DIM), F32)
    o_b, gdn_s = _gdn(z, nb, t, t, jnp.zeros((nb, SUBLANES, 3 * GROUP_WIDTH), F32), lp['conv_w8'],
                      lp['gdn_a_row'], lp['gdn_dt_row'], lp['gdn_ng'], zeros_state)
    o_d, ml_c, ml_n, ml_m = _mlstm(z, nb, t, t, lp['ml_ib_row'], lp['ml_fb_row'], lp['ml_ng'], zeros_state,
                                   jnp.zeros((nb, N_HEADS, 1, HEAD_DIM), F32), jnp.zeros((nb, N_HEADS, 1, LANES), F32))
    y = _out_ffn(x2, (o_a, o_b, o_c, o_d), wo, lp['ln2'], wg, wu, wd, ROW_TILE).reshape(nb, t, d)
    heads = lambda a: jnp.transpose(a.reshape(nb, N_HEADS, HEAD_DIM, t), (0, 3, 1, 2))
    z3 = z.reshape(nb, t, Z_WIDTH)
    qkv0 = ZB_GDN_QKV * GROUP_WIDTH
    states = (heads(fk_ct), heads(fv_ct), jnp.swapaxes(flf_ct[:, :N_HEADS], 1, 2),
              heads(mk_ct), heads(mv_ct),
              z3[:, t - (GDN_CONV - 1):, qkv0:qkv0 + 3 * GROUP_WIDTH], gdn_s,
              ml_c, ml_n[:, :, 0, :], ml_m[:, :, 0, 0])
    return y, states


def _sample_layer(x, layer, lw, lp, consts, caches, page_table, st):
    nb, t, d = x.shape
    assert t <= N_HEADS and t >= GDN_CONV - 1 and nb * t % SUBLANES == 0
    wp, wo, wg, wu, wd = lw
    x2 = x.reshape(nb * t, d)
    z = _proj(x2, lp['ln1'], wp, nb * t)
    fq32, fk32, flf, mq32, mk32 = _prep_sample(z, lp['attn_norms'], lp['fox_fb_row'], consts['cos_s'], consts['sin_s'],
                                               consts['gsum'])
    seq = lambda a: a.reshape(nb, t, a.shape[-1])
    tok_rows = lambda a: _pad_rows(jnp.repeat(seq(a), N_HEADS, axis=1), N_QROWS)
    new_page = lambda a: jnp.pad(jnp.swapaxes(seq(a), 1, 2), ((0, 0), (0, 0), (0, LANES - t)))
    rows_t = lambda a: jnp.pad(jnp.swapaxes(tok_rows(a), 1, 2), ((0, 0), (0, 0), (0, LANES - N_QROWS)))
    lf_new = jnp.swapaxes(seq(flf[:, :N_HEADS]), 1, 2)
    lf_new = jnp.pad(lf_new, ((0, 0), (0, 0), (0, LANES - t)))
    lf_new8 = jnp.concatenate([lf_new, lf_new], axis=1)
    fox_k, fox_v, fox_lf8, moba_k, moba_v = caches

    def own_head(o16):
        o5 = o16.reshape(nb, N_HEADS, N_HEADS, N_HEADS, HEAD_DIM)[:, :t]
        hh = jnp.arange(N_HEADS)
        return o5[:, :, hh, hh, :].reshape(nb * t, GROUP_WIDTH)

    o_a = own_head(_decode(page_table, layer, tok_rows(fq32), new_page(fk32), new_page(_zcols(z, ZB_FOX_V)),
                           lf_new8, fox_k, fox_v, fox_lf8, True))
    o_c = own_head(_decode(page_table, layer, tok_rows(mq32), new_page(mk32), new_page(_zcols(z, ZB_MOBA_V)),
                           rows_t(mq32), moba_k, moba_v, None, False))
    zp = _pad_rows(seq(z), CHUNK).reshape(nb * CHUNK, Z_WIDTH)
    conv_buf, gdn_s0, ml_c0, ml_n0, ml_m0 = st
    real_rows = lambda a: a.reshape(nb, CHUNK, GROUP_WIDTH)[:, :t].reshape(nb * t, GROUP_WIDTH)
    o_b, gdn_s = _gdn(zp, nb, CHUNK, t, _pad_rows(conv_buf, SUBLANES, front=SUBLANES - (GDN_CONV - 1)), lp['conv_w8'],
                      lp['gdn_a_row'], lp['gdn_dt_row'], lp['gdn_ng'], gdn_s0)
    o_d, ml_c, ml_n, ml_m = _mlstm(zp, nb, CHUNK, t, lp['ml_ib_row'], lp['ml_fb_row'], lp['ml_ng'], ml_c0,
                                   ml_n0[:, :, None, :],
                                   jnp.broadcast_to(ml_m0[:, :, None, None], (nb, N_HEADS, 1, LANES)))
    y = _out_ffn(x2, (o_a, real_rows(o_b), o_c, real_rows(o_d)), wo, lp['ln2'], wg, wu, wd, nb * t).reshape(nb, t, d)
    heads = lambda a: a.reshape(nb, t, N_HEADS, HEAD_DIM)
    qkv0 = ZB_GDN_QKV * GROUP_WIDTH
    conv_rows = jnp.concatenate([conv_buf, seq(z)[:, :, qkv0:qkv0 + 3 * GROUP_WIDTH]], axis=1)
    states = (heads(fk32), heads(_zcols(z, ZB_FOX_V)), seq(flf[:, :N_HEADS]),
              heads(mk32), heads(_zcols(z, ZB_MOBA_V)),
              conv_rows[:, conv_rows.shape[1] - (GDN_CONV - 1):], gdn_s,
              ml_c, ml_n[:, :, 0, :], ml_m[:, :, 0, 0])
    return y, states


def kernel(x_prompt, x_sample, cache_fox_k, cache_fox_v, cache_fox_logf, cache_moba_k, cache_moba_v,
           state_gdn_conv, state_gdn_s, state_mlstm_c, state_mlstm_n, state_mlstm_m, page_table,
           ln1, w_in, fox_q_norm, fox_k_norm, fox_f_bias, gdn_conv_w, gdn_a_log, gdn_dt_bias, gdn_norm,
           moba_q_norm, moba_k_norm, mlstm_i_bias, mlstm_f_bias, mlstm_norm, w_out, ln2,
           w_gate, w_up, w_down):
    depth = w_in.shape[0]
    t_prompt = x_prompt.shape[1]
    nb_s, t_sample = x_sample.shape[:2]
    n_pool, page = cache_fox_k.shape[1:3]
    past_len = page_table.shape[1] * page
    assert page == LANES and past_len % MOBA_BLOCK == 0
    cos_p, sin_p = _rope_tables(jnp.arange(t_prompt, dtype=jnp.int32))
    cos_s, sin_s = _rope_tables(past_len + jnp.arange(t_sample, dtype=jnp.int32))
    consts = {'cos_p': cos_p, 'sin_p': sin_p, 'cos_s': jnp.tile(cos_s, (nb_s, 1)), 'sin_s': jnp.tile(sin_s, (nb_s, 1)),
              'gsum': _head_sum_matrix(), 'tri': _lower_tri(MOBA_BLOCK)}
    pages = lambda c: jnp.transpose(c, (0, 1, 3, 4, 2)).reshape(depth, n_pool, GROUP_WIDTH, page)
    lf_t = jnp.swapaxes(cache_fox_logf, 2, 3)
    caches = (pages(cache_fox_k), pages(cache_fox_v), jnp.concatenate([lf_t, lf_t], axis=2),
              pages(cache_moba_k), pages(cache_moba_v))
    y_p, y_s = x_prompt, x_sample
    p_states, s_states = [], []
    for i in range(depth):
        lw = _layer_weights(w_in[i], w_out[i], w_gate[i], w_up[i], w_down[i])
        lp = {
            'ln1': ln1[i], 'ln2': ln2[i],
            'attn_norms': [_tile_heads(v[i]) for v in (fox_q_norm, fox_k_norm, moba_q_norm, moba_k_norm)],
            'fox_fb_row': _gate_row(fox_f_bias[i], GL_FOX_F),
            'conv_w8': jnp.pad(gdn_conv_w[i].astype(F32), ((0, SUBLANES - GDN_CONV), (0, 0))),
            'gdn_a_row': _gate_row(gdn_a_log[i], GL_GDN_A), 'gdn_dt_row': _gate_row(gdn_dt_bias[i], GL_GDN_A),
            'gdn_ng': gdn_norm[i].reshape(1, HEAD_DIM).astype(F32),
            'ml_ib_row': _gate_row(mlstm_i_bias[i], GL_ML_I), 'ml_fb_row': _gate_row(mlstm_f_bias[i], GL_ML_F),
            'ml_ng': mlstm_norm[i].reshape(1, HEAD_DIM).astype(F32),
        }
        y_p, st_p = _prompt_layer(y_p, lw, lp, consts)
        st = (state_gdn_conv[i], state_gdn_s[i], state_mlstm_c[i], state_mlstm_n[i], state_mlstm_m[i])
        y_s, st_s = _sample_layer(y_s, i, lw, lp, consts, caches, page_table, st)
        p_states.append(st_p)
        s_states.append(st_s)
    stack = lambda states: [jnp.stack(a) for a in zip(*states)]
    return (y_p, y_s, *stack(p_states), *stack(s_states))
```
